```python
import jax, jax.numpy as jnp
from jax import lax
import numpy as np

D_MODEL = 2048
BATCH = 2
SEQ = 4096
DEPTH = 2

N_BRANCH = 3
BRANCH_W = 1024
CONV_W = BRANCH_W
CONV_K = 3
ATT_HEADS = 8
ATT_HD = 128
ATT_W = ATT_HEADS * ATT_HD
ATT_BLOCK = 128
RWKV_HEADS = 16
RWKV_HD = 64
RWKV_W = RWKV_HEADS * RWKV_HD
DECAY_LORA = 96
AAA_LORA = 96
GATE_LORA = 256
RWKV_SHIFT_W = 3 * RWKV_W + DECAY_LORA + AAA_LORA + GATE_LORA
RWKV_SPLIT = (RWKV_W, 2 * RWKV_W, 3 * RWKV_W, 3 * RWKV_W + DECAY_LORA, 3 * RWKV_W + DECAY_LORA + AAA_LORA)
IN_COLS = 3 * CONV_W + 3 * ATT_W + RWKV_SHIFT_W + N_BRANCH * D_MODEL
SPLIT_IN = (3 * CONV_W, 3 * CONV_W + 3 * ATT_W, 3 * CONV_W + 3 * ATT_W + RWKV_SHIFT_W)
N_EXPERTS = 64
TOP_K = 8
N_GROUPS = 8
TOPK_GROUPS = 4
EXPERT_FF = 512
SHARED_FF = 512
ROUTED_SCALE = 2.5
LN_EPS = 1e-5
GN_EPS = 64e-5
DEEPNORM_ALPHA = (2 * DEPTH) ** 0.25
DEEPNORM_BETA = (8 * DEPTH) ** -0.25

kernel_name = 'hybrid_conv_stickbreak_rwkv7_moe_deepnorm'


def layer_norm(x, g, b):
    xf = x.astype(jnp.float32)
    mu = xf.mean(-1, keepdims=True)
    var = jnp.square(xf - mu).mean(-1, keepdims=True)
    return ((xf - mu) * lax.rsqrt(var + LN_EPS) * g + b).astype(x.dtype)


def short_conv(u, conv_w):
    return lax.conv_general_dilated(
        u, conv_w[:, None, :].astype(u.dtype), window_strides=(1,),
        padding=[(CONV_K - 1, 0)], dimension_numbers=('NWC', 'WIO', 'NWC'),
        feature_group_count=u.shape[-1])


def stick_breaking_attention(q, k, v):
    seq = q.shape[2]
    scale = ATT_HD ** -0.5
    outs = []
    for blk in range(seq // ATT_BLOCK):
        q0 = blk * ATT_BLOCK
        end = q0 + ATT_BLOCK
        z = jnp.einsum('bhqd,bhkd->bhqk', q[:, :, q0:end], k[:, :, :end],
                       preferred_element_type=jnp.float32) * scale
        causal = jnp.arange(end)[None, :] < (q0 + jnp.arange(ATT_BLOCK))[:, None]
        log_beta = jax.nn.log_sigmoid(z)
        log_1mb = jnp.where(causal, jax.nn.log_sigmoid(-z), 0.0)
        rev = lax.cumsum(log_1mb, axis=3, reverse=True)
        tail = jnp.concatenate([rev[..., 1:], jnp.zeros_like(rev[..., :1])], axis=-1)
        w = jnp.where(causal, jnp.exp(log_beta + tail), 0.0)
        outs.append(jnp.einsum('bhqk,bhkd->bhqd', w.astype(v.dtype), v[:, :, :end]))
    return jnp.concatenate(outs, axis=2)


def token_shift(p):
    return jnp.pad(p, ((0, 0), (1, 0), (0, 0)))[:, :-1]


def rwkv7_scan(r, w, k, v, a, b):
    bsz, _, nh, n = r.shape

    def step(state, inp):
        r_t, w_t, k_t, v_t, a_t, b_t = inp
        sa = jnp.einsum('bhij,bhj->bhi', state, a_t)
        state = (state * w_t[:, :, None, :] + sa[..., None] * b_t[:, :, None, :]
                 + v_t[..., None] * k_t[:, :, None, :])
        return state, jnp.einsum('bhij,bhj->bhi', state, r_t)

    xs = tuple(jnp.moveaxis(t.astype(jnp.float32), 1, 0) for t in (r, w, k, v, a, b))
    s0 = jnp.zeros((bsz, nh, n, n), jnp.float32)
    _, y = lax.scan(step, s0, xs)
    return jnp.moveaxis(y, 0, 1)


def rwkv7_mixer(p, mix, w0, w2, a0, a2, g2, k_k, k_a, r_k, lnx_g, lnx_b):
    bsz, seq, _ = p.shape
    p = p + (token_shift(p) - p) * mix
    r, k, v, lw, la, lg = jnp.split(p, RWKV_SPLIT, axis=-1)
    w_log = -jax.nn.softplus(-(w0 + jnp.tanh(lw) @ w2)) - 0.5
    decay = jnp.exp(-jnp.exp(w_log.astype(jnp.float32)))
    a = jax.nn.sigmoid(a0 + la @ a2)
    g = jax.nn.sigmoid(lg) @ g2
    heads = lambda t: t.reshape(bsz, seq, RWKV_HEADS, RWKV_HD)
    kk = heads(k * k_k).astype(jnp.float32)
    kk = kk / jnp.maximum(jnp.sqrt(jnp.sum(kk * kk, axis=-1, keepdims=True)), 1e-12)
    k = k * (1 + (a - 1) * k_a)
    r_h, k_h, v_h, a_h = heads(r), heads(k), heads(v), heads(a)
    y = rwkv7_scan(r_h, heads(decay), k_h, v_h, -kk, kk * a_h)
    mu = y.mean(-1, keepdims=True)
    var = jnp.square(y - mu).mean(-1, keepdims=True)
    y = ((y - mu) * lax.rsqrt(var + GN_EPS)).reshape(bsz, seq, RWKV_W) * lnx_g + lnx_b
    bonus = jnp.sum(r_h * k_h * r_k, axis=-1, keepdims=True) * v_h
    y = y + bonus.reshape(bsz, seq, RWKV_W)
    return (y * g).astype(p.dtype)


def moe_ffn(x, w_router, router_bias, w_e_gate, w_e_up, w_e_down, w_s_gate, w_s_up, w_s_down):
    bsz, seq, d = x.shape
    xt = x.reshape(-1, d)
    n_tok = xt.shape[0]
    scores = jax.nn.sigmoid((xt @ w_router).astype(jnp.float32))
    biased = scores + router_bias
    grp = biased.reshape(n_tok, N_GROUPS, N_EXPERTS // N_GROUPS)
    grp_score = lax.top_k(grp, 2)[0].sum(-1)
    _, grp_idx = lax.top_k(grp_score, TOPK_GROUPS)
    grp_mask = jnp.any(grp_idx[..., None] == jnp.arange(N_GROUPS), axis=1)
    exp_mask = jnp.repeat(grp_mask, N_EXPERTS // N_GROUPS, axis=1)
    _, top_i = lax.top_k(jnp.where(exp_mask, biased, -jnp.inf), TOP_K)
    top_s = jnp.take_along_axis(scores, top_i, axis=1)
    top_w = top_s / jnp.sum(top_s, axis=-1, keepdims=True) * ROUTED_SCALE
    flat_e = top_i.reshape(-1)
    order = jnp.argsort(flat_e)
    tok = order // TOP_K
    sizes = jnp.bincount(flat_e, length=N_EXPERTS).astype(jnp.int32)
    xs = xt[tok]
    h = jax.nn.silu(lax.ragged_dot(xs, w_e_gate, sizes)) * lax.ragged_dot(xs, w_e_up, sizes)
    ys = lax.ragged_dot(h, w_e_down, sizes) * top_w.reshape(-1)[order][:, None].astype(x.dtype)
    routed = jax.ops.segment_sum(ys, tok, num_segments=n_tok)
    shared = (jax.nn.silu(xt @ w_s_gate) * (xt @ w_s_up)) @ w_s_down
    return (routed + shared).reshape(bsz, seq, d)


def setup_inputs(seed: int = 0) -> dict:
    key = jax.random.key(seed)
    ks = iter(jax.random.split(key, 40))
    L, D = DEPTH, D_MODEL
    beta = DEEPNORM_BETA

    def nrm(shape, scale):
        return jax.random.normal(next(ks), shape, jnp.float32) * scale

    def unif(shape, lo, hi):
        return jax.random.uniform(next(ks), shape, jnp.float32, lo, hi)

    return {
        'x': nrm((BATCH, SEQ, D), 1.0),
        'w_in': nrm((L, D, IN_COLS), D ** -0.5),
        'conv_w': nrm((L, CONV_K, CONV_W), CONV_K ** -0.5),
        'rwkv_mix': unif((L, RWKV_SHIFT_W), 0.0, 1.0),
        'rwkv_w0': unif((L, RWKV_W), -6.0, -1.0),
        'rwkv_w2': nrm((L, DECAY_LORA, RWKV_W), DECAY_LORA ** -0.5),
        'rwkv_a0': nrm((L, RWKV_W), 0.1),
        'rwkv_a2': nrm((L, AAA_LORA, RWKV_W), AAA_LORA ** -0.5),
        'rwkv_g2': nrm((L, GATE_LORA, RWKV_W), GATE_LORA ** -0.5),
        'rwkv_kk': 0.85 + nrm((L, RWKV_W), 0.02),
        'rwkv_ka': 1.0 + nrm((L, RWKV_W), 0.02),
        'rwkv_rk': nrm((L, RWKV_HEADS, RWKV_HD), 0.1),
        'rwkv_lnx_g': 1.0 + nrm((L, RWKV_W), 0.02),
        'rwkv_lnx_b': nrm((L, RWKV_W), 0.02),
        'w_branch': nrm((L, N_BRANCH, BRANCH_W, D), beta * BRANCH_W ** -0.5),
        'w_o': nrm((L, D, D), beta * D ** -0.5),
        'ln1_g': 1.0 + nrm((L, D), 0.02),
        'ln1_b': nrm((L, D), 0.02),
        'w_router': nrm((L, D, N_EXPERTS), D ** -0.5),
        'router_bias': nrm((L, N_EXPERTS), 0.01),
        'w_e_gate': nrm((L, N_EXPERTS, D, EXPERT_FF), D ** -0.5),
        'w_e_up': nrm((L, N_EXPERTS, D, EXPERT_FF), D ** -0.5),
        'w_e_down': nrm((L, N_EXPERTS, EXPERT_FF, D), beta * EXPERT_FF ** -0.5),
        'w_s_gate': nrm((L, D, SHARED_FF), D ** -0.5),
        'w_s_up': nrm((L, D, SHARED_FF), D ** -0.5),
        'w_s_down': nrm((L, SHARED_FF, D), beta * SHARED_FF ** -0.5),
        'ln2_g': 1.0 + nrm((L, D), 0.02),
        'ln2_b': nrm((L, D), 0.02),
    }


def reference(x, w_in, conv_w, rwkv_mix, rwkv_w0, rwkv_w2, rwkv_a0, rwkv_a2, rwkv_g2,
              rwkv_kk, rwkv_ka, rwkv_rk, rwkv_lnx_g, rwkv_lnx_b, w_branch, w_o, ln1_g, ln1_b,
              w_router, router_bias, w_e_gate, w_e_up, w_e_down, w_s_gate, w_s_up, w_s_down,
              ln2_g, ln2_b):
    bsz, seq, _ = x.shape
    for l in range(DEPTH):
        p = x @ w_in[l]
        conv_p, att_p, rwkv_p, gate_p = jnp.split(p, SPLIT_IN, axis=-1)
        gb, gc, hc = jnp.split(conv_p, 3, axis=-1)
        y_conv = gb * short_conv(gc * hc, conv_w[l])
        q, k, v = (t.reshape(bsz, seq, ATT_HEADS, ATT_HD).transpose(0, 2, 1, 3)
                   for t in jnp.split(att_p, 3, axis=-1))
        y_att = stick_breaking_attention(q, k, v).transpose(0, 2, 1, 3).reshape(bsz, seq, ATT_W)
        y_rwkv = rwkv7_mixer(rwkv_p, rwkv_mix[l], rwkv_w0[l], rwkv_w2[l], rwkv_a0[l], rwkv_a2[l],
                             rwkv_g2[l], rwkv_kk[l], rwkv_ka[l], rwkv_rk[l],
                             rwkv_lnx_g[l], rwkv_lnx_b[l])
        branches = jnp.stack([y_conv, y_att, y_rwkv], axis=2)
        gates = jax.nn.sigmoid(gate_p.reshape(bsz, seq, N_BRANCH, D_MODEL))
        merged = jnp.sum(gates * jnp.einsum('bsnc,ncd->bsnd', branches, w_branch[l]), axis=2)
        x = layer_norm(DEEPNORM_ALPHA * x + merged @ w_o[l], ln1_g[l], ln1_b[l])
        y_moe = moe_ffn(x, w_router[l], router_bias[l], w_e_gate[l], w_e_up[l], w_e_down[l],
                        w_s_gate[l], w_s_up[l], w_s_down[l])
        x = layer_norm(DEEPNORM_ALPHA * x + y_moe, ln2_g[l], ln2_b[l])
    return x
```

```python
import functools

import jax
import jax.numpy as jnp
from jax import lax
from jax.experimental import pallas as pl
from jax.experimental.pallas import tpu as pltpu

F32 = jnp.float32
BF16 = jnp.bfloat16
I32 = jnp.int32
U32 = jnp.uint32

D_MODEL = 2048
DEPTH = 2
BRANCH_W = 1024
CONV_K = 3
ATT_HEADS = 8
ATT_HD = 128
RWKV_HEADS = 16
RWKV_HD = 64
RWKV_W = RWKV_HEADS * RWKV_HD
DECAY_LORA = 96
AAA_LORA = 96
GATE_LORA = 256
LORA_PAD = 128
N_EXPERTS = 64
TOP_K = 8
N_GROUPS = 8
GROUP_SIZE = N_EXPERTS // N_GROUPS
TOPK_GROUPS = 4
EXPERT_FF = 512
SHARED_FF = 512
ROUTED_SCALE = 2.5
LN_EPS = 1e-5
GN_EPS = 64e-5
DEEPNORM_ALPHA = (2 * DEPTH) ** 0.25

COL_CONV = 0
COL_ATT = 3 * BRANCH_W
COL_GATE = 6 * BRANCH_W
COL_RWKV = COL_GATE + 3 * D_MODEL
RWKV_COLS = 3 * RWKV_W + 2 * LORA_PAD + GATE_LORA
IN_COLS_PAD = COL_RWKV + RWKV_COLS

RWKV_CHUNK = 64
RWKV_GROUP = 4
RWKV_GW = RWKV_GROUP * RWKV_HD
ATT_TILE = 256
GMM_TILE = 256
LANES = 128
PACK_W = D_MODEL // 2
PACK_SUB = PACK_W // LANES
ROW_SUB = D_MODEL // LANES

VMEM_LIMIT = 56 * 1024 * 1024

NN = ((1,), (0,))
NT = ((1,), (1,))


def _cparams(n_axes):
    return pltpu.CompilerParams(dimension_semantics=("arbitrary",) * n_axes,
                                vmem_limit_bytes=VMEM_LIMIT)


def _dg(a, b, dims=NN):
    return lax.dot_general(a, b, (dims, ((), ())), preferred_element_type=F32)


def _split2(x):
    hi = x.astype(BF16)
    lo = (x - hi.astype(F32)).astype(BF16)
    return hi, lo


def _split3(x):
    hi = x.astype(BF16)
    r1 = x - hi.astype(F32)
    mid = r1.astype(BF16)
    lo = (r1 - mid.astype(F32)).astype(BF16)
    return hi, mid, lo


def _mm3(a, b, dims=NN):
    ah, al = _split2(a)
    bh, bl = _split2(b)
    return _dg(ah, bh, dims) + _dg(ah, bl, dims) + _dg(al, bh, dims)


def _sigmoid(x):
    return 1.0 / (1.0 + jnp.exp(-x))


def _softplus(x):
    return jnp.maximum(x, 0.0) + jnp.log(1.0 + jnp.exp(-jnp.abs(x)))


def _mm_kernel(x_ref, w_ref, o_ref):
    o_ref[...] = _dg(x_ref[...], w_ref[...]).astype(o_ref.dtype)


def _matmul_cols(x, w, col_off, ncols, tn, tm, out_dtype):
    m, k = x.shape
    off = col_off // tn
    return pl.pallas_call(
        _mm_kernel,
        grid=(ncols // tn, m // tm),
        in_specs=[pl.BlockSpec((tm, k), lambda j, i: (i, 0)),
                  pl.BlockSpec((k, tn), lambda j, i: (0, off + j))],
        out_specs=pl.BlockSpec((tm, tn), lambda j, i: (i, j)),
        out_shape=jax.ShapeDtypeStruct((m, ncols), out_dtype),
        compiler_params=_cparams(2),
        name="in_proj",
    )(x, w)


def _conv_kernel(gb_ref, gc_ref, hc_ref, gch_ref, hch_ref, cw_ref, o_ref):
    i = pl.program_id(1)
    u = gc_ref[...] * hc_ref[...]
    uh = gch_ref[...] * hch_ref[...]
    uh = jnp.where(i > 0, uh, 0.0)
    row = lax.broadcasted_iota(I32, u.shape, 0)
    u1 = jnp.where(row == 0, uh[7:8, :], pltpu.roll(u, 1, axis=0))
    u2 = pltpu.roll(u, 2, axis=0)
    u2 = jnp.where(row == 0, uh[6:7, :], jnp.where(row == 1, uh[7:8, :], u2))
    cw = cw_ref[...]
    y = gb_ref[...] * (cw[2:3, :] * u + cw[1:2, :] * u1 + cw[0:1, :] * u2)
    o_ref[...] = y.astype(o_ref.dtype)


def _short_conv(p_conv, conv_w, bsz, seq, ts=512):
    p3 = p_conv.reshape(bsz, seq, 3 * BRANCH_W)
    hb = ts // 8
    halo = lambda c: pl.BlockSpec((None, 8, BRANCH_W),
                                  lambda b, i: (b, jnp.maximum(i * hb - 1, 0), c))
    main = lambda c: pl.BlockSpec((None, ts, BRANCH_W), lambda b, i: (b, i, c))
    return pl.pallas_call(
        _conv_kernel,
        grid=(bsz, seq // ts),
        in_specs=[main(0), main(1), main(2), halo(1), halo(2),
                  pl.BlockSpec((CONV_K, BRANCH_W), lambda b, i: (0, 0))],
        out_specs=pl.BlockSpec((None, ts, BRANCH_W), lambda b, i: (b, i, 0)),
        out_shape=jax.ShapeDtypeStruct((bsz, seq, BRANCH_W), BF16),
        compiler_params=_cparams(2),
        name="short_conv",
    )(p3, p3, p3, p3, p3, conv_w)


def _attn_kernel(q_ref, k_ref, v_ref, o_ref, *, tq, scale):
    i = pl.program_id(2)
    q = q_ref[...]
    r_io = lax.broadcasted_iota(I32, (tq, tq), 0)
    c_io = lax.broadcasted_iota(I32, (tq, tq), 1)
    suffix = (r_io > c_io).astype(BF16)
    causal = c_io < r_io

    def block(j, carry, diag):
        acc, c = carry
        start = pl.multiple_of(j * tq, tq)
        kj = k_ref[pl.ds(start, tq), :]
        vj = v_ref[pl.ds(start, tq), :]
        z = _dg(q, kj, NT) * scale
        sp = jnp.log(1.0 + jnp.exp(-jnp.abs(z)))
        log_beta = jnp.minimum(z, 0.0) - sp
        log_1mb = -jnp.maximum(z, 0.0) - sp
        if diag:
            log_1mb = jnp.where(causal, log_1mb, 0.0)
        hi, lo = _split2(log_1mb)
        tail = _dg(hi, suffix) + _dg(lo, suffix) + c
        w = jnp.exp(log_beta + tail)
        if diag:
            w = jnp.where(causal, w, 0.0)
        acc = acc + _dg(w.astype(BF16), vj)
        c = c + jnp.sum(log_1mb, axis=1, keepdims=True)
        return acc, c

    carry = (jnp.zeros((tq, ATT_HD), F32), jnp.zeros((tq, 1), F32))
    carry = block(i, carry, True)
    carry = lax.fori_loop(0, i, lambda jj, cr: block(i - 1 - jj, cr, False), carry)
    o_ref[...] = carry[0].astype(o_ref.dtype)


def _stick_breaking(p_att, bsz, seq):
    p3 = p_att.reshape(bsz, seq, 3 * BRANCH_W)
    tq = ATT_TILE
    kern = functools.partial(_attn_kernel, tq=tq, scale=ATT_HD ** -0.5)
    return pl.pallas_call(
        kern,
        grid=(bsz, ATT_HEADS, seq // tq),
        in_specs=[pl.BlockSpec((None, tq, ATT_HD), lambda b, h, i: (b, i, h)),
                  pl.BlockSpec((None, seq, ATT_HD), lambda b, h, i: (b, 0, ATT_HEADS + h)),
                  pl.BlockSpec((None, seq, ATT_HD), lambda b, h, i: (b, 0, 2 * ATT_HEADS + h))],
        out_specs=pl.BlockSpec((None, tq, ATT_HD), lambda b, h, i: (b, i, h)),
        out_shape=jax.ShapeDtypeStruct((bsz, seq, BRANCH_W), BF16),
        compiler_params=_cparams(3),
        name="stick_breaking_attention",
    )(p3, p3, p3)


def _head_sum(x, ones_bd):
    hi, lo = _split2(x)
    outs = []
    for c in range(x.shape[1] // RWKV_GW):
        sl = slice(RWKV_GW * c, RWKV_GW * (c + 1))
        outs.append(_dg(hi[:, sl], ones_bd) + _dg(lo[:, sl], ones_bd))
    return outs[0] if len(outs) == 1 else jnp.concatenate(outs, axis=1)


def _head_ones():
    r = lax.broadcasted_iota(I32, (RWKV_GW, RWKV_GW), 0) // RWKV_HD
    c = lax.broadcasted_iota(I32, (RWKV_GW, RWKV_GW), 1) // RWKV_HD
    return (r == c).astype(BF16)


def _rwkv_prep_kernel(p_ref, ph_ref, mix_ref, w0_ref, w2_ref, a0_ref, a2_ref, g2_ref, kk_ref, ka_ref,
                      r_ref, lw_ref, k_ref, v_ref, a_ref, b_ref, g_ref):
    i = pl.program_id(1)
    p = p_ref[...]
    prev = jnp.where(i > 0, ph_ref[...][7:8, :], 0.0)
    row = lax.broadcasted_iota(I32, p.shape, 0)
    shifted = jnp.where(row == 0, prev, pltpu.roll(p, 1, axis=0))
    ps = p + (shifted - p) * mix_ref[...]
    w = RWKV_W
    r = ps[:, 0:w]
    k = ps[:, w:2 * w]
    v = ps[:, 2 * w:3 * w]
    lw = ps[:, 3 * w:3 * w + LORA_PAD]
    la = ps[:, 3 * w + LORA_PAD:3 * w + 2 * LORA_PAD]
    lg = ps[:, 3 * w + 2 * LORA_PAD:]
    dec_in = w0_ref[...] + _dg(jnp.tanh(lw).astype(BF16), w2_ref[...])
    w_log = -_softplus(-dec_in) - 0.5
    a = _sigmoid(a0_ref[...] + _dg(la.astype(BF16), a2_ref[...]))
    g = _dg(_sigmoid(lg).astype(BF16), g2_ref[...])
    kk = k * kk_ref[...]
    norm = jnp.sqrt(_head_sum(kk * kk, _head_ones()))
    kk = kk / jnp.maximum(norm, 1e-12)
    r_ref[...] = r
    lw_ref[...] = -jnp.exp(w_log)
    k_ref[...] = k * (1.0 + (a - 1.0) * ka_ref[...])
    v_ref[...] = v
    a_ref[...] = -kk
    b_ref[...] = kk * a
    g_ref[...] = g


def _rwkv_prep(p_rwkv, mix, w0, w2, a0, a2, g2, k_k, k_a, bsz, seq, ts=256):
    p3 = p_rwkv.reshape(bsz, seq, RWKV_COLS)
    hb = ts // 8
    vec = lambda n: pl.BlockSpec((1, n), lambda b, i: (0, 0))
    mat = lambda r: pl.BlockSpec((r, RWKV_W), lambda b, i: (0, 0))
    out = pl.BlockSpec((None, ts, RWKV_W), lambda b, i: (b, i, 0))
    return pl.pallas_call(
        _rwkv_prep_kernel,
        grid=(bsz, seq // ts),
        in_specs=[pl.BlockSpec((None, ts, RWKV_COLS), lambda b, i: (b, i, 0)),
                  pl.BlockSpec((None, 8, RWKV_COLS), lambda b, i: (b, jnp.maximum(i * hb - 1, 0), 0)),
                  vec(RWKV_COLS), vec(RWKV_W), mat(LORA_PAD), vec(RWKV_W), mat(LORA_PAD),
                  mat(GATE_LORA), vec(RWKV_W), vec(RWKV_W)],
        out_specs=[out] * 7,
        out_shape=[jax.ShapeDtypeStruct((bsz, seq, RWKV_W), F32)] * 7,
        compiler_params=_cparams(2),
        name="rwkv_prep",
    )(p3, p3, mix, w0, w2, a0, a2, g2, k_k, k_a)


def _rwkv_scan_kernel(r_ref, lw_ref, k_ref, v_ref, a_ref, b_ref, g_ref, rk_ref, lng_ref, lnb_ref,
                      o_ref, s_ref):
    c_len = RWKV_CHUNK
    gw = RWKV_GW

    @pl.when(pl.program_id(2) == 0)
    def _():
        s_ref[...] = jnp.zeros_like(s_ref)

    r = r_ref[...]
    lw = lw_ref[...]
    k = k_ref[...]
    v = v_ref[...]
    a = a_ref[...]
    b = b_ref[...]

    ti = lax.broadcasted_iota(I32, (c_len, c_len), 0)
    si = lax.broadcasted_iota(I32, (c_len, c_len), 1)
    tril_c = (ti >= si).astype(BF16)
    l_hi, l_mid, l_lo = _split3(lw)
    cum = _dg(tril_c, l_hi) + _dg(tril_c, l_mid) + _dg(tril_c, l_lo)
    cum_end = cum[c_len - 1:c_len, :]
    e_cum = jnp.exp(cum)
    e_inv = jnp.exp(-cum)
    e_rem = jnp.exp(cum_end - cum)
    a_t = a * jnp.exp(cum - lw)
    r_t = r * e_cum
    b_t = b * e_inv
    k_t = k * e_inv
    b_h = b * e_rem
    k_h = k * e_rem
    w_end = jnp.exp(cum_end)

    rows = lax.broadcasted_iota(I32, (gw, gw), 0)
    cols = lax.broadcasted_iota(I32, (gw, gw), 1)
    head_mask = ((rows // c_len) == (cols // RWKV_HD)).astype(F32)
    strict = (rows > cols).astype(F32)
    incl = (rows >= cols).astype(F32)
    eye = (rows == cols).astype(F32)

    def stack(x):
        return jnp.concatenate([x] * RWKV_GROUP, axis=0) * head_mask

    a_s, r_s, b_s, k_s, v_s = stack(a_t), stack(r_t), stack(b_t), stack(k_t), stack(v)
    bh_s, kh_s = stack(b_h), stack(k_h)

    a_ab = _mm3(a_s, b_s, NT) * strict
    a_ak = _mm3(a_s, k_s, NT) * strict
    a_rb = _mm3(r_s, b_s, NT) * incl
    a_rk = _mm3(r_s, k_s, NT) * incl

    inv = eye + a_ab
    pw = a_ab
    n_sq = 0
    while (2 << n_sq) < c_len:
        n_sq += 1
    for _ in range(n_sq):
        pw = _mm3(pw, pw)
        inv = inv + _mm3(inv, pw)

    s0 = s_ref[...]
    u = _mm3(inv, _mm3(a_s, s0, NT) + _mm3(a_ak, v_s))
    y_s = _mm3(r_s, s0, NT) + _mm3(a_rb, u) + _mm3(a_rk, v_s)
    y = y_s[0:c_len]
    for h in range(1, RWKV_GROUP):
        y = y + y_s[h * c_len:(h + 1) * c_len]
    s_ref[...] = s0 * w_end + _mm3(u.T, bh_s) + _mm3(v_s.T, kh_s)

    ones_bd = _head_ones()
    mu = _head_sum(y, ones_bd) * (1.0 / RWKV_HD)
    d = y - mu
    var = _head_sum(d * d, ones_bd) * (1.0 / RWKV_HD)
    yn = d * lax.rsqrt(var + GN_EPS) * lng_ref[...] + lnb_ref[...]
    bonus = _head_sum(r * k * rk_ref[...], ones_bd) * v
    o_ref[...] = ((yn + bonus) * g_ref[...]).astype(o_ref.dtype)


def _rwkv_scan(r, lw, k, v, a, b, g, r_k, lnx_g, lnx_b, bsz, seq):
    n_groups = RWKV_W // RWKV_GW
    tok = pl.BlockSpec((None, RWKV_CHUNK, RWKV_GW), lambda bb, gg, i: (bb, i, gg))
    vec = pl.BlockSpec((1, RWKV_GW), lambda bb, gg, i: (0, gg))
    return pl.pallas_call(
        _rwkv_scan_kernel,
        grid=(bsz, n_groups, seq // RWKV_CHUNK),
        in_specs=[tok] * 7 + [vec] * 3,
        out_specs=tok,
        out_shape=jax.ShapeDtypeStruct((bsz, seq, RWKV_W), BF16),
        scratch_shapes=[pltpu.VMEM((RWKV_GW, RWKV_GW), F32)],
        compiler_params=_cparams(3),
        name="rwkv_scan",
    )(r, lw, k, v, a, b, g, r_k, lnx_g, lnx_b)


def _merge_kernel(yc_ref, ya_ref, yr_ref, wb_ref, g0_ref, g1_ref, g2_ref, o_ref):
    acc = None
    for n, (y_ref, g_ref) in enumerate(((yc_ref, g0_ref), (ya_ref, g1_ref), (yr_ref, g2_ref))):
        term = _sigmoid(g_ref[...]) * _dg(y_ref[...], wb_ref[n])
        acc = term if acc is None else acc + term
    o_ref[...] = acc.astype(o_ref.dtype)


def _merge(y_conv, y_att, y_rwkv, w_branch, p_gate, tm=512, tn=1024):
    m = y_conv.shape[0]
    nj = D_MODEL // tn
    ysp = pl.BlockSpec((tm, BRANCH_W), lambda j, i: (i, 0))
    gsp = lambda n: pl.BlockSpec((tm, tn), lambda j, i: (i, n * nj + j))
    return pl.pallas_call(
        _merge_kernel,
        grid=(nj, m // tm),
        in_specs=[ysp, ysp, ysp, pl.BlockSpec((3, BRANCH_W, tn), lambda j, i: (0, 0, j)),
                  gsp(0), gsp(1), gsp(2)],
        out_specs=pl.BlockSpec((tm, tn), lambda j, i: (i, j)),
        out_shape=jax.ShapeDtypeStruct((m, D_MODEL), BF16),
        compiler_params=_cparams(2),
        name="branch_merge",
    )(y_conv, y_att, y_rwkv, w_branch, p_gate, p_gate, p_gate)


def _layer_norm(y, g, b):
    mu = jnp.mean(y, axis=-1, keepdims=True)
    d = y - mu
    var = jnp.mean(d * d, axis=-1, keepdims=True)
    return d * lax.rsqrt(var + LN_EPS) * g + b


def _pack_rows(x):
    hi = lax.bitcast_convert_type(x[:, :PACK_W].astype(BF16).astype(F32), U32)
    lo = lax.bitcast_convert_type(x[:, PACK_W:].astype(BF16).astype(F32), U32)
    return hi | (lo >> 16)


def _unpack_rows(w):
    hi = lax.bitcast_convert_type(w & jnp.uint32(0xFFFF0000), F32).astype(BF16)
    lo = lax.bitcast_convert_type(w << 16, F32).astype(BF16)
    return hi, lo


def _oproj_ln_kernel(m_ref, w_ref, x_ref, g_ref, b_ref, o_ref, opk_ref):
    y = DEEPNORM_ALPHA * x_ref[...] + _dg(m_ref[...], w_ref[...])
    out = _layer_norm(y, g_ref[...], b_ref[...])
    o_ref[...] = out
    opk_ref[...] = _pack_rows(out)


def _oproj_ln(merged, w_o, x, g, b, tm=256):
    m = x.shape[0]
    row = lambda n: pl.BlockSpec((tm, n), lambda i: (i, 0))
    vec = pl.BlockSpec((1, D_MODEL), lambda i: (0, 0))
    return pl.pallas_call(
        _oproj_ln_kernel,
        grid=(m // tm,),
        in_specs=[row(D_MODEL), pl.BlockSpec((D_MODEL, D_MODEL), lambda i: (0, 0)), row(D_MODEL), vec, vec],
        out_specs=[row(D_MODEL), row(PACK_W)],
        out_shape=[jax.ShapeDtypeStruct((m, D_MODEL), F32), jax.ShapeDtypeStruct((m, PACK_W), U32)],
        compiler_params=_cparams(1),
        name="out_proj_ln",
    )(merged, w_o, x, g, b)


def _first_max(x, idx, n):
    m = jnp.max(x, axis=0, keepdims=True)
    first = jnp.min(jnp.where(x == m, idx, float(n)), axis=0, keepdims=True)
    return m, first


def _router_kernel(x_ref, wr_ref, bias_ref, ti_ref, tw_ref, rk_ref, cnt_ref, run_ref, *, tm):
    @pl.when(pl.program_id(0) == 0)
    def _():
        run_ref[...] = jnp.zeros_like(run_ref)

    xh, xl = _split2(x_ref[...])
    wh, wl = _split2(wr_ref[...])
    logits = _dg(wh, xh, NT) + _dg(wh, xl, NT) + _dg(wl, xh, NT)
    scores = _sigmoid(logits)
    biased = scores + bias_ref[...]
    neg = -jnp.inf

    sub8 = lax.broadcasted_iota(I32, (GROUP_SIZE, tm), 0).astype(F32)
    grp_rows = []
    for gi in range(N_GROUPS):
        blk = biased[gi * GROUP_SIZE:(gi + 1) * GROUP_SIZE, :]
        m1, f1 = _first_max(blk, sub8, GROUP_SIZE)
        m2 = jnp.max(jnp.where(sub8 == f1, neg, blk), axis=0, keepdims=True)
        grp_rows.append(m1 + m2)
    cur = jnp.concatenate(grp_rows, axis=0)
    grp_io = lax.broadcasted_iota(I32, (N_GROUPS, tm), 0).astype(F32)
    keep = jnp.zeros((N_GROUPS, tm), F32)
    for _ in range(TOPK_GROUPS):
        _, f = _first_max(cur, grp_io, N_GROUPS)
        hit = grp_io == f
        keep = jnp.where(hit, 1.0, keep)
        cur = jnp.where(hit, neg, cur)
    exp_keep = jnp.concatenate(
        [jnp.broadcast_to(keep[gi:gi + 1, :], (GROUP_SIZE, tm)) for gi in range(N_GROUPS)], axis=0)
    cand = jnp.where(exp_keep > 0.5, biased, neg)

    exp_io = lax.broadcasted_iota(I32, (N_EXPERTS, tm), 0).astype(F32)
    sel = jnp.zeros((N_EXPERTS, tm), F32)
    idxs, vals = [], []
    for _ in range(TOP_K):
        _, f = _first_max(cand, exp_io, N_EXPERTS)
        hit = exp_io == f
        idxs.append(f)
        vals.append(jnp.sum(jnp.where(hit, scores, 0.0), axis=0, keepdims=True))
        sel = jnp.where(hit, 1.0, sel)
        cand = jnp.where(hit, neg, cand)
    top_s = jnp.concatenate(vals, axis=0)
    ti_ref[...] = jnp.concatenate(idxs, axis=0).astype(I32)
    tw_ref[...] = top_s / jnp.sum(top_s, axis=0, keepdims=True) * ROUTED_SCALE

    before = (lax.broadcasted_iota(I32, (tm, tm), 0) < lax.broadcasted_iota(I32, (tm, tm), 1)).astype(BF16)
    rank_all = _dg(sel.astype(BF16), before) + run_ref[...][:, 0:1]
    ranks = [jnp.sum(jnp.where(exp_io == f, rank_all, 0.0), axis=0, keepdims=True) for f in idxs]
    rk_ref[...] = jnp.concatenate(ranks, axis=0).astype(I32)
    run_ref[...] = run_ref[...] + jnp.sum(sel, axis=1, keepdims=True)
    cnt_ref[...] = run_ref[...]


def _router(x, wr_t, bias, tm=512):
    m = x.shape[0]
    slot = pl.BlockSpec((TOP_K, tm), lambda i: (0, i))
    return pl.pallas_call(
        functools.partial(_router_kernel, tm=tm),
        grid=(m // tm,),
        in_specs=[pl.BlockSpec((tm, D_MODEL), lambda i: (i, 0)),
                  pl.BlockSpec((N_EXPERTS, D_MODEL), lambda i: (0, 0)),
                  pl.BlockSpec((N_EXPERTS, 1), lambda i: (0, 0))],
        out_specs=[slot, slot, slot, pl.BlockSpec((N_EXPERTS, LANES), lambda i: (0, 0))],
        out_shape=[jax.ShapeDtypeStruct((TOP_K, m), I32), jax.ShapeDtypeStruct((TOP_K, m), F32),
                   jax.ShapeDtypeStruct((TOP_K, m), I32), jax.ShapeDtypeStruct((N_EXPERTS, LANES), F32)],
        scratch_shapes=[pltpu.VMEM((N_EXPERTS, LANES), F32)],
        compiler_params=_cparams(1),
        name="moe_router",
    )(x, wr_t, bias)


def _dispatch_kernel(pos_hbm, x_hbm, xs_hbm, pos_smem, sem_idx, sem, *, tm):
    i = pl.program_id(0)
    cp = pltpu.make_async_copy(pos_hbm.at[i], pos_smem, sem_idx)
    cp.start()
    cp.wait()

    def row_copy(t, kk):
        return pltpu.make_async_copy(x_hbm.at[i * tm + t], xs_hbm.at[pos_smem[kk, t]], sem)

    def issue(t, carry):
        for kk in range(TOP_K):
            row_copy(t, kk).start()
        return carry

    def drain(t, carry):
        for kk in range(TOP_K):
            row_copy(t, kk).wait()
        return carry

    lax.fori_loop(0, tm, issue, 0)
    lax.fori_loop(0, tm, drain, 0)


def _dispatch(pos_tiles, x_pk3, tm):
    n_tiles = pos_tiles.shape[0]
    m = x_pk3.shape[0]
    return pl.pallas_call(
        functools.partial(_dispatch_kernel, tm=tm),
        grid=(n_tiles,),
        in_specs=[pl.BlockSpec(memory_space=pl.ANY), pl.BlockSpec(memory_space=pl.ANY)],
        out_specs=pl.BlockSpec(memory_space=pl.ANY),
        out_shape=jax.ShapeDtypeStruct((m * TOP_K, PACK_SUB, LANES), U32),
        scratch_shapes=[pltpu.SMEM((TOP_K, tm), I32), pltpu.SemaphoreType.DMA, pltpu.SemaphoreType.DMA],
        compiler_params=_cparams(1),
        name="moe_dispatch",
    )(pos_tiles, x_pk3)


def _gmm_kernel(vt_ref, ve_ref, vlo_ref, vhi_ref, vfirst_ref, x_ref, wg_ref, wu_ref, wd_ref, o_ref):
    vi = pl.program_id(0)
    lo = vlo_ref[vi]
    hi = vhi_ref[vi]

    @pl.when(vfirst_ref[vi] == 1)
    def _():
        o_ref[...] = jnp.zeros_like(o_ref)

    @pl.when(hi > lo)
    def _():
        xa, xb = _unpack_rows(x_ref[...])
        gate = _dg(xa, wg_ref[0:PACK_W, :]) + _dg(xb, wg_ref[PACK_W:, :])
        up = _dg(xa, wu_ref[0:PACK_W, :]) + _dg(xb, wu_ref[PACK_W:, :])
        h = gate * _sigmoid(gate) * up
        y = _dg(h.astype(BF16), wd_ref[...])
        row = lax.broadcasted_iota(I32, (y.shape[0], 1), 0)
        o_ref[...] += jnp.where((row >= lo) & (row < hi), y, 0.0)


def _grouped_mlp(visits, xs, w_gate, w_up, w_down, n_visits):
    rows = xs.shape[0]
    tm = GMM_TILE
    grid_spec = pltpu.PrefetchScalarGridSpec(
        num_scalar_prefetch=5,
        grid=(n_visits,),
        in_specs=[pl.BlockSpec((tm, PACK_W), lambda vi, vt, ve, vlo, vhi, vf: (vt[vi], 0)),
                  pl.BlockSpec((None, D_MODEL, EXPERT_FF), lambda vi, vt, ve, vlo, vhi, vf: (ve[vi], 0, 0)),
                  pl.BlockSpec((None, D_MODEL, EXPERT_FF), lambda vi, vt, ve, vlo, vhi, vf: (ve[vi], 0, 0)),
                  pl.BlockSpec((None, EXPERT_FF, D_MODEL), lambda vi, vt, ve, vlo, vhi, vf: (ve[vi], 0, 0))],
        out_specs=pl.BlockSpec((tm, D_MODEL), lambda vi, vt, ve, vlo, vhi, vf: (vt[vi], 0)),
    )
    return pl.pallas_call(
        _gmm_kernel,
        grid_spec=grid_spec,
        out_shape=jax.ShapeDtypeStruct((rows, D_MODEL), F32),
        compiler_params=_cparams(1),
        name="moe_grouped_mlp",
    )(*visits, xs, w_gate, w_up, w_down)


def _visit_list(counts, n_rows, tm, n_visits):
    ends = jnp.cumsum(counts)
    offs = ends - counts
    first_tile = offs // tm
    last_tile = jnp.where(counts > 0, (ends - 1) // tm, first_tile - 1)
    n_t = last_tile - first_tile + 1
    v_end = jnp.cumsum(n_t)
    v_start = v_end - n_t
    total = v_end[-1]
    vi = jnp.arange(n_visits, dtype=I32)
    e = jnp.minimum(jnp.searchsorted(v_end, vi, side="right"), N_EXPERTS - 1).astype(I32)
    tile = first_tile[e] + (vi - v_start[e])
    active = vi < total
    last_e = jnp.max(jnp.where(counts > 0, jnp.arange(N_EXPERTS, dtype=I32), 0))
    tile = jnp.where(active, tile, n_rows // tm - 1).astype(I32)
    e = jnp.where(active, e, last_e).astype(I32)
    lo = jnp.where(active, jnp.maximum(offs[e], tile * tm) - tile * tm, 0).astype(I32)
    hi = jnp.where(active, jnp.minimum(ends[e], (tile + 1) * tm) - tile * tm, 0).astype(I32)
    prev_tile = jnp.concatenate([jnp.full((1,), -1, I32), tile[:-1]])
    first = (active & (tile != prev_tile)).astype(I32)
    return tile, e, lo, hi, first


def _combine_kernel(pos_hbm, w_hbm, ys_hbm, o_ref, pos_smem, w_smem, buf, sem_idx, sem, *, tm):
    i = pl.program_id(0)
    cp_pos = pltpu.make_async_copy(pos_hbm.at[i], pos_smem, sem_idx.at[0])
    cp_w = pltpu.make_async_copy(w_hbm.at[i], w_smem, sem_idx.at[1])
    cp_pos.start()
    cp_w.start()
    cp_pos.wait()
    cp_w.wait()

    def row_copy(t, kk):
        return pltpu.make_async_copy(ys_hbm.at[pos_smem[kk, t]], buf.at[t, kk], sem)

    def issue(t, carry):
        for kk in range(TOP_K):
            row_copy(t, kk).start()
        return carry

    def drain(t, carry):
        for kk in range(TOP_K):
            row_copy(t, kk).wait()
        return carry

    lax.fori_loop(0, tm, issue, 0)
    lax.fori_loop(0, tm, drain, 0)

    def reduce(t, carry):
        acc = w_smem[0, t] * buf[t, 0]
        for kk in range(1, TOP_K):
            acc = acc + w_smem[kk, t] * buf[t, kk]
        o_ref[t] = acc
        return carry

    lax.fori_loop(0, tm, reduce, 0)


def _combine(pos_tiles, w_tiles, ys3, tm):
    n_tiles = pos_tiles.shape[0]
    return pl.pallas_call(
        functools.partial(_combine_kernel, tm=tm),
        grid=(n_tiles,),
        in_specs=[pl.BlockSpec(memory_space=pl.ANY)] * 3,
        out_specs=pl.BlockSpec((tm, ROW_SUB, LANES), lambda i: (i, 0, 0)),
        out_shape=jax.ShapeDtypeStruct((n_tiles * tm, ROW_SUB, LANES), F32),
        scratch_shapes=[pltpu.SMEM((TOP_K, tm), I32), pltpu.SMEM((TOP_K, tm), F32),
                        pltpu.VMEM((tm, TOP_K, ROW_SUB, LANES), F32),
                        pltpu.SemaphoreType.DMA((2,)), pltpu.SemaphoreType.DMA],
        compiler_params=_cparams(1),
        name="moe_combine",
    )(pos_tiles, w_tiles, ys3)


def _shared_ln_kernel(xpk_ref, x_ref, moe_ref, wg_ref, wu_ref, wd_ref, g_ref, b_ref, o_ref, obf_ref):
    xa, xb = _unpack_rows(xpk_ref[...])
    gate = _dg(xa, wg_ref[0:PACK_W, :]) + _dg(xb, wg_ref[PACK_W:, :])
    up = _dg(xa, wu_ref[0:PACK_W, :]) + _dg(xb, wu_ref[PACK_W:, :])
    h = gate * _sigmoid(gate) * up
    shared = _dg(h.astype(BF16), wd_ref[...])
    y = DEEPNORM_ALPHA * x_ref[...] + (moe_ref[...] + shared)
    out = _layer_norm(y, g_ref[...], b_ref[...])
    o_ref[...] = out
    obf_ref[...] = out.astype(BF16)


def _shared_ln(x_pk, x, moe, w_gate, w_up, w_down, g, b, tm=256):
    m = x.shape[0]
    row = lambda n: pl.BlockSpec((tm, n), lambda i: (i, 0))
    vec = pl.BlockSpec((1, D_MODEL), lambda i: (0, 0))
    return pl.pallas_call(
        _shared_ln_kernel,
        grid=(m // tm,),
        in_specs=[row(PACK_W), row(D_MODEL), row(D_MODEL),
                  pl.BlockSpec((D_MODEL, SHARED_FF), lambda i: (0, 0)),
                  pl.BlockSpec((D_MODEL, SHARED_FF), lambda i: (0, 0)),
                  pl.BlockSpec((SHARED_FF, D_MODEL), lambda i: (0, 0)), vec, vec],
        out_specs=[row(D_MODEL), row(D_MODEL)],
        out_shape=[jax.ShapeDtypeStruct((m, D_MODEL), F32), jax.ShapeDtypeStruct((m, D_MODEL), BF16)],
        compiler_params=_cparams(1),
        name="shared_expert_ln",
    )(x_pk, x, moe, w_gate, w_up, w_down, g, b)


def _pad_cols(w, n):
    return jnp.pad(w, ((0, 0), (0, n - w.shape[1])))


def _in_proj_weight(w_in_l):
    c0 = 6 * BRANCH_W
    rw = c0
    lora = rw + 3 * RWKV_W
    parts = [w_in_l[:, :c0],
             w_in_l[:, c0 + 3 * RWKV_W + DECAY_LORA + AAA_LORA + GATE_LORA:],
             w_in_l[:, rw:lora],
             _pad_cols(w_in_l[:, lora:lora + DECAY_LORA], LORA_PAD),
             _pad_cols(w_in_l[:, lora + DECAY_LORA:lora + DECAY_LORA + AAA_LORA], LORA_PAD),
             w_in_l[:, lora + DECAY_LORA + AAA_LORA:lora + DECAY_LORA + AAA_LORA + GATE_LORA]]
    return jnp.concatenate(parts, axis=1).astype(BF16)


def _rwkv_mix_vector(mix_l):
    lora = 3 * RWKV_W
    parts = [mix_l[:lora],
             jnp.pad(mix_l[lora:lora + DECAY_LORA], (0, LORA_PAD - DECAY_LORA)),
             jnp.pad(mix_l[lora + DECAY_LORA:lora + DECAY_LORA + AAA_LORA], (0, LORA_PAD - AAA_LORA)),
             mix_l[lora + DECAY_LORA + AAA_LORA:]]
    return jnp.concatenate(parts)[None, :]


def _pad_rows(w, n):
    return jnp.pad(w, ((0, n - w.shape[0]), (0, 0))).astype(BF16)


def _tile_major(a, tm):
    kk, t = a.shape
    return a.reshape(kk, t // tm, tm).transpose(1, 0, 2)


def kernel(x, w_in, conv_w, rwkv_mix, rwkv_w0, rwkv_w2, rwkv_a0, rwkv_a2, rwkv_g2, rwkv_kk, rwkv_ka, rwkv_rk,
           rwkv_lnx_g, rwkv_lnx_b, w_branch, w_o, ln1_g, ln1_b, w_router, router_bias, w_e_gate, w_e_up,
           w_e_down, w_s_gate, w_s_up, w_s_down, ln2_g, ln2_b):
    bsz, seq, d = x.shape
    n_tok = bsz * seq
    n_rows = n_tok * TOP_K
    n_visits = n_rows // GMM_TILE + N_EXPERTS
    io_tile = 256
    xf = x.reshape(n_tok, d)
    xb = xf.astype(BF16)
    row1 = lambda v: v[None, :]
    for l in range(DEPTH):
        w_in_r = _in_proj_weight(w_in[l])
        p_conv = _matmul_cols(xb, w_in_r, COL_CONV, 3 * BRANCH_W, 1024, 1024, F32)
        p_att = _matmul_cols(xb, w_in_r, COL_ATT, 3 * BRANCH_W, 1024, 1024, BF16)
        p_gate = _matmul_cols(xb, w_in_r, COL_GATE, 3 * D_MODEL, 1024, 1024, F32)
        p_rwkv = _matmul_cols(xb, w_in_r, COL_RWKV, RWKV_COLS, 512, 1024, F32)

        y_conv = _short_conv(p_conv, conv_w[l], bsz, seq)
        y_att = _stick_breaking(p_att, bsz, seq)
        prep = _rwkv_prep(p_rwkv, _rwkv_mix_vector(rwkv_mix[l]), row1(rwkv_w0[l]),
                          _pad_rows(rwkv_w2[l], LORA_PAD), row1(rwkv_a0[l]), _pad_rows(rwkv_a2[l], LORA_PAD),
                          rwkv_g2[l].astype(BF16), row1(rwkv_kk[l]), row1(rwkv_ka[l]), bsz, seq)
        y_rwkv = _rwkv_scan(*prep, rwkv_rk[l].reshape(1, RWKV_W), row1(rwkv_lnx_g[l]), row1(rwkv_lnx_b[l]),
                            bsz, seq)

        merged = _merge(y_conv.reshape(n_tok, BRANCH_W), y_att.reshape(n_tok, BRANCH_W),
                        y_rwkv.reshape(n_tok, BRANCH_W), w_branch[l].astype(BF16), p_gate)
        x1, x1_pk = _oproj_ln(merged, w_o[l].astype(BF16), xf, row1(ln1_g[l]), row1(ln1_b[l]))

        top_i, top_w, rank, cnt = _router(x1, w_router[l].T, router_bias[l][:, None])
        counts = cnt[:, 0].astype(I32)
        offs = jnp.cumsum(counts) - counts
        pos = offs[top_i] + rank
        pos_tiles = _tile_major(pos, io_tile)
        xs3 = _dispatch(pos_tiles, x1_pk.reshape(n_tok, PACK_SUB, LANES), io_tile)
        visits = _visit_list(counts, n_rows, GMM_TILE, n_visits)
        ys = _grouped_mlp(visits, xs3.reshape(n_rows, PACK_W), w_e_gate[l].astype(BF16),
                          w_e_up[l].astype(BF16), w_e_down[l].astype(BF16), n_visits)
        moe3 = _combine(pos_tiles, _tile_major(top_w, io_tile), ys.reshape(n_rows, ROW_SUB, LANES), io_tile)
        xf, xb = _shared_ln(x1_pk, x1, moe3.reshape(n_tok, D_MODEL), w_s_gate[l].astype(BF16),
                            w_s_up[l].astype(BF16), w_s_down[l].astype(BF16), row1(ln2_g[l]), row1(ln2_b[l]))
    return xf.reshape(bsz, seq, d)
```

```python
import functools

import jax
import jax.numpy as jnp
from jax import lax
from jax.experimental import pallas as pl
from jax.experimental.pallas import tpu as pltpu

F32 = jnp.float32
BF16 = jnp.bfloat16
I32 = jnp.int32
U32 = jnp.uint32

D_MODEL = 2048
DEPTH = 2
BRANCH_W = 1024
CONV_K = 3
ATT_HEADS = 8
ATT_HD = 128
RWKV_HEADS = 16
RWKV_HD = 64
RWKV_W = RWKV_HEADS * RWKV_HD
DECAY_LORA = 96
AAA_LORA = 96
GATE_LORA = 256
LORA_PAD = 128
N_EXPERTS = 64
TOP_K = 8
N_GROUPS = 8
GROUP_SIZE = N_EXPERTS // N_GROUPS
TOPK_GROUPS = 4
EXPERT_FF = 512
SHARED_FF = 512
ROUTED_SCALE = 2.5
LN_EPS = 1e-5
GN_EPS = 64e-5
DEEPNORM_ALPHA = (2 * DEPTH) ** 0.25

COL_CONV = 0
COL_ATT = 3 * BRANCH_W
COL_GATE = 6 * BRANCH_W
COL_RWKV = COL_GATE + 3 * D_MODEL
RWKV_COLS = 3 * RWKV_W + 2 * LORA_PAD + GATE_LORA
IN_COLS_PAD = COL_RWKV + RWKV_COLS

RWKV_CHUNK = 64
RWKV_GROUP = 4
RWKV_GW = RWKV_GROUP * RWKV_HD
RWKV_GROUPS_PER_STEP = 2
ATT_TILE = 512
ATT_KEY_TILE = 256
GMM_TILE = 256
LANES = 128
PACK_W = D_MODEL // 2

VMEM_LIMIT = 56 * 1024 * 1024

NN = ((1,), (0,))
NT = ((1,), (1,))


def _cparams(n_axes):
    return pltpu.CompilerParams(dimension_semantics=("arbitrary",) * n_axes,
                                vmem_limit_bytes=VMEM_LIMIT)


def _dg(a, b, dims=NN):
    return lax.dot_general(a, b, (dims, ((), ())), preferred_element_type=F32)


def _split2(x):
    hi = x.astype(BF16)
    lo = (x - hi.astype(F32)).astype(BF16)
    return hi, lo


def _split3(x):
    hi = x.astype(BF16)
    r1 = x - hi.astype(F32)
    mid = r1.astype(BF16)
    lo = (r1 - mid.astype(F32)).astype(BF16)
    return hi, mid, lo


def _mm3(a, b, dims=NN):
    ah, al = _split2(a)
    bh, bl = _split2(b)
    return _dg(ah, bh, dims) + _dg(ah, bl, dims) + _dg(al, bh, dims)


def _sigmoid(x):
    return 1.0 / (1.0 + jnp.exp(-x))


def _softplus(x):
    return jnp.maximum(x, 0.0) + jnp.log(1.0 + jnp.exp(-jnp.abs(x)))


def _mm_kernel(x_ref, w_ref, o_ref):
    o_ref[...] = _dg(x_ref[...], w_ref[...]).astype(o_ref.dtype)


def _matmul_cols(x, w, col_off, ncols, tn, tm, out_dtype):
    m, k = x.shape
    off = col_off // tn
    return pl.pallas_call(
        _mm_kernel,
        grid=(ncols // tn, m // tm),
        in_specs=[pl.BlockSpec((tm, k), lambda j, i: (i, 0)),
                  pl.BlockSpec((k, tn), lambda j, i: (0, off + j))],
        out_specs=pl.BlockSpec((tm, tn), lambda j, i: (i, j)),
        out_shape=jax.ShapeDtypeStruct((m, ncols), out_dtype),
        compiler_params=_cparams(2),
        name="in_proj",
    )(x, w)


def _conv_kernel(gb_ref, gc_ref, hc_ref, gch_ref, hch_ref, cw_ref, o_ref):
    i = pl.program_id(1)
    u = gc_ref[...] * hc_ref[...]
    uh = gch_ref[...] * hch_ref[...]
    uh = jnp.where(i > 0, uh, 0.0)
    row = lax.broadcasted_iota(I32, u.shape, 0)
    u1 = jnp.where(row == 0, uh[7:8, :], pltpu.roll(u, 1, axis=0))
    u2 = pltpu.roll(u, 2, axis=0)
    u2 = jnp.where(row == 0, uh[6:7, :], jnp.where(row == 1, uh[7:8, :], u2))
    cw = cw_ref[...]
    y = gb_ref[...] * (cw[2:3, :] * u + cw[1:2, :] * u1 + cw[0:1, :] * u2)
    o_ref[...] = y.astype(o_ref.dtype)


def _short_conv(p_conv, conv_w, bsz, seq, ts=512):
    p3 = p_conv.reshape(bsz, seq, 3 * BRANCH_W)
    hb = ts // 8
    halo = lambda c: pl.BlockSpec((None, 8, BRANCH_W),
                                  lambda b, i: (b, jnp.maximum(i * hb - 1, 0), c))
    main = lambda c: pl.BlockSpec((None, ts, BRANCH_W), lambda b, i: (b, i, c))
    return pl.pallas_call(
        _conv_kernel,
        grid=(bsz, seq // ts),
        in_specs=[main(0), main(1), main(2), halo(1), halo(2),
                  pl.BlockSpec((CONV_K, BRANCH_W), lambda b, i: (0, 0))],
        out_specs=pl.BlockSpec((None, ts, BRANCH_W), lambda b, i: (b, i, 0)),
        out_shape=jax.ShapeDtypeStruct((bsz, seq, BRANCH_W), BF16),
        compiler_params=_cparams(2),
        name="short_conv",
    )(p3, p3, p3, p3, p3, conv_w)


def _attn_kernel(q_ref, k_ref, v_ref, o_ref, *, tq, tk, scale):
    i = pl.program_id(2)
    q = q_ref[...]
    n_sub = tq // tk
    suffix = (lax.broadcasted_iota(I32, (tk, tk), 0)
              > lax.broadcasted_iota(I32, (tk, tk), 1)).astype(BF16)
    r_io = lax.broadcasted_iota(I32, (tq, tk), 0)
    c_io = lax.broadcasted_iota(I32, (tq, tk), 1)

    def block(j, carry, diag_off):
        acc, c = carry
        start = pl.multiple_of(j * tk, tk)
        kj = k_ref[pl.ds(start, tk), :]
        vj = v_ref[pl.ds(start, tk), :]
        z = _dg(q, kj, NT) * scale
        sp = jnp.log(1.0 + jnp.exp(-jnp.abs(z)))
        log_beta = jnp.minimum(z, 0.0) - sp
        log_1mb = -jnp.maximum(z, 0.0) - sp
        if diag_off is not None:
            causal = (c_io + diag_off * tk) < r_io
            log_1mb = jnp.where(causal, log_1mb, 0.0)
        hi, lo = _split2(log_1mb)
        tail = _dg(hi, suffix) + _dg(lo, suffix) + c
        w = jnp.exp(log_beta + tail)
        if diag_off is not None:
            w = jnp.where(causal, w, 0.0)
        acc = acc + _dg(w.astype(BF16), vj)
        c = c + jnp.sum(log_1mb, axis=1, keepdims=True)
        return acc, c

    carry = (jnp.zeros((tq, ATT_HD), F32), jnp.zeros((tq, 1), F32))
    for d in reversed(range(n_sub)):
        carry = block(n_sub * i + d, carry, d)
    n_below = n_sub * i
    carry = lax.fori_loop(0, n_below, lambda jj, cr: block(n_below - 1 - jj, cr, None), carry)
    o_ref[...] = carry[0].astype(o_ref.dtype)


def _stick_breaking(p_att, bsz, seq):
    p3 = p_att.reshape(bsz, seq, 3 * BRANCH_W)
    tq = ATT_TILE
    kern = functools.partial(_attn_kernel, tq=tq, tk=ATT_KEY_TILE, scale=ATT_HD ** -0.5)
    return pl.pallas_call(
        kern,
        grid=(bsz, ATT_HEADS, seq // tq),
        in_specs=[pl.BlockSpec((None, tq, ATT_HD), lambda b, h, i: (b, i, h)),
                  pl.BlockSpec((None, seq, ATT_HD), lambda b, h, i: (b, 0, ATT_HEADS + h)),
                  pl.BlockSpec((None, seq, ATT_HD), lambda b, h, i: (b, 0, 2 * ATT_HEADS + h))],
        out_specs=pl.BlockSpec((None, tq, ATT_HD), lambda b, h, i: (b, i, h)),
        out_shape=jax.ShapeDtypeStruct((bsz, seq, BRANCH_W), BF16),
        compiler_params=_cparams(3),
        name="stick_breaking_attention",
    )(p3, p3, p3)


def _head_sum(x, ones_bd):
    hi, lo = _split2(x)
    outs = []
    for c in range(x.shape[1] // RWKV_GW):
        sl = slice(RWKV_GW * c, RWKV_GW * (c + 1))
        outs.append(_dg(hi[:, sl], ones_bd) + _dg(lo[:, sl], ones_bd))
    return outs[0] if len(outs) == 1 else jnp.concatenate(outs, axis=1)


def _head_ones():
    r = lax.broadcasted_iota(I32, (RWKV_GW, RWKV_GW), 0) // RWKV_HD
    c = lax.broadcasted_iota(I32, (RWKV_GW, RWKV_GW), 1) // RWKV_HD
    return (r == c).astype(BF16)


def _rwkv_prep_kernel(p_ref, ph_ref, mix_ref, w0_ref, w2_ref, a0_ref, a2_ref, g2_ref, kk_ref, ka_ref,
                      r_ref, lw_ref, k_ref, v_ref, a_ref, b_ref, g_ref):
    i = pl.program_id(1)
    p = p_ref[...]
    prev = jnp.where(i > 0, ph_ref[...][7:8, :], 0.0)
    row = lax.broadcasted_iota(I32, p.shape, 0)
    shifted = jnp.where(row == 0, prev, pltpu.roll(p, 1, axis=0))
    ps = p + (shifted - p) * mix_ref[...]
    w = RWKV_W
    r = ps[:, 0:w]
    k = ps[:, w:2 * w]
    v = ps[:, 2 * w:3 * w]
    lw = ps[:, 3 * w:3 * w + LORA_PAD]
    la = ps[:, 3 * w + LORA_PAD:3 * w + 2 * LORA_PAD]
    lg = ps[:, 3 * w + 2 * LORA_PAD:]
    dec_in = w0_ref[...] + _dg(jnp.tanh(lw).astype(BF16), w2_ref[...])
    w_log = -_softplus(-dec_in) - 0.5
    a = _sigmoid(a0_ref[...] + _dg(la.astype(BF16), a2_ref[...]))
    g = _dg(_sigmoid(lg).astype(BF16), g2_ref[...])
    kk = k * kk_ref[...]
    norm = jnp.sqrt(_head_sum(kk * kk, _head_ones()))
    kk = kk / jnp.maximum(norm, 1e-12)
    r_ref[...] = r
    lw_ref[...] = -jnp.exp(w_log)
    k_ref[...] = k * (1.0 + (a - 1.0) * ka_ref[...])
    v_ref[...] = v
    a_ref[...] = -kk
    b_ref[...] = kk * a
    g_ref[...] = g


def _rwkv_prep(p_rwkv, mix, w0, w2, a0, a2, g2, k_k, k_a, bsz, seq, ts=256):
    p3 = p_rwkv.reshape(bsz, seq, RWKV_COLS)
    hb = ts // 8
    vec = lambda n: pl.BlockSpec((1, n), lambda b, i: (0, 0))
    mat = lambda r: pl.BlockSpec((r, RWKV_W), lambda b, i: (0, 0))
    out = pl.BlockSpec((None, ts, RWKV_W), lambda b, i: (b, i, 0))
    return pl.pallas_call(
        _rwkv_prep_kernel,
        grid=(bsz, seq // ts),
        in_specs=[pl.BlockSpec((None, ts, RWKV_COLS), lambda b, i: (b, i, 0)),
                  pl.BlockSpec((None, 8, RWKV_COLS), lambda b, i: (b, jnp.maximum(i * hb - 1, 0), 0)),
                  vec(RWKV_COLS), vec(RWKV_W), mat(LORA_PAD), vec(RWKV_W), mat(LORA_PAD),
                  mat(GATE_LORA), vec(RWKV_W), vec(RWKV_W)],
        out_specs=[out] * 7,
        out_shape=[jax.ShapeDtypeStruct((bsz, seq, RWKV_W), F32)] * 7,
        compiler_params=_cparams(2),
        name="rwkv_prep",
    )(p3, p3, mix, w0, w2, a0, a2, g2, k_k, k_a)


RWKV_LEVELS = RWKV_CHUNK.bit_length() - 1


def _rwkv_masks():
    r = jnp.arange(RWKV_GW)[:, None]
    c = jnp.arange(RWKV_GW)[None, :]
    bd = (r // RWKV_CHUNK) == (c // RWKV_HD)
    planes = [bd, bd & (r > c), bd & (r >= c)]
    for m in range(RWKV_LEVELS):
        s = 1 << m
        planes.append((r // (2 * s) == c // (2 * s)) & (r % (2 * s) >= s) & (c % (2 * s) < s))
    return jnp.stack(planes).astype(F32), bd.astype(BF16)


def _rwkv_scan_kernel(r_ref, lw_ref, k_ref, v_ref, a_ref, b_ref, g_ref, rk_ref, lng_ref, lnb_ref,
                      msk_ref, bdb_ref, o_ref, s_ref):
    @pl.when(pl.program_id(1) == 0)
    def _():
        s_ref[...] = jnp.zeros_like(s_ref)

    c_len = RWKV_CHUNK
    gw = RWKV_GW
    chains = [(bb, gi) for bb in range(r_ref.shape[0]) for gi in range(r_ref.shape[2] // gw)]

    def smap(f, *lists):
        return [f(*xs) for xs in zip(*lists)]

    def tok(ref):
        return [ref[bb, :, gi * gw:(gi + 1) * gw] for bb, gi in chains]

    def par(ref):
        return [ref[:, gi * gw:(gi + 1) * gw] for _, gi in chains]

    def tile_heads(x):
        return jnp.concatenate([x.astype(BF16)] * RWKV_GROUP, axis=0)

    def unstack(z):
        out = z[0:c_len]
        for h in range(1, RWKV_GROUP):
            out = out + z[h * c_len:(h + 1) * c_len]
        return out

    r, lw, k, v, a, b = tok(r_ref), tok(lw_ref), tok(k_ref), tok(v_ref), tok(a_ref), tok(b_ref)
    tril_c = (lax.broadcasted_iota(I32, (c_len, c_len), 0)
              >= lax.broadcasted_iota(I32, (c_len, c_len), 1)).astype(BF16)

    def cum_decay(x):
        hi, mid, lo = _split3(x)
        return _dg(tril_c, hi) + _dg(tril_c, mid) + _dg(tril_c, lo)

    cum = smap(cum_decay, lw)
    cum_end = smap(lambda c: c[c_len - 1:c_len, :], cum)
    e_inv = smap(lambda c: jnp.exp(-c), cum)
    e_rem = smap(lambda ce, c: jnp.exp(ce - c), cum_end, cum)

    bd = msk_ref[0]
    strict = msk_ref[1]
    incl = msk_ref[2]
    head_b = bdb_ref[...]
    a_s = smap(lambda x, c, l: tile_heads(x * jnp.exp(c - l)) * head_b, a, cum, lw)
    r_s = smap(lambda x, c: tile_heads(x * jnp.exp(c)) * head_b, r, cum)
    b_x = smap(lambda x, e: tile_heads(x * e), b, e_inv)
    k_x = smap(lambda x, e: tile_heads(x * e), k, e_inv)
    v_x = smap(tile_heads, v)

    a_ab = smap(lambda p, q: _dg(p, q, NT) * strict, a_s, b_x)
    a_ak = smap(lambda p, q: (_dg(p, q, NT) * strict).astype(BF16), a_s, k_x)
    a_rb = smap(lambda p, q: (_dg(p, q, NT) * incl).astype(BF16), r_s, b_x)
    a_rk = smap(lambda p, q: (_dg(p, q, NT) * incl).astype(BF16), r_s, k_x)

    inv = smap(lambda m: (incl - strict) + m * msk_ref[3], a_ab)
    for lvl in range(1, RWKV_LEVELS):
        inv_b = smap(lambda t: t.astype(BF16), inv)
        low = smap(lambda m: (m * msk_ref[3 + lvl]).astype(BF16), a_ab)
        mid = smap(lambda t, lo_: _dg(t, lo_).astype(BF16), inv_b, low)
        inv = smap(lambda t, md, tb: t + _dg(md, tb), inv, mid, inv_b)
    inv_b = smap(lambda t: t.astype(BF16), inv)

    s0 = [s_ref[bb, gi] for bb, gi in chains]
    s0_b = smap(lambda s: s.astype(BF16), s0)
    rhs = smap(lambda p, s, m, vv: (_dg(p, s, NT) + _dg(m, vv)).astype(BF16), a_s, s0_b, a_ak, v_x)
    u = smap(_dg, inv_b, rhs)
    u_b = smap(lambda t: t.astype(BF16), u)
    y = smap(lambda p, s, m1, ub, m2, vv: unstack((_dg(p, s, NT) + _dg(m1, ub) + _dg(m2, vv)) * bd),
             r_s, s0_b, a_rb, u_b, a_rk, v_x)
    uv_t = smap(lambda uu, vv: jnp.concatenate([unstack(uu * bd), vv], axis=0).T.astype(BF16), u, v)
    bk_h = smap(lambda bb_, kk_, e: jnp.concatenate([bb_ * e, kk_ * e], axis=0).astype(BF16), b, k, e_rem)
    s_new = smap(lambda s, ce, p, q: s * jnp.exp(ce) + _dg(p, q) * bd, s0, cum_end, uv_t, bk_h)
    for (bb, gi), s in zip(chains, s_new):
        s_ref[bb, gi] = s

    ones_bd = _head_ones()
    inv_n = 1.0 / RWKV_HD
    d = smap(lambda yy: yy - _head_sum(yy, ones_bd) * inv_n, y)
    var = smap(lambda dd: _head_sum(dd * dd, ones_bd) * inv_n, d)
    yn = smap(lambda dd, vr, gg, bb_: dd * lax.rsqrt(vr + GN_EPS) * gg + bb_, d, var, par(lng_ref), par(lnb_ref))
    bonus = smap(lambda rr, kk_, rk, vv: _head_sum(rr * kk_ * rk, ones_bd) * vv, r, k, par(rk_ref), v)
    for (bb, gi), yy, bo, gg in zip(chains, yn, bonus, tok(g_ref)):
        o_ref[bb, :, gi * gw:(gi + 1) * gw] = ((yy + bo) * gg).astype(o_ref.dtype)


def _rwkv_scan(r, lw, k, v, a, b, g, r_k, lnx_g, lnx_b, bsz, seq):
    n_groups = RWKV_W // RWKV_GW
    gps = RWKV_GROUPS_PER_STEP
    tok = pl.BlockSpec((bsz, RWKV_CHUNK, gps * RWKV_GW), lambda gg, i: (0, i, gg))
    vec = pl.BlockSpec((1, gps * RWKV_GW), lambda gg, i: (0, gg))
    masks, head_mask = _rwkv_masks()
    return pl.pallas_call(
        _rwkv_scan_kernel,
        grid=(n_groups // gps, seq // RWKV_CHUNK),
        in_specs=[tok] * 7 + [vec] * 3 + [
            pl.BlockSpec((3 + RWKV_LEVELS, RWKV_GW, RWKV_GW), lambda gg, i: (0, 0, 0)),
            pl.BlockSpec((RWKV_GW, RWKV_GW), lambda gg, i: (0, 0))],
        out_specs=tok,
        out_shape=jax.ShapeDtypeStruct((bsz, seq, RWKV_W), BF16),
        scratch_shapes=[pltpu.VMEM((bsz, gps, RWKV_GW, RWKV_GW), F32)],
        compiler_params=_cparams(2),
        name="rwkv_scan",
    )(r, lw, k, v, a, b, g, r_k, lnx_g, lnx_b, masks, head_mask)


def _merge_kernel(yc_ref, ya_ref, yr_ref, wb_ref, g0_ref, g1_ref, g2_ref, o_ref):
    acc = None
    for n, (y_ref, g_ref) in enumerate(((yc_ref, g0_ref), (ya_ref, g1_ref), (yr_ref, g2_ref))):
        term = _sigmoid(g_ref[...]) * _dg(y_ref[...], wb_ref[n])
        acc = term if acc is None else acc + term
    o_ref[...] = acc.astype(o_ref.dtype)


def _merge(y_conv, y_att, y_rwkv, w_branch, p_gate, tm=512, tn=1024):
    m = y_conv.shape[0]
    nj = D_MODEL // tn
    ysp = pl.BlockSpec((tm, BRANCH_W), lambda j, i: (i, 0))
    gsp = lambda n: pl.BlockSpec((tm, tn), lambda j, i: (i, n * nj + j))
    return pl.pallas_call(
        _merge_kernel,
        grid=(nj, m // tm),
        in_specs=[ysp, ysp, ysp, pl.BlockSpec((3, BRANCH_W, tn), lambda j, i: (0, 0, j)),
                  gsp(0), gsp(1), gsp(2)],
        out_specs=pl.BlockSpec((tm, tn), lambda j, i: (i, j)),
        out_shape=jax.ShapeDtypeStruct((m, D_MODEL), BF16),
        compiler_params=_cparams(2),
        name="branch_merge",
    )(y_conv, y_att, y_rwkv, w_branch, p_gate, p_gate, p_gate)


def _layer_norm(y, g, b):
    mu = jnp.mean(y, axis=-1, keepdims=True)
    d = y - mu
    var = jnp.mean(d * d, axis=-1, keepdims=True)
    return d * lax.rsqrt(var + LN_EPS) * g + b


def _pack_rows(x):
    hi = lax.bitcast_convert_type(x[:, :PACK_W].astype(BF16).astype(F32), U32)
    lo = lax.bitcast_convert_type(x[:, PACK_W:].astype(BF16).astype(F32), U32)
    return hi | (lo >> 16)


def _unpack_rows(w):
    hi = lax.bitcast_convert_type(w & jnp.uint32(0xFFFF0000), F32).astype(BF16)
    lo = lax.bitcast_convert_type(w << 16, F32).astype(BF16)
    return hi, lo


def _oproj_ln_kernel(m_ref, w_ref, x_ref, g_ref, b_ref, o_ref, opk_ref):
    y = DEEPNORM_ALPHA * x_ref[...] + _dg(m_ref[...], w_ref[...])
    out = _layer_norm(y, g_ref[...], b_ref[...])
    o_ref[...] = out
    opk_ref[...] = _pack_rows(out)


def _oproj_ln(merged, w_o, x, g, b, tm=256):
    m = x.shape[0]
    row = lambda n: pl.BlockSpec((tm, n), lambda i: (i, 0))
    vec = pl.BlockSpec((1, D_MODEL), lambda i: (0, 0))
    return pl.pallas_call(
        _oproj_ln_kernel,
        grid=(m // tm,),
        in_specs=[row(D_MODEL), pl.BlockSpec((D_MODEL, D_MODEL), lambda i: (0, 0)), row(D_MODEL), vec, vec],
        out_specs=[row(D_MODEL), row(PACK_W)],
        out_shape=[jax.ShapeDtypeStruct((m, D_MODEL), F32), jax.ShapeDtypeStruct((m, PACK_W), U32)],
        compiler_params=_cparams(1),
        name="out_proj_ln",
    )(merged, w_o, x, g, b)


def _first_max(x, idx, n):
    m = jnp.max(x, axis=0, keepdims=True)
    first = jnp.min(jnp.where(x == m, idx, float(n)), axis=0, keepdims=True)
    return m, first


def _router_kernel(x_ref, wr_ref, bias_ref, ti_ref, tw_ref, rk_ref, cnt_ref, run_ref, *, tm):
    @pl.when(pl.program_id(0) == 0)
    def _():
        run_ref[...] = jnp.zeros_like(run_ref)

    xh, xl = _split2(x_ref[...])
    wh, wl = _split2(wr_ref[...])
    logits = _dg(wh, xh, NT) + _dg(wh, xl, NT) + _dg(wl, xh, NT)
    scores = _sigmoid(logits)
    biased = scores + bias_ref[...]
    neg = -jnp.inf

    sub8 = lax.broadcasted_iota(I32, (GROUP_SIZE, tm), 0).astype(F32)
    grp_rows = []
    for gi in range(N_GROUPS):
        blk = biased[gi * GROUP_SIZE:(gi + 1) * GROUP_SIZE, :]
        m1, f1 = _first_max(blk, sub8, GROUP_SIZE)
        m2 = jnp.max(jnp.where(sub8 == f1, neg, blk), axis=0, keepdims=True)
        grp_rows.append(m1 + m2)
    cur = jnp.concatenate(grp_rows, axis=0)
    grp_io = lax.broadcasted_iota(I32, (N_GROUPS, tm), 0).astype(F32)
    keep = jnp.zeros((N_GROUPS, tm), F32)
    for _ in range(TOPK_GROUPS):
        _, f = _first_max(cur, grp_io, N_GROUPS)
        hit = grp_io == f
        keep = jnp.where(hit, 1.0, keep)
        cur = jnp.where(hit, neg, cur)
    exp_keep = jnp.concatenate(
        [jnp.broadcast_to(keep[gi:gi + 1, :], (GROUP_SIZE, tm)) for gi in range(N_GROUPS)], axis=0)
    cand = jnp.where(exp_keep > 0.5, biased, neg)

    exp_io = lax.broadcasted_iota(I32, (N_EXPERTS, tm), 0).astype(F32)
    sel = jnp.zeros((N_EXPERTS, tm), F32)
    idxs, vals = [], []
    for _ in range(TOP_K):
        _, f = _first_max(cand, exp_io, N_EXPERTS)
        hit = exp_io == f
        idxs.append(f)
        vals.append(jnp.sum(jnp.where(hit, scores, 0.0), axis=0, keepdims=True))
        sel = jnp.where(hit, 1.0, sel)
        cand = jnp.where(hit, neg, cand)
    top_s = jnp.concatenate(vals, axis=0)
    ti_ref[...] = jnp.concatenate(idxs, axis=0).astype(I32)
    tw_ref[...] = top_s / jnp.sum(top_s, axis=0, keepdims=True) * ROUTED_SCALE

    before = (lax.broadcasted_iota(I32, (tm, tm), 0) < lax.broadcasted_iota(I32, (tm, tm), 1)).astype(BF16)
    rank_all = _dg(sel.astype(BF16), before) + run_ref[...][:, 0:1]
    ranks = [jnp.sum(jnp.where(exp_io == f, rank_all, 0.0), axis=0, keepdims=True) for f in idxs]
    rk_ref[...] = jnp.concatenate(ranks, axis=0).astype(I32)
    run_ref[...] = run_ref[...] + jnp.sum(sel, axis=1, keepdims=True)
    cnt_ref[...] = run_ref[...]


def _router(x, wr_t, bias, tm=512):
    m = x.shape[0]
    slot = pl.BlockSpec((TOP_K, tm), lambda i: (0, i))
    return pl.pallas_call(
        functools.partial(_router_kernel, tm=tm),
        grid=(m // tm,),
        in_specs=[pl.BlockSpec((tm, D_MODEL), lambda i: (i, 0)),
                  pl.BlockSpec((N_EXPERTS, D_MODEL), lambda i: (0, 0)),
                  pl.BlockSpec((N_EXPERTS, 1), lambda i: (0, 0))],
        out_specs=[slot, slot, slot, pl.BlockSpec((N_EXPERTS, LANES), lambda i: (0, 0))],
        out_shape=[jax.ShapeDtypeStruct((TOP_K, m), I32), jax.ShapeDtypeStruct((TOP_K, m), F32),
                   jax.ShapeDtypeStruct((TOP_K, m), I32), jax.ShapeDtypeStruct((N_EXPERTS, LANES), F32)],
        scratch_shapes=[pltpu.VMEM((N_EXPERTS, LANES), F32)],
        compiler_params=_cparams(1),
        name="moe_router",
    )(x, wr_t, bias)


def _dispatch_kernel(pos_hbm, x_ref, xs_hbm, pos_smem, sem_idx, sem, *, tm):
    i = pl.program_id(0)
    cp = pltpu.make_async_copy(pos_hbm.at[i], pos_smem, sem_idx)
    cp.start()
    cp.wait()

    def row_copy(t, kk):
        return pltpu.make_async_copy(x_ref.at[pl.ds(t, 1), :], xs_hbm.at[pl.ds(pos_smem[kk, t], 1), :], sem)

    def issue(t, carry):
        for kk in range(TOP_K):
            row_copy(t, kk).start()
        return carry

    def drain(t, carry):
        for kk in range(TOP_K):
            row_copy(t, kk).wait()
        return carry

    lax.fori_loop(0, tm, issue, 0)
    lax.fori_loop(0, tm, drain, 0)


def _dispatch(pos_tiles, x_pk, tm):
    n_tiles = pos_tiles.shape[0]
    m = x_pk.shape[0]
    return pl.pallas_call(
        functools.partial(_dispatch_kernel, tm=tm),
        grid=(n_tiles,),
        in_specs=[pl.BlockSpec(memory_space=pl.ANY),
                  pl.BlockSpec((tm, PACK_W), lambda i: (i, 0))],
        out_specs=pl.BlockSpec(memory_space=pl.ANY),
        out_shape=jax.ShapeDtypeStruct((m * TOP_K, PACK_W), U32),
        scratch_shapes=[pltpu.SMEM((TOP_K, tm), I32), pltpu.SemaphoreType.DMA, pltpu.SemaphoreType.DMA],
        compiler_params=_cparams(1),
        name="moe_dispatch",
    )(pos_tiles, x_pk)


def _gmm_kernel(vt_ref, ve_ref, vlo_ref, vhi_ref, vfirst_ref, vnew_ref, x_ref, wg_ref, wu_ref, wd_ref, o_ref,
                wg_b, wu_b, wd_b):
    vi = pl.program_id(0)
    lo = vlo_ref[vi]
    hi = vhi_ref[vi]
    first = vfirst_ref[vi]

    @pl.when(vnew_ref[vi] == 1)
    def _():
        wg_b[...] = wg_ref[...].astype(BF16)
        wu_b[...] = wu_ref[...].astype(BF16)
        wd_b[...] = wd_ref[...].astype(BF16)

    @pl.when(hi > lo)
    def _():
        xa, xb = _unpack_rows(x_ref[...])
        gate = _dg(xa, wg_b[0:PACK_W, :]) + _dg(xb, wg_b[PACK_W:, :])
        up = _dg(xa, wu_b[0:PACK_W, :]) + _dg(xb, wu_b[PACK_W:, :])
        row = lax.broadcasted_iota(I32, (gate.shape[0], 1), 0)
        h = jnp.where((row >= lo) & (row < hi), gate * _sigmoid(gate) * up, 0.0)
        y = _dg(h.astype(BF16), wd_b[...])

        @pl.when(first == 1)
        def _():
            o_ref[...] = y

        @pl.when(first != 1)
        def _():
            o_ref[...] += y


def _grouped_mlp(visits, xs, w_gate, w_up, w_down, n_visits):
    rows = xs.shape[0]
    tm = GMM_TILE
    by_tile = lambda vi, vt, ve, vlo, vhi, vf, vn: (vt[vi], 0)
    by_expert = lambda vi, vt, ve, vlo, vhi, vf, vn: (ve[vi], 0, 0)
    grid_spec = pltpu.PrefetchScalarGridSpec(
        num_scalar_prefetch=6,
        grid=(n_visits,),
        in_specs=[pl.BlockSpec((tm, PACK_W), by_tile),
                  pl.BlockSpec((None, D_MODEL, EXPERT_FF), by_expert),
                  pl.BlockSpec((None, D_MODEL, EXPERT_FF), by_expert),
                  pl.BlockSpec((None, EXPERT_FF, D_MODEL), by_expert)],
        out_specs=pl.BlockSpec((tm, D_MODEL), by_tile),
        scratch_shapes=[pltpu.VMEM((D_MODEL, EXPERT_FF), BF16), pltpu.VMEM((D_MODEL, EXPERT_FF), BF16),
                        pltpu.VMEM((EXPERT_FF, D_MODEL), BF16)],
    )
    return pl.pallas_call(
        _gmm_kernel,
        grid_spec=grid_spec,
        out_shape=jax.ShapeDtypeStruct((rows, D_MODEL), F32),
        compiler_params=_cparams(1),
        name="moe_grouped_mlp",
    )(*visits, xs, w_gate, w_up, w_down)


def _visit_list(counts, n_rows, tm, n_visits):
    ends = jnp.cumsum(counts)
    offs = ends - counts
    first_tile = offs // tm
    last_tile = jnp.where(counts > 0, (ends - 1) // tm, first_tile - 1)
    n_t = last_tile - first_tile + 1
    v_end = jnp.cumsum(n_t)
    v_start = v_end - n_t
    total = v_end[-1]
    vi = jnp.arange(n_visits, dtype=I32)
    e = jnp.minimum(jnp.sum(v_end[None, :] <= vi[:, None], axis=1), N_EXPERTS - 1).astype(I32)
    tile = first_tile[e] + (vi - v_start[e])
    active = vi < total
    last_e = jnp.max(jnp.where(counts > 0, jnp.arange(N_EXPERTS, dtype=I32), 0))
    tile = jnp.where(active, tile, n_rows // tm - 1).astype(I32)
    e = jnp.where(active, e, last_e).astype(I32)
    lo = jnp.where(active, jnp.maximum(offs[e], tile * tm) - tile * tm, 0).astype(I32)
    hi = jnp.where(active, jnp.minimum(ends[e], (tile + 1) * tm) - tile * tm, 0).astype(I32)
    prev_tile = jnp.concatenate([jnp.full((1,), -1, I32), tile[:-1]])
    first = (active & (tile != prev_tile)).astype(I32)
    prev_e = jnp.concatenate([jnp.full((1,), -1, I32), e[:-1]])
    new_e = (e != prev_e).astype(I32)
    return tile, e, lo, hi, first, new_e


def _moe_finish_kernel(pos_hbm, ys_hbm, w_ref, xpk_ref, x_ref, wg_ref, wu_ref, wd_ref, g_ref, b_ref,
                       o_ref, obf_ref, pos_smem, buf, sem_idx, sem, *, tm):
    i = pl.program_id(0)
    cp_pos = pltpu.make_async_copy(pos_hbm.at[i], pos_smem, sem_idx)
    cp_pos.start()
    cp_pos.wait()

    def row_copy(t, kk):
        return pltpu.make_async_copy(ys_hbm.at[pl.ds(pos_smem[kk, t], 1), :], buf.at[kk, pl.ds(t, 1), :], sem)

    def issue(t, carry):
        for kk in range(TOP_K):
            row_copy(t, kk).start()
        return carry

    def drain(t, carry):
        for kk in range(TOP_K):
            row_copy(t, kk).wait()
        return carry

    lax.fori_loop(0, tm, issue, 0)

    xa, xb = _unpack_rows(xpk_ref[...])
    gate = _dg(xa, wg_ref[0:PACK_W, :]) + _dg(xb, wg_ref[PACK_W:, :])
    up = _dg(xa, wu_ref[0:PACK_W, :]) + _dg(xb, wu_ref[PACK_W:, :])
    h = gate * _sigmoid(gate) * up
    y = DEEPNORM_ALPHA * x_ref[...] + _dg(h.astype(BF16), wd_ref[...])

    lax.fori_loop(0, tm, drain, 0)
    w = w_ref[...]
    routed = w[:, 0:1] * buf[0]
    for kk in range(1, TOP_K):
        routed = routed + w[:, kk:kk + 1] * buf[kk]
    out = _layer_norm(y + routed, g_ref[...], b_ref[...])
    o_ref[...] = out
    obf_ref[...] = out.astype(BF16)


def _moe_finish(pos_tiles, ys, w_tok, x_pk, x, w_gate, w_up, w_down, g, b, tm):
    m = x.shape[0]
    row = lambda n: pl.BlockSpec((tm, n), lambda i: (i, 0))
    vec = pl.BlockSpec((1, D_MODEL), lambda i: (0, 0))
    return pl.pallas_call(
        functools.partial(_moe_finish_kernel, tm=tm),
        grid=(m // tm,),
        in_specs=[pl.BlockSpec(memory_space=pl.ANY), pl.BlockSpec(memory_space=pl.ANY),
                  row(TOP_K), row(PACK_W), row(D_MODEL),
                  pl.BlockSpec((D_MODEL, SHARED_FF), lambda i: (0, 0)),
                  pl.BlockSpec((D_MODEL, SHARED_FF), lambda i: (0, 0)),
                  pl.BlockSpec((SHARED_FF, D_MODEL), lambda i: (0, 0)), vec, vec],
        out_specs=[row(D_MODEL), row(D_MODEL)],
        out_shape=[jax.ShapeDtypeStruct((m, D_MODEL), F32), jax.ShapeDtypeStruct((m, D_MODEL), BF16)],
        scratch_shapes=[pltpu.SMEM((TOP_K, tm), I32), pltpu.VMEM((TOP_K, tm, D_MODEL), F32),
                        pltpu.SemaphoreType.DMA, pltpu.SemaphoreType.DMA],
        compiler_params=_cparams(1),
        name="moe_finish",
    )(pos_tiles, ys, w_tok, x_pk, x, w_gate, w_up, w_down, g, b)


def _pad_cols(w, n):
    return jnp.pad(w, ((0, 0), (0, n - w.shape[1])))


def _in_proj_weight(w_in_l):
    c0 = 6 * BRANCH_W
    rw = c0
    lora = rw + 3 * RWKV_W
    parts = [w_in_l[:, :c0],
             w_in_l[:, c0 + 3 * RWKV_W + DECAY_LORA + AAA_LORA + GATE_LORA:],
             w_in_l[:, rw:lora],
             _pad_cols(w_in_l[:, lora:lora + DECAY_LORA], LORA_PAD),
             _pad_cols(w_in_l[:, lora + DECAY_LORA:lora + DECAY_LORA + AAA_LORA], LORA_PAD),
             w_in_l[:, lora + DECAY_LORA + AAA_LORA:lora + DECAY_LORA + AAA_LORA + GATE_LORA]]
    return jnp.concatenate(parts, axis=1).astype(BF16)


def _rwkv_mix_vector(mix_l):
    lora = 3 * RWKV_W
    parts = [mix_l[:lora],
             jnp.pad(mix_l[lora:lora + DECAY_LORA], (0, LORA_PAD - DECAY_LORA)),
             jnp.pad(mix_l[lora + DECAY_LORA:lora + DECAY_LORA + AAA_LORA], (0, LORA_PAD - AAA_LORA)),
             mix_l[lora + DECAY_LORA + AAA_LORA:]]
    return jnp.concatenate(parts)[None, :]


def _pad_rows(w, n):
    return jnp.pad(w, ((0, n - w.shape[0]), (0, 0))).astype(BF16)


def _tile_major(a, tm):
    kk, t = a.shape
    return a.reshape(kk, t // tm, tm).transpose(1, 0, 2)


def kernel(x, w_in, conv_w, rwkv_mix, rwkv_w0, rwkv_w2, rwkv_a0, rwkv_a2, rwkv_g2, rwkv_kk, rwkv_ka, rwkv_rk,
           rwkv_lnx_g, rwkv_lnx_b, w_branch, w_o, ln1_g, ln1_b, w_router, router_bias, w_e_gate, w_e_up,
           w_e_down, w_s_gate, w_s_up, w_s_down, ln2_g, ln2_b):
    bsz, seq, d = x.shape
    n_tok = bsz * seq
    n_rows = n_tok * TOP_K
    n_visits = n_rows // GMM_TILE + N_EXPERTS
    io_tile = 256
    xf = x.reshape(n_tok, d)
    xb = xf.astype(BF16)
    row1 = lambda v: v[None, :]
    for l in range(DEPTH):
        w_in_r = _in_proj_weight(w_in[l])
        p_conv = _matmul_cols(xb, w_in_r, COL_CONV, 3 * BRANCH_W, 1024, 1024, F32)
        p_att = _matmul_cols(xb, w_in_r, COL_ATT, 3 * BRANCH_W, 1024, 1024, BF16)
        p_gate = _matmul_cols(xb, w_in_r, COL_GATE, 3 * D_MODEL, 1024, 1024, F32)
        p_rwkv = _matmul_cols(xb, w_in_r, COL_RWKV, RWKV_COLS, 512, 1024, F32)

        y_conv = _short_conv(p_conv, conv_w[l], bsz, seq)
        y_att = _stick_breaking(p_att, bsz, seq)
        prep = _rwkv_prep(p_rwkv, _rwkv_mix_vector(rwkv_mix[l]), row1(rwkv_w0[l]),
                          _pad_rows(rwkv_w2[l], LORA_PAD), row1(rwkv_a0[l]), _pad_rows(rwkv_a2[l], LORA_PAD),
                          rwkv_g2[l].astype(BF16), row1(rwkv_kk[l]), row1(rwkv_ka[l]), bsz, seq)
        y_rwkv = _rwkv_scan(*prep, rwkv_rk[l].reshape(1, RWKV_W), row1(rwkv_lnx_g[l]), row1(rwkv_lnx_b[l]),
                            bsz, seq)

        merged = _merge(y_conv.reshape(n_tok, BRANCH_W), y_att.reshape(n_tok, BRANCH_W),
                        y_rwkv.reshape(n_tok, BRANCH_W), w_branch[l].astype(BF16), p_gate)
        x1, x1_pk = _oproj_ln(merged, w_o[l].astype(BF16), xf, row1(ln1_g[l]), row1(ln1_b[l]))

        top_i, top_w, rank, cnt = _router(x1, w_router[l].T, router_bias[l][:, None])
        counts = cnt[:, 0].astype(I32)
        offs = jnp.cumsum(counts) - counts
        onehot = top_i[:, :, None] == jnp.arange(N_EXPERTS, dtype=I32)
        pos = jnp.sum(jnp.where(onehot, offs, 0), axis=-1) + rank
        pos_tiles = _tile_major(pos, io_tile)
        xs = _dispatch(pos_tiles, x1_pk, io_tile)
        visits = _visit_list(counts, n_rows, GMM_TILE, n_visits)
        ys = _grouped_mlp(visits, xs, w_e_gate[l], w_e_up[l], w_e_down[l], n_visits)
        xf, xb = _moe_finish(pos_tiles, ys, top_w.T, x1_pk, x1, w_s_gate[l].astype(BF16),
                             w_s_up[l].astype(BF16), w_s_down[l].astype(BF16), row1(ln2_g[l]), row1(ln2_b[l]),
                             io_tile)
    return xf.reshape(bsz, seq, d)
```

```python
import functools

import jax
import jax.numpy as jnp
from jax import lax
from jax.experimental import pallas as pl
from jax.experimental.pallas import tpu as pltpu

F32 = jnp.float32
BF16 = jnp.bfloat16
I32 = jnp.int32
U32 = jnp.uint32

D_MODEL = 2048
DEPTH = 2
BRANCH_W = 1024
CONV_K = 3
ATT_HEADS = 8
ATT_HD = 128
RWKV_HEADS = 16
RWKV_HD = 64
RWKV_W = RWKV_HEADS * RWKV_HD
DECAY_LORA = 96
AAA_LORA = 96
GATE_LORA = 256
LORA_PAD = 128
N_EXPERTS = 64
TOP_K = 8
N_GROUPS = 8
GROUP_SIZE = N_EXPERTS // N_GROUPS
TOPK_GROUPS = 4
EXPERT_FF = 512
SHARED_FF = 512
ROUTED_SCALE = 2.5
LN_EPS = 1e-5
GN_EPS = 64e-5
DEEPNORM_ALPHA = (2 * DEPTH) ** 0.25

COL_CONV = 0
COL_ATT = 3 * BRANCH_W
COL_GATE = 6 * BRANCH_W
COL_RWKV = COL_GATE + 3 * D_MODEL
RWKV_COLS = 3 * RWKV_W + 2 * LORA_PAD + GATE_LORA
IN_COLS_PAD = COL_RWKV + RWKV_COLS

RWKV_CHUNK = 64
RWKV_GROUP = 4
RWKV_GW = RWKV_GROUP * RWKV_HD
RWKV_GROUPS_PER_STEP = 2
ATT_TILE = 512
ATT_KEY_TILE = 256
GMM_TILE = 256
LANES = 128
PACK_W = D_MODEL // 2

VMEM_LIMIT = 56 * 1024 * 1024

NN = ((1,), (0,))
NT = ((1,), (1,))


def _cparams(n_axes):
    return pltpu.CompilerParams(dimension_semantics=("arbitrary",) * n_axes,
                                vmem_limit_bytes=VMEM_LIMIT)


def _dg(a, b, dims=NN):
    return lax.dot_general(a, b, (dims, ((), ())), preferred_element_type=F32)


def _split2(x):
    hi = x.astype(BF16)
    lo = (x - hi.astype(F32)).astype(BF16)
    return hi, lo


def _split3(x):
    hi = x.astype(BF16)
    r1 = x - hi.astype(F32)
    mid = r1.astype(BF16)
    lo = (r1 - mid.astype(F32)).astype(BF16)
    return hi, mid, lo


def _mm3(a, b, dims=NN):
    ah, al = _split2(a)
    bh, bl = _split2(b)
    return _dg(ah, bh, dims) + _dg(ah, bl, dims) + _dg(al, bh, dims)


def _sigmoid(x):
    return 1.0 / (1.0 + jnp.exp(-x))


def _softplus(x):
    return jnp.maximum(x, 0.0) + jnp.log(1.0 + jnp.exp(-jnp.abs(x)))


def _mm_kernel(x_ref, w_ref, o_ref):
    o_ref[...] = _dg(x_ref[...], w_ref[...]).astype(o_ref.dtype)


def _matmul_cols(x, w, col_off, ncols, tn, tm, out_dtype):
    m, k = x.shape
    off = col_off // tn
    return pl.pallas_call(
        _mm_kernel,
        grid=(ncols // tn, m // tm),
        in_specs=[pl.BlockSpec((tm, k), lambda j, i: (i, 0)),
                  pl.BlockSpec((k, tn), lambda j, i: (0, off + j))],
        out_specs=pl.BlockSpec((tm, tn), lambda j, i: (i, j)),
        out_shape=jax.ShapeDtypeStruct((m, ncols), out_dtype),
        compiler_params=_cparams(2),
        name="in_proj",
    )(x, w)


def _conv_kernel(gb_ref, gc_ref, hc_ref, gch_ref, hch_ref, cw_ref, o_ref):
    i = pl.program_id(1)
    u = gc_ref[...] * hc_ref[...]
    uh = gch_ref[...] * hch_ref[...]
    uh = jnp.where(i > 0, uh, 0.0)
    row = lax.broadcasted_iota(I32, u.shape, 0)
    u1 = jnp.where(row == 0, uh[7:8, :], pltpu.roll(u, 1, axis=0))
    u2 = pltpu.roll(u, 2, axis=0)
    u2 = jnp.where(row == 0, uh[6:7, :], jnp.where(row == 1, uh[7:8, :], u2))
    cw = cw_ref[...]
    y = gb_ref[...] * (cw[2:3, :] * u + cw[1:2, :] * u1 + cw[0:1, :] * u2)
    o_ref[...] = y.astype(o_ref.dtype)


def _short_conv(p_conv, conv_w, bsz, seq, ts=512):
    p3 = p_conv.reshape(bsz, seq, 3 * BRANCH_W)
    hb = ts // 8
    halo = lambda c: pl.BlockSpec((None, 8, BRANCH_W),
                                  lambda b, i: (b, jnp.maximum(i * hb - 1, 0), c))
    main = lambda c: pl.BlockSpec((None, ts, BRANCH_W), lambda b, i: (b, i, c))
    return pl.pallas_call(
        _conv_kernel,
        grid=(bsz, seq // ts),
        in_specs=[main(0), main(1), main(2), halo(1), halo(2),
                  pl.BlockSpec((CONV_K, BRANCH_W), lambda b, i: (0, 0))],
        out_specs=pl.BlockSpec((None, ts, BRANCH_W), lambda b, i: (b, i, 0)),
        out_shape=jax.ShapeDtypeStruct((bsz, seq, BRANCH_W), BF16),
        compiler_params=_cparams(2),
        name="short_conv",
    )(p3, p3, p3, p3, p3, conv_w)


def _attn_kernel(q_ref, k_ref, v_ref, o_ref, *, tq, tk, scale):
    i = pl.program_id(2)
    q = q_ref[...]
    n_sub = tq // tk
    suffix = (lax.broadcasted_iota(I32, (tk, tk), 0)
              > lax.broadcasted_iota(I32, (tk, tk), 1)).astype(BF16)
    suffix2 = jnp.concatenate([suffix, suffix], axis=0)
    r_io = lax.broadcasted_iota(I32, (tq, tk), 0)
    c_io = lax.broadcasted_iota(I32, (tq, tk), 1)

    def block(j, carry, diag_off):
        acc, c = carry
        start = pl.multiple_of(j * tk, tk)
        kj = k_ref[pl.ds(start, tk), :]
        vj = v_ref[pl.ds(start, tk), :]
        z = _dg(q, kj, NT) * scale
        sp = jnp.log(1.0 + jnp.exp(-jnp.abs(z)))
        log_beta = jnp.minimum(z, 0.0) - sp
        log_1mb = -jnp.maximum(z, 0.0) - sp
        if diag_off is not None:
            causal = (c_io + diag_off * tk) < r_io
            log_1mb = jnp.where(causal, log_1mb, 0.0)
        hi, lo = _split2(log_1mb)
        tail = _dg(jnp.concatenate([hi, lo], axis=1), suffix2) + c
        w = jnp.exp(log_beta + tail)
        if diag_off is not None:
            w = jnp.where(causal, w, 0.0)
        acc = acc + _dg(w.astype(BF16), vj)
        c = c + jnp.sum(log_1mb, axis=1, keepdims=True)
        return acc, c

    carry = (jnp.zeros((tq, ATT_HD), F32), jnp.zeros((tq, 1), F32))
    n_below = n_sub * i
    for d in reversed(range(n_sub)):
        carry = block(n_below + d, carry, d)
    carry = lax.fori_loop(0, n_below, lambda jj, cr: block(n_below - 1 - jj, cr, None), carry)
    o_ref[...] = carry[0].astype(o_ref.dtype)


def _stick_breaking(p_att, bsz, seq):
    p3 = p_att.reshape(bsz, seq, 3 * BRANCH_W)
    tq = ATT_TILE
    kern = functools.partial(_attn_kernel, tq=tq, tk=ATT_KEY_TILE, scale=ATT_HD ** -0.5)
    return pl.pallas_call(
        kern,
        grid=(bsz, ATT_HEADS, seq // tq),
        in_specs=[pl.BlockSpec((None, tq, ATT_HD), lambda b, h, i: (b, i, h)),
                  pl.BlockSpec((None, seq, ATT_HD), lambda b, h, i: (b, 0, ATT_HEADS + h)),
                  pl.BlockSpec((None, seq, ATT_HD), lambda b, h, i: (b, 0, 2 * ATT_HEADS + h))],
        out_specs=pl.BlockSpec((None, tq, ATT_HD), lambda b, h, i: (b, i, h)),
        out_shape=jax.ShapeDtypeStruct((bsz, seq, BRANCH_W), BF16),
        compiler_params=_cparams(3),
        name="stick_breaking_attention",
    )(p3, p3, p3)


def _head_sum(x, ones_bd):
    hi, lo = _split2(x)
    outs = []
    for c in range(x.shape[1] // RWKV_GW):
        sl = slice(RWKV_GW * c, RWKV_GW * (c + 1))
        outs.append(_dg(hi[:, sl], ones_bd) + _dg(lo[:, sl], ones_bd))
    return outs[0] if len(outs) == 1 else jnp.concatenate(outs, axis=1)


def _head_ones():
    r = lax.broadcasted_iota(I32, (RWKV_GW, RWKV_GW), 0) // RWKV_HD
    c = lax.broadcasted_iota(I32, (RWKV_GW, RWKV_GW), 1) // RWKV_HD
    return (r == c).astype(BF16)


def _rwkv_prep_kernel(p_ref, ph_ref, mix_ref, w0_ref, w2_ref, a0_ref, a2_ref, g2_ref, kk_ref, ka_ref,
                      r_ref, lw_ref, k_ref, v_ref, a_ref, b_ref, g_ref):
    i = pl.program_id(1)
    p = p_ref[...]
    prev = jnp.where(i > 0, ph_ref[...][7:8, :], 0.0)
    row = lax.broadcasted_iota(I32, p.shape, 0)
    shifted = jnp.where(row == 0, prev, pltpu.roll(p, 1, axis=0))
    ps = p + (shifted - p) * mix_ref[...]
    w = RWKV_W
    r = ps[:, 0:w]
    k = ps[:, w:2 * w]
    v = ps[:, 2 * w:3 * w]
    lw = ps[:, 3 * w:3 * w + LORA_PAD]
    la = ps[:, 3 * w + LORA_PAD:3 * w + 2 * LORA_PAD]
    lg = ps[:, 3 * w + 2 * LORA_PAD:]
    dec_in = w0_ref[...] + _dg(jnp.tanh(lw).astype(BF16), w2_ref[...])
    w_log = -_softplus(-dec_in) - 0.5
    a = _sigmoid(a0_ref[...] + _dg(la.astype(BF16), a2_ref[...]))
    g = _dg(_sigmoid(lg).astype(BF16), g2_ref[...])
    kk = k * kk_ref[...]
    norm = jnp.sqrt(_head_sum(kk * kk, _head_ones()))
    kk = kk / jnp.maximum(norm, 1e-12)
    r_ref[...] = r
    lw_ref[...] = -jnp.exp(w_log)
    k_ref[...] = k * (1.0 + (a - 1.0) * ka_ref[...])
    v_ref[...] = v
    a_ref[...] = -kk
    b_ref[...] = kk * a
    g_ref[...] = g


def _rwkv_prep(p_rwkv, mix, w0, w2, a0, a2, g2, k_k, k_a, bsz, seq, ts=256):
    p3 = p_rwkv.reshape(bsz, seq, RWKV_COLS)
    hb = ts // 8
    vec = lambda n: pl.BlockSpec((1, n), lambda b, i: (0, 0))
    mat = lambda r: pl.BlockSpec((r, RWKV_W), lambda b, i: (0, 0))
    out = pl.BlockSpec((None, ts, RWKV_W), lambda b, i: (b, i, 0))
    return pl.pallas_call(
        _rwkv_prep_kernel,
        grid=(bsz, seq // ts),
        in_specs=[pl.BlockSpec((None, ts, RWKV_COLS), lambda b, i: (b, i, 0)),
                  pl.BlockSpec((None, 8, RWKV_COLS), lambda b, i: (b, jnp.maximum(i * hb - 1, 0), 0)),
                  vec(RWKV_COLS), vec(RWKV_W), mat(LORA_PAD), vec(RWKV_W), mat(LORA_PAD),
                  mat(GATE_LORA), vec(RWKV_W), vec(RWKV_W)],
        out_specs=[out] * 7,
        out_shape=[jax.ShapeDtypeStruct((bsz, seq, RWKV_W), F32)] * 7,
        compiler_params=_cparams(2),
        name="rwkv_prep",
    )(p3, p3, mix, w0, w2, a0, a2, g2, k_k, k_a)


RWKV_LEVELS = RWKV_CHUNK.bit_length() - 1


def _rwkv_masks():
    r = jnp.arange(RWKV_GW)[:, None]
    c = jnp.arange(RWKV_GW)[None, :]
    bd = (r // RWKV_CHUNK) == (c // RWKV_HD)
    planes = [bd, bd & (r > c), bd & (r >= c)]
    for m in range(RWKV_LEVELS):
        s = 1 << m
        planes.append((r // (2 * s) == c // (2 * s)) & (r % (2 * s) >= s) & (c % (2 * s) < s))
    return jnp.stack(planes).astype(F32), bd.astype(BF16)


def _rwkv_scan_kernel(r_ref, lw_ref, k_ref, v_ref, a_ref, b_ref, g_ref, rk_ref, lng_ref, lnb_ref,
                      msk_ref, bdb_ref, o_ref, s_ref):
    @pl.when(pl.program_id(1) == 0)
    def _():
        s_ref[...] = jnp.zeros_like(s_ref)

    c_len = RWKV_CHUNK
    gw = RWKV_GW
    chains = [(bb, gi) for bb in range(r_ref.shape[0]) for gi in range(r_ref.shape[2] // gw)]

    def smap(f, *lists):
        return [f(*xs) for xs in zip(*lists)]

    def tok(ref):
        return [ref[bb, :, gi * gw:(gi + 1) * gw] for bb, gi in chains]

    def par(ref):
        return [ref[:, gi * gw:(gi + 1) * gw] for _, gi in chains]

    def tile_heads(x):
        return jnp.concatenate([x.astype(BF16)] * RWKV_GROUP, axis=0)

    def unstack(z):
        out = z[0:c_len]
        for h in range(1, RWKV_GROUP):
            out = out + z[h * c_len:(h + 1) * c_len]
        return out

    r, lw, k, v, a, b = tok(r_ref), tok(lw_ref), tok(k_ref), tok(v_ref), tok(a_ref), tok(b_ref)
    tril_c = (lax.broadcasted_iota(I32, (c_len, c_len), 0)
              >= lax.broadcasted_iota(I32, (c_len, c_len), 1)).astype(BF16)

    def cum_decay(x):
        hi, mid, lo = _split3(x)
        return _dg(tril_c, hi) + _dg(tril_c, mid) + _dg(tril_c, lo)

    cum = smap(cum_decay, lw)
    cum_end = smap(lambda c: c[c_len - 1:c_len, :], cum)
    e_inv = smap(lambda c: jnp.exp(-c), cum)
    e_rem = smap(lambda ce, c: jnp.exp(ce - c), cum_end, cum)

    bd = msk_ref[0]
    strict = msk_ref[1]
    incl = msk_ref[2]
    head_b = bdb_ref[...]
    a_s = smap(lambda x, c, l: tile_heads(x * jnp.exp(c - l)) * head_b, a, cum, lw)
    r_s = smap(lambda x, c: tile_heads(x * jnp.exp(c)) * head_b, r, cum)
    b_x = smap(lambda x, e: tile_heads(x * e), b, e_inv)
    k_x = smap(lambda x, e: tile_heads(x * e), k, e_inv)
    v_x = smap(tile_heads, v)

    a_ab = smap(lambda p, q: _dg(p, q, NT) * strict, a_s, b_x)
    a_ak = smap(lambda p, q: (_dg(p, q, NT) * strict).astype(BF16), a_s, k_x)
    a_rb = smap(lambda p, q: (_dg(p, q, NT) * incl).astype(BF16), r_s, b_x)
    a_rk = smap(lambda p, q: (_dg(p, q, NT) * incl).astype(BF16), r_s, k_x)

    inv = smap(lambda m: (incl - strict) + m * msk_ref[3], a_ab)
    for lvl in range(1, RWKV_LEVELS):
        inv_b = smap(lambda t: t.astype(BF16), inv)
        low = smap(lambda m: (m * msk_ref[3 + lvl]).astype(BF16), a_ab)
        mid = smap(lambda t, lo_: _dg(t, lo_).astype(BF16), inv_b, low)
        inv = smap(lambda t, md, tb: t + _dg(md, tb), inv, mid, inv_b)
    inv_b = smap(lambda t: t.astype(BF16), inv)

    s0 = [s_ref[bb, gi] for bb, gi in chains]
    s0_b = smap(lambda s: s.astype(BF16), s0)
    rhs = smap(lambda p, s, m, vv: (_dg(p, s, NT) + _dg(m, vv)).astype(BF16), a_s, s0_b, a_ak, v_x)
    u = smap(_dg, inv_b, rhs)
    u_b = smap(lambda t: t.astype(BF16), u)
    y = smap(lambda p, s, m1, ub, m2, vv: unstack((_dg(p, s, NT) + _dg(m1, ub) + _dg(m2, vv)) * bd),
             r_s, s0_b, a_rb, u_b, a_rk, v_x)
    uv_t = smap(lambda uu, vv: jnp.concatenate([unstack(uu * bd), vv], axis=0).T.astype(BF16), u, v)
    bk_h = smap(lambda bb_, kk_, e: jnp.concatenate([bb_ * e, kk_ * e], axis=0).astype(BF16), b, k, e_rem)
    s_new = smap(lambda s, ce, p, q: s * jnp.exp(ce) + _dg(p, q) * bd, s0, cum_end, uv_t, bk_h)
    for (bb, gi), s in zip(chains, s_new):
        s_ref[bb, gi] = s

    ones_bd = _head_ones()
    inv_n = 1.0 / RWKV_HD
    d = smap(lambda yy: yy - _head_sum(yy, ones_bd) * inv_n, y)
    var = smap(lambda dd: _head_sum(dd * dd, ones_bd) * inv_n, d)
    yn = smap(lambda dd, vr, gg, bb_: dd * lax.rsqrt(vr + GN_EPS) * gg + bb_, d, var, par(lng_ref), par(lnb_ref))
    bonus = smap(lambda rr, kk_, rk, vv: _head_sum(rr * kk_ * rk, ones_bd) * vv, r, k, par(rk_ref), v)
    for (bb, gi), yy, bo, gg in zip(chains, yn, bonus, tok(g_ref)):
        o_ref[bb, :, gi * gw:(gi + 1) * gw] = ((yy + bo) * gg).astype(o_ref.dtype)


def _rwkv_scan(r, lw, k, v, a, b, g, r_k, lnx_g, lnx_b, bsz, seq):
    n_groups = RWKV_W // RWKV_GW
    gps = RWKV_GROUPS_PER_STEP
    tok = pl.BlockSpec((bsz, RWKV_CHUNK, gps * RWKV_GW), lambda gg, i: (0, i, gg))
    vec = pl.BlockSpec((1, gps * RWKV_GW), lambda gg, i: (0, gg))
    masks, head_mask = _rwkv_masks()
    return pl.pallas_call(
        _rwkv_scan_kernel,
        grid=(n_groups // gps, seq // RWKV_CHUNK),
        in_specs=[tok] * 7 + [vec] * 3 + [
            pl.BlockSpec((3 + RWKV_LEVELS, RWKV_GW, RWKV_GW), lambda gg, i: (0, 0, 0)),
            pl.BlockSpec((RWKV_GW, RWKV_GW), lambda gg, i: (0, 0))],
        out_specs=tok,
        out_shape=jax.ShapeDtypeStruct((bsz, seq, RWKV_W), BF16),
        scratch_shapes=[pltpu.VMEM((bsz, gps, RWKV_GW, RWKV_GW), F32)],
        compiler_params=_cparams(2),
        name="rwkv_scan",
    )(r, lw, k, v, a, b, g, r_k, lnx_g, lnx_b, masks, head_mask)


def _merge_kernel(yc_ref, ya_ref, yr_ref, wb_ref, g0_ref, g1_ref, g2_ref, o_ref):
    acc = None
    for n, (y_ref, g_ref) in enumerate(((yc_ref, g0_ref), (ya_ref, g1_ref), (yr_ref, g2_ref))):
        term = _sigmoid(g_ref[...]) * _dg(y_ref[...], wb_ref[n])
        acc = term if acc is None else acc + term
    o_ref[...] = acc.astype(o_ref.dtype)


def _merge(y_conv, y_att, y_rwkv, w_branch, p_gate, tm=512, tn=1024):
    m = y_conv.shape[0]
    nj = D_MODEL // tn
    ysp = pl.BlockSpec((tm, BRANCH_W), lambda j, i: (i, 0))
    gsp = lambda n: pl.BlockSpec((tm, tn), lambda j, i: (i, n * nj + j))
    return pl.pallas_call(
        _merge_kernel,
        grid=(nj, m // tm),
        in_specs=[ysp, ysp, ysp, pl.BlockSpec((3, BRANCH_W, tn), lambda j, i: (0, 0, j)),
                  gsp(0), gsp(1), gsp(2)],
        out_specs=pl.BlockSpec((tm, tn), lambda j, i: (i, j)),
        out_shape=jax.ShapeDtypeStruct((m, D_MODEL), BF16),
        compiler_params=_cparams(2),
        name="branch_merge",
    )(y_conv, y_att, y_rwkv, w_branch, p_gate, p_gate, p_gate)


def _layer_norm(y, g, b):
    mu = jnp.mean(y, axis=-1, keepdims=True)
    d = y - mu
    var = jnp.mean(d * d, axis=-1, keepdims=True)
    return d * lax.rsqrt(var + LN_EPS) * g + b


def _pack_rows(x):
    hi = lax.bitcast_convert_type(x[:, :PACK_W].astype(BF16).astype(F32), U32)
    lo = lax.bitcast_convert_type(x[:, PACK_W:].astype(BF16).astype(F32), U32)
    return hi | (lo >> 16)


def _unpack_rows(w):
    hi = lax.bitcast_convert_type(w & jnp.uint32(0xFFFF0000), F32).astype(BF16)
    lo = lax.bitcast_convert_type(w << 16, F32).astype(BF16)
    return hi, lo


def _oproj_ln_kernel(m_ref, w_ref, x_ref, g_ref, b_ref, o_ref, opk_ref):
    y = DEEPNORM_ALPHA * x_ref[...] + _dg(m_ref[...], w_ref[...])
    out = _layer_norm(y, g_ref[...], b_ref[...])
    o_ref[...] = out
    opk_ref[...] = _pack_rows(out)


def _oproj_ln(merged, w_o, x, g, b, tm=256):
    m = x.shape[0]
    row = lambda n: pl.BlockSpec((tm, n), lambda i: (i, 0))
    vec = pl.BlockSpec((1, D_MODEL), lambda i: (0, 0))
    return pl.pallas_call(
        _oproj_ln_kernel,
        grid=(m // tm,),
        in_specs=[row(D_MODEL), pl.BlockSpec((D_MODEL, D_MODEL), lambda i: (0, 0)), row(D_MODEL), vec, vec],
        out_specs=[row(D_MODEL), row(PACK_W)],
        out_shape=[jax.ShapeDtypeStruct((m, D_MODEL), F32), jax.ShapeDtypeStruct((m, PACK_W), U32)],
        compiler_params=_cparams(1),
        name="out_proj_ln",
    )(merged, w_o, x, g, b)


def _first_max(x, idx, n):
    m = jnp.max(x, axis=0, keepdims=True)
    first = jnp.min(jnp.where(x == m, idx, float(n)), axis=0, keepdims=True)
    return m, first


def _router_kernel(x_ref, wr_ref, bias_ref, ti_ref, tw_ref, rk_ref, cnt_ref, run_ref, *, tm):
    @pl.when(pl.program_id(0) == 0)
    def _():
        run_ref[...] = jnp.zeros_like(run_ref)

    xh, xl = _split2(x_ref[...])
    wh, wl = _split2(wr_ref[...])
    logits = _dg(wh, xh, NT) + _dg(wh, xl, NT) + _dg(wl, xh, NT)
    scores = _sigmoid(logits)
    biased = scores + bias_ref[...]
    neg = -jnp.inf

    sub8 = lax.broadcasted_iota(I32, (GROUP_SIZE, tm), 0).astype(F32)
    grp_rows = []
    for gi in range(N_GROUPS):
        blk = biased[gi * GROUP_SIZE:(gi + 1) * GROUP_SIZE, :]
        m1, f1 = _first_max(blk, sub8, GROUP_SIZE)
        m2 = jnp.max(jnp.where(sub8 == f1, neg, blk), axis=0, keepdims=True)
        grp_rows.append(m1 + m2)
    cur = jnp.concatenate(grp_rows, axis=0)
    grp_io = lax.broadcasted_iota(I32, (N_GROUPS, tm), 0).astype(F32)
    keep = jnp.zeros((N_GROUPS, tm), F32)
    for _ in range(TOPK_GROUPS):
        _, f = _first_max(cur, grp_io, N_GROUPS)
        hit = grp_io == f
        keep = jnp.where(hit, 1.0, keep)
        cur = jnp.where(hit, neg, cur)
    exp_keep = jnp.concatenate(
        [jnp.broadcast_to(keep[gi:gi + 1, :], (GROUP_SIZE, tm)) for gi in range(N_GROUPS)], axis=0)
    cand = jnp.where(exp_keep > 0.5, biased, neg)

    exp_io = lax.broadcasted_iota(I32, (N_EXPERTS, tm), 0).astype(F32)
    sel = jnp.zeros((N_EXPERTS, tm), F32)
    idxs, vals = [], []
    for _ in range(TOP_K):
        _, f = _first_max(cand, exp_io, N_EXPERTS)
        hit = exp_io == f
        idxs.append(f)
        vals.append(jnp.sum(jnp.where(hit, scores, 0.0), axis=0, keepdims=True))
        sel = jnp.where(hit, 1.0, sel)
        cand = jnp.where(hit, neg, cand)
    top_s = jnp.concatenate(vals, axis=0)
    ti_ref[...] = jnp.concatenate(idxs, axis=0).astype(I32)
    tw_ref[...] = top_s / jnp.sum(top_s, axis=0, keepdims=True) * ROUTED_SCALE

    before = (lax.broadcasted_iota(I32, (tm, tm), 0) < lax.broadcasted_iota(I32, (tm, tm), 1)).astype(BF16)
    rank_all = _dg(sel.astype(BF16), before) + run_ref[...][:, 0:1]
    ranks = [jnp.sum(jnp.where(exp_io == f, rank_all, 0.0), axis=0, keepdims=True) for f in idxs]
    rk_ref[...] = jnp.concatenate(ranks, axis=0).astype(I32)
    run_ref[...] = run_ref[...] + jnp.sum(sel, axis=1, keepdims=True)
    cnt_ref[...] = run_ref[...]


def _router(x, wr_t, bias, tm=512):
    m = x.shape[0]
    slot = pl.BlockSpec((TOP_K, tm), lambda i: (0, i))
    return pl.pallas_call(
        functools.partial(_router_kernel, tm=tm),
        grid=(m // tm,),
        in_specs=[pl.BlockSpec((tm, D_MODEL), lambda i: (i, 0)),
                  pl.BlockSpec((N_EXPERTS, D_MODEL), lambda i: (0, 0)),
                  pl.BlockSpec((N_EXPERTS, 1), lambda i: (0, 0))],
        out_specs=[slot, slot, slot, pl.BlockSpec((N_EXPERTS, LANES), lambda i: (0, 0))],
        out_shape=[jax.ShapeDtypeStruct((TOP_K, m), I32), jax.ShapeDtypeStruct((TOP_K, m), F32),
                   jax.ShapeDtypeStruct((TOP_K, m), I32), jax.ShapeDtypeStruct((N_EXPERTS, LANES), F32)],
        scratch_shapes=[pltpu.VMEM((N_EXPERTS, LANES), F32)],
        compiler_params=_cparams(1),
        name="moe_router",
    )(x, wr_t, bias)


def _positions_kernel(offs_ref, ti_ref, rk_ref, o_ref):
    ti = ti_ref[...]
    acc = rk_ref[...]
    for e in range(N_EXPERTS):
        acc = acc + jnp.where(ti == e, offs_ref[e], 0)
    o_ref[...] = acc


def _positions(offs, top_i, rank, tm):
    m = top_i.shape[1]
    slot = pl.BlockSpec((TOP_K, tm), lambda i, offs_ref: (0, i))
    return pl.pallas_call(
        _positions_kernel,
        grid_spec=pltpu.PrefetchScalarGridSpec(
            num_scalar_prefetch=1, grid=(m // tm,), in_specs=[slot, slot],
            out_specs=pl.BlockSpec((None, TOP_K, tm), lambda i, offs_ref: (i, 0, 0))),
        out_shape=jax.ShapeDtypeStruct((m // tm, TOP_K, tm), I32),
        compiler_params=_cparams(1),
        name="moe_positions",
    )(offs, top_i, rank)


def _dispatch_kernel(pos_hbm, x_ref, xs_hbm, pos_smem, sem_idx, sem, *, tm):
    i = pl.program_id(0)
    cp = pltpu.make_async_copy(pos_hbm.at[i], pos_smem, sem_idx)
    cp.start()
    cp.wait()

    def row_copy(t, kk):
        return pltpu.make_async_copy(x_ref.at[pl.ds(t, 1), :], xs_hbm.at[pl.ds(pos_smem[kk, t], 1), :], sem)

    def issue(t, carry):
        for kk in range(TOP_K):
            row_copy(t, kk).start()
        return carry

    lax.fori_loop(0, tm, issue, 0)
    for kk in range(TOP_K):
        pltpu.make_async_copy(x_ref, xs_hbm.at[pl.ds(0, tm), :], sem).wait()


def _dispatch(pos_tiles, x_pk, tm):
    n_tiles = pos_tiles.shape[0]
    m = x_pk.shape[0]
    return pl.pallas_call(
        functools.partial(_dispatch_kernel, tm=tm),
        grid=(n_tiles,),
        in_specs=[pl.BlockSpec(memory_space=pl.ANY),
                  pl.BlockSpec((tm, PACK_W), lambda i: (i, 0))],
        out_specs=pl.BlockSpec(memory_space=pl.ANY),
        out_shape=jax.ShapeDtypeStruct((m * TOP_K, PACK_W), U32),
        scratch_shapes=[pltpu.SMEM((TOP_K, tm), I32), pltpu.SemaphoreType.DMA, pltpu.SemaphoreType.DMA],
        compiler_params=_cparams(1),
        name="moe_dispatch",
    )(pos_tiles, x_pk)


def _gmm_kernel(vt_ref, ve_ref, vlo_ref, vhi_ref, vfirst_ref, vnew_ref, x_ref, wg_ref, wu_ref, wd_ref, o_ref,
                wg_b, wu_b, wd_b):
    vi = pl.program_id(0)
    lo = vlo_ref[vi]
    hi = vhi_ref[vi]
    first = vfirst_ref[vi]

    @pl.when(vnew_ref[vi] == 1)
    def _():
        wg_b[...] = wg_ref[...].astype(BF16)
        wu_b[...] = wu_ref[...].astype(BF16)
        wd_b[...] = wd_ref[...].astype(BF16)

    @pl.when(hi > lo)
    def _():
        n_half = 2
        hr = x_ref.shape[0] // n_half
        halves = [slice(s * hr, (s + 1) * hr) for s in range(n_half)]
        xs = [_unpack_rows(x_ref[sl, :]) for sl in halves]
        gate = [_dg(xa, wg_b[0:PACK_W, :]) + _dg(xb, wg_b[PACK_W:, :]) for xa, xb in xs]
        up = [_dg(xa, wu_b[0:PACK_W, :]) + _dg(xb, wu_b[PACK_W:, :]) for xa, xb in xs]
        row = lax.broadcasted_iota(I32, (hr, 1), 0)
        h = [jnp.where((row + s * hr >= lo) & (row + s * hr < hi), g * _sigmoid(g) * u, 0.0).astype(BF16)
             for s, (g, u) in enumerate(zip(gate, up))]
        y = [_dg(hh, wd_b[...]) for hh in h]

        @pl.when(first == 1)
        def _():
            for sl, yy in zip(halves, y):
                o_ref[sl, :] = yy

        @pl.when(first != 1)
        def _():
            for sl, yy in zip(halves, y):
                o_ref[sl, :] += yy


def _grouped_mlp(visits, xs, w_gate, w_up, w_down, layer, n_visits):
    rows = xs.shape[0]
    tm = GMM_TILE
    by_tile = lambda vi, vt, ve, vlo, vhi, vf, vn: (vt[vi], 0)
    by_expert = lambda vi, vt, ve, vlo, vhi, vf, vn: (layer, ve[vi], 0, 0)
    grid_spec = pltpu.PrefetchScalarGridSpec(
        num_scalar_prefetch=6,
        grid=(n_visits,),
        in_specs=[pl.BlockSpec((tm, PACK_W), by_tile),
                  pl.BlockSpec((None, None, D_MODEL, EXPERT_FF), by_expert),
                  pl.BlockSpec((None, None, D_MODEL, EXPERT_FF), by_expert),
                  pl.BlockSpec((None, None, EXPERT_FF, D_MODEL), by_expert)],
        out_specs=pl.BlockSpec((tm, D_MODEL), by_tile),
        scratch_shapes=[pltpu.VMEM((D_MODEL, EXPERT_FF), BF16), pltpu.VMEM((D_MODEL, EXPERT_FF), BF16),
                        pltpu.VMEM((EXPERT_FF, D_MODEL), BF16)],
    )
    return pl.pallas_call(
        _gmm_kernel,
        grid_spec=grid_spec,
        out_shape=jax.ShapeDtypeStruct((rows, D_MODEL), F32),
        compiler_params=_cparams(1),
        name="moe_grouped_mlp",
    )(*visits, xs, w_gate, w_up, w_down)


def _visit_list(counts, n_rows, tm, n_visits):
    ends = jnp.cumsum(counts)
    offs = ends - counts
    first_tile = offs // tm
    last_tile = jnp.where(counts > 0, (ends - 1) // tm, first_tile - 1)
    n_t = last_tile - first_tile + 1
    v_end = jnp.cumsum(n_t)
    v_start = v_end - n_t
    total = v_end[-1]
    vi = jnp.arange(n_visits, dtype=I32)
    e = jnp.minimum(jnp.sum(v_end[None, :] <= vi[:, None], axis=1), N_EXPERTS - 1).astype(I32)
    tile = first_tile[e] + (vi - v_start[e])
    active = vi < total
    last_e = jnp.max(jnp.where(counts > 0, jnp.arange(N_EXPERTS, dtype=I32), 0))
    tile = jnp.where(active, tile, n_rows // tm - 1).astype(I32)
    e = jnp.where(active, e, last_e).astype(I32)
    lo = jnp.where(active, jnp.maximum(offs[e], tile * tm) - tile * tm, 0).astype(I32)
    hi = jnp.where(active, jnp.minimum(ends[e], (tile + 1) * tm) - tile * tm, 0).astype(I32)
    prev_tile = jnp.concatenate([jnp.full((1,), -1, I32), tile[:-1]])
    first = (active & (tile != prev_tile)).astype(I32)
    prev_e = jnp.concatenate([jnp.full((1,), -1, I32), e[:-1]])
    new_e = (e != prev_e).astype(I32)
    return tile, e, lo, hi, first, new_e


def _moe_finish_kernel(pos_hbm, ys_hbm, w_ref, xpk_ref, x_ref, wg_ref, wu_ref, wd_ref, g_ref, b_ref,
                       o_ref, obf_ref, pos_smem, buf, sem_idx, sem, *, tm):
    i = pl.program_id(0)
    cp_pos = pltpu.make_async_copy(pos_hbm.at[i], pos_smem, sem_idx)
    cp_pos.start()
    cp_pos.wait()

    def row_copy(t, kk):
        return pltpu.make_async_copy(ys_hbm.at[pl.ds(pos_smem[kk, t], 1), :], buf.at[kk, pl.ds(t, 1), :], sem)

    def issue(t, carry):
        for kk in range(TOP_K):
            row_copy(t, kk).start()
        return carry

    lax.fori_loop(0, tm, issue, 0)

    xa, xb = _unpack_rows(xpk_ref[...])
    gate = _dg(xa, wg_ref[0:PACK_W, :]) + _dg(xb, wg_ref[PACK_W:, :])
    up = _dg(xa, wu_ref[0:PACK_W, :]) + _dg(xb, wu_ref[PACK_W:, :])
    h = gate * _sigmoid(gate) * up
    y = DEEPNORM_ALPHA * x_ref[...] + _dg(h.astype(BF16), wd_ref[...])

    pltpu.make_async_copy(buf, buf, sem).wait()
    w = w_ref[...]
    routed = w[:, 0:1] * buf[0]
    for kk in range(1, TOP_K):
        routed = routed + w[:, kk:kk + 1] * buf[kk]
    out = _layer_norm(y + routed, g_ref[...], b_ref[...])
    o_ref[...] = out
    obf_ref[...] = out.astype(BF16)


def _moe_finish(pos_tiles, ys, w_tok, x_pk, x, w_gate, w_up, w_down, g, b, tm):
    m = x.shape[0]
    row = lambda n: pl.BlockSpec((tm, n), lambda i: (i, 0))
    vec = pl.BlockSpec((1, D_MODEL), lambda i: (0, 0))
    return pl.pallas_call(
        functools.partial(_moe_finish_kernel, tm=tm),
        grid=(m // tm,),
        in_specs=[pl.BlockSpec(memory_space=pl.ANY), pl.BlockSpec(memory_space=pl.ANY),
                  row(TOP_K), row(PACK_W), row(D_MODEL),
                  pl.BlockSpec((D_MODEL, SHARED_FF), lambda i: (0, 0)),
                  pl.BlockSpec((D_MODEL, SHARED_FF), lambda i: (0, 0)),
                  pl.BlockSpec((SHARED_FF, D_MODEL), lambda i: (0, 0)), vec, vec],
        out_specs=[row(D_MODEL), row(D_MODEL)],
        out_shape=[jax.ShapeDtypeStruct((m, D_MODEL), F32), jax.ShapeDtypeStruct((m, D_MODEL), BF16)],
        scratch_shapes=[pltpu.SMEM((TOP_K, tm), I32), pltpu.VMEM((TOP_K, tm, D_MODEL), F32),
                        pltpu.SemaphoreType.DMA, pltpu.SemaphoreType.DMA],
        compiler_params=_cparams(1),
        name="moe_finish",
    )(pos_tiles, ys, w_tok, x_pk, x, w_gate, w_up, w_down, g, b)


def _pad_cols(w, n):
    return jnp.pad(w, ((0, 0), (0, n - w.shape[1])))


def _in_proj_weight(w_in_l):
    c0 = 6 * BRANCH_W
    rw = c0
    lora = rw + 3 * RWKV_W
    parts = [w_in_l[:, :c0],
             w_in_l[:, c0 + 3 * RWKV_W + DECAY_LORA + AAA_LORA + GATE_LORA:],
             w_in_l[:, rw:lora],
             _pad_cols(w_in_l[:, lora:lora + DECAY_LORA], LORA_PAD),
             _pad_cols(w_in_l[:, lora + DECAY_LORA:lora + DECAY_LORA + AAA_LORA], LORA_PAD),
             w_in_l[:, lora + DECAY_LORA + AAA_LORA:lora + DECAY_LORA + AAA_LORA + GATE_LORA]]
    return jnp.concatenate(parts, axis=1).astype(BF16)


def _rwkv_mix_vector(mix_l):
    lora = 3 * RWKV_W
    parts = [mix_l[:lora],
             jnp.pad(mix_l[lora:lora + DECAY_LORA], (0, LORA_PAD - DECAY_LORA)),
             jnp.pad(mix_l[lora + DECAY_LORA:lora + DECAY_LORA + AAA_LORA], (0, LORA_PAD - AAA_LORA)),
             mix_l[lora + DECAY_LORA + AAA_LORA:]]
    return jnp.concatenate(parts)[None, :]


def _pad_rows(w, n):
    return jnp.pad(w, ((0, n - w.shape[0]), (0, 0))).astype(BF16)


def kernel(x, w_in, conv_w, rwkv_mix, rwkv_w0, rwkv_w2, rwkv_a0, rwkv_a2, rwkv_g2, rwkv_kk, rwkv_ka, rwkv_rk,
           rwkv_lnx_g, rwkv_lnx_b, w_branch, w_o, ln1_g, ln1_b, w_router, router_bias, w_e_gate, w_e_up,
           w_e_down, w_s_gate, w_s_up, w_s_down, ln2_g, ln2_b):
    bsz, seq, d = x.shape
    n_tok = bsz * seq
    n_rows = n_tok * TOP_K
    n_visits = n_rows // GMM_TILE + N_EXPERTS
    io_tile = 256
    xf = x.reshape(n_tok, d)
    xb = xf.astype(BF16)
    row1 = lambda v: v[None, :]
    for l in range(DEPTH):
        w_in_r = _in_proj_weight(w_in[l])
        p_conv = _matmul_cols(xb, w_in_r, COL_CONV, 3 * BRANCH_W, 1024, 1024, F32)
        p_att = _matmul_cols(xb, w_in_r, COL_ATT, 3 * BRANCH_W, 1024, 1024, BF16)
        p_gate = _matmul_cols(xb, w_in_r, COL_GATE, 3 * D_MODEL, 1024, 1024, F32)
        p_rwkv = _matmul_cols(xb, w_in_r, COL_RWKV, RWKV_COLS, 512, 1024, F32)

        y_conv = _short_conv(p_conv, conv_w[l], bsz, seq)
        y_att = _stick_breaking(p_att, bsz, seq)
        prep = _rwkv_prep(p_rwkv, _rwkv_mix_vector(rwkv_mix[l]), row1(rwkv_w0[l]),
                          _pad_rows(rwkv_w2[l], LORA_PAD), row1(rwkv_a0[l]), _pad_rows(rwkv_a2[l], LORA_PAD),
                          rwkv_g2[l].astype(BF16), row1(rwkv_kk[l]), row1(rwkv_ka[l]), bsz, seq)
        y_rwkv = _rwkv_scan(*prep, rwkv_rk[l].reshape(1, RWKV_W), row1(rwkv_lnx_g[l]), row1(rwkv_lnx_b[l]),
                            bsz, seq)

        merged = _merge(y_conv.reshape(n_tok, BRANCH_W), y_att.reshape(n_tok, BRANCH_W),
                        y_rwkv.reshape(n_tok, BRANCH_W), w_branch[l].astype(BF16), p_gate)
        x1, x1_pk = _oproj_ln(merged, w_o[l].astype(BF16), xf, row1(ln1_g[l]), row1(ln1_b[l]))

        top_i, top_w, rank, cnt = _router(x1, w_router[l].T, router_bias[l][:, None])
        counts = cnt[:, 0].astype(I32)
        offs = jnp.cumsum(counts) - counts
        pos_tiles = _positions(offs, top_i, rank, io_tile)
        xs = _dispatch(pos_tiles, x1_pk, io_tile)
        visits = _visit_list(counts, n_rows, GMM_TILE, n_visits)
        ys = _grouped_mlp(visits, xs, w_e_gate, w_e_up, w_e_down, l, n_visits)
        xf, xb = _moe_finish(pos_tiles, ys, top_w.T, x1_pk, x1, w_s_gate[l].astype(BF16),
                             w_s_up[l].astype(BF16), w_s_down[l].astype(BF16), row1(ln2_g[l]), row1(ln2_b[l]),
                             io_tile)
    return xf.reshape(bsz, seq, d)
```

```python
import functools

import jax
import jax.numpy as jnp
from jax import lax
from jax.experimental import pallas as pl
from jax.experimental.pallas import tpu as pltpu

F32 = jnp.float32
BF16 = jnp.bfloat16
I32 = jnp.int32
U32 = jnp.uint32

D_MODEL = 2048
DEPTH = 2
BRANCH_W = 1024
CONV_K = 3
ATT_HEADS = 8
ATT_HD = 128
RWKV_HEADS = 16
RWKV_HD = 64
RWKV_W = RWKV_HEADS * RWKV_HD
DECAY_LORA = 96
AAA_LORA = 96
GATE_LORA = 256
LORA_PAD = 128
N_EXPERTS = 64
TOP_K = 8
N_GROUPS = 8
GROUP_SIZE = N_EXPERTS // N_GROUPS
TOPK_GROUPS = 4
EXPERT_FF = 512
SHARED_FF = 512
ROUTED_SCALE = 2.5
LN_EPS = 1e-5
GN_EPS = 64e-5
DEEPNORM_ALPHA = (2 * DEPTH) ** 0.25

COL_CONV = 0
COL_ATT = 3 * BRANCH_W
COL_GATE = 6 * BRANCH_W
COL_RWKV = COL_GATE + 3 * D_MODEL
RWKV_COLS = 3 * RWKV_W + 2 * LORA_PAD + GATE_LORA
IN_COLS_PAD = COL_RWKV + RWKV_COLS

RWKV_CHUNK = 64
RWKV_GROUP = 4
RWKV_GW = RWKV_GROUP * RWKV_HD
RWKV_GROUPS_PER_STEP = 2
ATT_TILE = 512
ATT_KEY_TILE = 256
ATT_ROW_CHUNK = 32
GMM_TILE = 256
LANES = 128
PACK_W = D_MODEL // 2

VMEM_LIMIT = 56 * 1024 * 1024

NN = ((1,), (0,))
NT = ((1,), (1,))


def _cparams(n_axes):
    return pltpu.CompilerParams(dimension_semantics=("arbitrary",) * n_axes,
                                vmem_limit_bytes=VMEM_LIMIT)


def _dg(a, b, dims=NN):
    return lax.dot_general(a, b, (dims, ((), ())), preferred_element_type=F32)


def _split2(x):
    hi = x.astype(BF16)
    lo = (x - hi.astype(F32)).astype(BF16)
    return hi, lo


def _split3(x):
    hi = x.astype(BF16)
    r1 = x - hi.astype(F32)
    mid = r1.astype(BF16)
    lo = (r1 - mid.astype(F32)).astype(BF16)
    return hi, mid, lo


def _mm3(a, b, dims=NN):
    ah, al = _split2(a)
    bh, bl = _split2(b)
    return _dg(ah, bh, dims) + _dg(ah, bl, dims) + _dg(al, bh, dims)


def _sigmoid(x):
    return 1.0 / (1.0 + jnp.exp(-x))


def _softplus(x):
    return jnp.maximum(x, 0.0) + jnp.log(1.0 + jnp.exp(-jnp.abs(x)))


def _mm_kernel(x_ref, w_ref, o_ref):
    o_ref[...] = _dg(x_ref[...], w_ref[...]).astype(o_ref.dtype)


def _matmul_cols(x, w, col_off, ncols, tn, tm, out_dtype):
    m, k = x.shape
    off = col_off // tn
    return pl.pallas_call(
        _mm_kernel,
        grid=(ncols // tn, m // tm),
        in_specs=[pl.BlockSpec((tm, k), lambda j, i: (i, 0)),
                  pl.BlockSpec((k, tn), lambda j, i: (0, off + j))],
        out_specs=pl.BlockSpec((tm, tn), lambda j, i: (i, j)),
        out_shape=jax.ShapeDtypeStruct((m, ncols), out_dtype),
        compiler_params=_cparams(2),
        name="in_proj",
    )(x, w)


def _conv_kernel(gb_ref, gc_ref, hc_ref, gch_ref, hch_ref, cw_ref, o_ref):
    i = pl.program_id(1)
    u = gc_ref[...] * hc_ref[...]
    uh = gch_ref[...] * hch_ref[...]
    uh = jnp.where(i > 0, uh, 0.0)
    row = lax.broadcasted_iota(I32, u.shape, 0)
    u1 = jnp.where(row == 0, uh[7:8, :], pltpu.roll(u, 1, axis=0))
    u2 = pltpu.roll(u, 2, axis=0)
    u2 = jnp.where(row == 0, uh[6:7, :], jnp.where(row == 1, uh[7:8, :], u2))
    cw = cw_ref[...]
    y = gb_ref[...] * (cw[2:3, :] * u + cw[1:2, :] * u1 + cw[0:1, :] * u2)
    o_ref[...] = y.astype(o_ref.dtype)


def _short_conv(p_conv, conv_w, bsz, seq, ts=512):
    p3 = p_conv.reshape(bsz, seq, 3 * BRANCH_W)
    hb = ts // 8
    halo = lambda c: pl.BlockSpec((None, 8, BRANCH_W),
                                  lambda b, i: (b, jnp.maximum(i * hb - 1, 0), c))
    main = lambda c: pl.BlockSpec((None, ts, BRANCH_W), lambda b, i: (b, i, c))
    return pl.pallas_call(
        _conv_kernel,
        grid=(bsz, seq // ts),
        in_specs=[main(0), main(1), main(2), halo(1), halo(2),
                  pl.BlockSpec((CONV_K, BRANCH_W), lambda b, i: (0, 0))],
        out_specs=pl.BlockSpec((None, ts, BRANCH_W), lambda b, i: (b, i, 0)),
        out_shape=jax.ShapeDtypeStruct((bsz, seq, BRANCH_W), BF16),
        compiler_params=_cparams(2),
        name="short_conv",
    )(p3, p3, p3, p3, p3, conv_w)


def _attn_kernel(q_ref, k_ref, v_ref, o_ref, z_s, lb_s, hl_s, tail_s, w_s, c_s, rs_s, acc_s, *, tq, tk, scale):
    i = pl.program_id(2)
    n_sub = tq // tk
    rc = ATT_ROW_CHUNK
    suffix = (lax.broadcasted_iota(I32, (tk, tk), 0)
              > lax.broadcasted_iota(I32, (tk, tk), 1)).astype(BF16)
    suffix2 = jnp.concatenate([suffix, suffix], axis=0)
    r_io = lax.broadcasted_iota(I32, (rc, tk), 0)
    c_io = lax.broadcasted_iota(I32, (rc, tk), 1)
    acc_s[...] = jnp.zeros_like(acc_s)
    c_s[...] = jnp.zeros_like(c_s)
    n_chunks = tq // rc

    def key_rows(j):
        return pl.ds(pl.multiple_of(j * tk, tk), tk)

    def logits(s, diag_off):
        def step(r, carry):
            rows = pl.ds(pl.multiple_of(r * rc, rc), rc)
            z = z_s[s, rows, :] * scale
            log_beta = jnp.minimum(z, 0.0) - jnp.log(1.0 + jnp.exp(-jnp.abs(z)))
            log_1mb = log_beta - z
            if diag_off is not None:
                causal = (c_io + diag_off * tk) < (r_io + r * rc)
                log_1mb = jnp.where(causal, log_1mb, 0.0)
            hi, lo = _split2(log_1mb)
            hl_s[s, rows, 0:tk] = hi
            hl_s[s, rows, tk:2 * tk] = lo
            lb_s[s, rows, :] = log_beta
            rs_s[s, rows, :] = jnp.sum(log_1mb, axis=1, keepdims=True)
            return carry
        lax.fori_loop(0, n_chunks, step, 0, unroll=True)

    def weights(s, diag_off):
        def step(r, carry):
            rows = pl.ds(pl.multiple_of(r * rc, rc), rc)
            w = jnp.exp(lb_s[s, rows, :] + tail_s[s, rows, :] + c_s[rows, :])
            if diag_off is not None:
                causal = (c_io + diag_off * tk) < (r_io + r * rc)
                w = jnp.where(causal, w, 0.0)
            w_s[s, rows, :] = w.astype(BF16)
            return carry
        lax.fori_loop(0, n_chunks, step, 0, unroll=True)

    def block_pair(j_hi, diag):
        js = (j_hi, j_hi - 1)
        offs = (1, 0) if diag else (None, None)
        for s in range(2):
            z_s[s] = _dg(q_ref[...], k_ref[key_rows(js[s]), :], NT)
        for s in range(2):
            logits(s, offs[s])
        for s in range(2):
            tail_s[s] = _dg(hl_s[s], suffix2)
        for s in range(2):
            weights(s, offs[s])
            c_s[...] += rs_s[s]
        for s in range(2):
            acc_s[...] += _dg(w_s[s], v_ref[key_rows(js[s]), :])

    block_pair(n_sub * i + 1, True)

    def below(jj, carry):
        block_pair(n_sub * i - 1 - 2 * jj, False)
        return carry

    lax.fori_loop(0, i, below, 0)
    o_ref[...] = acc_s[...].astype(o_ref.dtype)


def _stick_breaking(p_att, bsz, seq):
    p3 = p_att.reshape(bsz, seq, 3 * BRANCH_W)
    tq = ATT_TILE
    tk = ATT_KEY_TILE
    assert tq == 2 * tk, "the attention kernel walks the keys in pairs of blocks"
    kern = functools.partial(_attn_kernel, tq=tq, tk=tk, scale=ATT_HD ** -0.5)
    return pl.pallas_call(
        kern,
        grid=(bsz, ATT_HEADS, seq // tq),
        in_specs=[pl.BlockSpec((None, tq, ATT_HD), lambda b, h, i: (b, i, h)),
                  pl.BlockSpec((None, seq, ATT_HD), lambda b, h, i: (b, 0, ATT_HEADS + h)),
                  pl.BlockSpec((None, seq, ATT_HD), lambda b, h, i: (b, 0, 2 * ATT_HEADS + h))],
        out_specs=pl.BlockSpec((None, tq, ATT_HD), lambda b, h, i: (b, i, h)),
        out_shape=jax.ShapeDtypeStruct((bsz, seq, BRANCH_W), BF16),
        scratch_shapes=[pltpu.VMEM((2, tq, tk), F32), pltpu.VMEM((2, tq, tk), F32),
                        pltpu.VMEM((2, tq, 2 * tk), BF16), pltpu.VMEM((2, tq, tk), F32),
                        pltpu.VMEM((2, tq, tk), BF16), pltpu.VMEM((tq, 1), F32),
                        pltpu.VMEM((2, tq, 1), F32), pltpu.VMEM((tq, ATT_HD), F32)],
        compiler_params=_cparams(3),
        name="stick_breaking_attention",
    )(p3, p3, p3)


def _head_sum(x, ones_bd):
    hi, lo = _split2(x)
    outs = []
    for c in range(x.shape[1] // RWKV_GW):
        sl = slice(RWKV_GW * c, RWKV_GW * (c + 1))
        outs.append(_dg(hi[:, sl], ones_bd) + _dg(lo[:, sl], ones_bd))
    return outs[0] if len(outs) == 1 else jnp.concatenate(outs, axis=1)


def _head_ones():
    r = lax.broadcasted_iota(I32, (RWKV_GW, RWKV_GW), 0) // RWKV_HD
    c = lax.broadcasted_iota(I32, (RWKV_GW, RWKV_GW), 1) // RWKV_HD
    return (r == c).astype(BF16)


def _rwkv_prep_kernel(p_ref, ph_ref, mix_ref, w0_ref, w2_ref, a0_ref, a2_ref, g2_ref, kk_ref, ka_ref,
                      r_ref, lw_ref, k_ref, v_ref, a_ref, b_ref, g_ref):
    i = pl.program_id(1)
    p = p_ref[...]
    prev = jnp.where(i > 0, ph_ref[...][7:8, :], 0.0)
    row = lax.broadcasted_iota(I32, p.shape, 0)
    shifted = jnp.where(row == 0, prev, pltpu.roll(p, 1, axis=0))
    ps = p + (shifted - p) * mix_ref[...]
    w = RWKV_W
    r = ps[:, 0:w]
    k = ps[:, w:2 * w]
    v = ps[:, 2 * w:3 * w]
    lw = ps[:, 3 * w:3 * w + LORA_PAD]
    la = ps[:, 3 * w + LORA_PAD:3 * w + 2 * LORA_PAD]
    lg = ps[:, 3 * w + 2 * LORA_PAD:]
    dec_in = w0_ref[...] + _dg(jnp.tanh(lw).astype(BF16), w2_ref[...])
    w_log = -_softplus(-dec_in) - 0.5
    a = _sigmoid(a0_ref[...] + _dg(la.astype(BF16), a2_ref[...]))
    g = _dg(_sigmoid(lg).astype(BF16), g2_ref[...])
    kk = k * kk_ref[...]
    norm = jnp.sqrt(_head_sum(kk * kk, _head_ones()))
    kk = kk / jnp.maximum(norm, 1e-12)
    r_ref[...] = r
    lw_ref[...] = -jnp.exp(w_log)
    k_ref[...] = k * (1.0 + (a - 1.0) * ka_ref[...])
    v_ref[...] = v
    a_ref[...] = -kk
    b_ref[...] = kk * a
    g_ref[...] = g


def _rwkv_prep(p_rwkv, mix, w0, w2, a0, a2, g2, k_k, k_a, bsz, seq, ts=256):
    p3 = p_rwkv.reshape(bsz, seq, RWKV_COLS)
    hb = ts // 8
    vec = lambda n: pl.BlockSpec((1, n), lambda b, i: (0, 0))
    mat = lambda r: pl.BlockSpec((r, RWKV_W), lambda b, i: (0, 0))
    out = pl.BlockSpec((None, ts, RWKV_W), lambda b, i: (b, i, 0))
    return pl.pallas_call(
        _rwkv_prep_kernel,
        grid=(bsz, seq // ts),
        in_specs=[pl.BlockSpec((None, ts, RWKV_COLS), lambda b, i: (b, i, 0)),
                  pl.BlockSpec((None, 8, RWKV_COLS), lambda b, i: (b, jnp.maximum(i * hb - 1, 0), 0)),
                  vec(RWKV_COLS), vec(RWKV_W), mat(LORA_PAD), vec(RWKV_W), mat(LORA_PAD),
                  mat(GATE_LORA), vec(RWKV_W), vec(RWKV_W)],
        out_specs=[out] * 7,
        out_shape=[jax.ShapeDtypeStruct((bsz, seq, RWKV_W), F32)] * 7,
        compiler_params=_cparams(2),
        name="rwkv_prep",
    )(p3, p3, mix, w0, w2, a0, a2, g2, k_k, k_a)


RWKV_LEVELS = RWKV_CHUNK.bit_length() - 1


def _rwkv_masks():
    r = jnp.arange(RWKV_GW)[:, None]
    c = jnp.arange(RWKV_GW)[None, :]
    bd = (r // RWKV_CHUNK) == (c // RWKV_HD)
    planes = [bd, bd & (r > c), bd & (r >= c)]
    for m in range(RWKV_LEVELS):
        s = 1 << m
        planes.append((r // (2 * s) == c // (2 * s)) & (r % (2 * s) >= s) & (c % (2 * s) < s))
    return jnp.stack(planes).astype(F32), bd.astype(BF16)


def _rwkv_scan_kernel(r_ref, lw_ref, k_ref, v_ref, a_ref, b_ref, g_ref, rk_ref, lng_ref, lnb_ref,
                      msk_ref, bdb_ref, o_ref, s_ref):
    @pl.when(pl.program_id(1) == 0)
    def _():
        s_ref[...] = jnp.zeros_like(s_ref)

    c_len = RWKV_CHUNK
    gw = RWKV_GW
    chains = [(bb, gi) for bb in range(r_ref.shape[0]) for gi in range(r_ref.shape[2] // gw)]

    def smap(f, *lists):
        return [f(*xs) for xs in zip(*lists)]

    def tok(ref):
        return [ref[bb, :, gi * gw:(gi + 1) * gw] for bb, gi in chains]

    def par(ref):
        return [ref[:, gi * gw:(gi + 1) * gw] for _, gi in chains]

    def tile_heads(x):
        return jnp.concatenate([x.astype(BF16)] * RWKV_GROUP, axis=0)

    def unstack(z):
        out = z[0:c_len]
        for h in range(1, RWKV_GROUP):
            out = out + z[h * c_len:(h + 1) * c_len]
        return out

    r, lw, k, v, a, b = tok(r_ref), tok(lw_ref), tok(k_ref), tok(v_ref), tok(a_ref), tok(b_ref)
    tril_c = (lax.broadcasted_iota(I32, (c_len, c_len), 0)
              >= lax.broadcasted_iota(I32, (c_len, c_len), 1)).astype(BF16)

    def cum_decay(x):
        hi, mid, lo = _split3(x)
        return _dg(tril_c, hi) + _dg(tril_c, mid) + _dg(tril_c, lo)

    cum = smap(cum_decay, lw)
    cum_end = smap(lambda c: c[c_len - 1:c_len, :], cum)
    e_inv = smap(lambda c: jnp.exp(-c), cum)
    e_rem = smap(lambda ce, c: jnp.exp(ce - c), cum_end, cum)

    bd = msk_ref[0]
    strict = msk_ref[1]
    incl = msk_ref[2]
    head_b = bdb_ref[...]
    a_s = smap(lambda x, c, l: tile_heads(x * jnp.exp(c - l)) * head_b, a, cum, lw)
    r_s = smap(lambda x, c: tile_heads(x * jnp.exp(c)) * head_b, r, cum)
    b_x = smap(lambda x, e: tile_heads(x * e), b, e_inv)
    k_x = smap(lambda x, e: tile_heads(x * e), k, e_inv)
    v_x = smap(tile_heads, v)

    a_ab = smap(lambda p, q: _dg(p, q, NT) * strict, a_s, b_x)
    a_ak = smap(lambda p, q: (_dg(p, q, NT) * strict).astype(BF16), a_s, k_x)
    a_rb = smap(lambda p, q: (_dg(p, q, NT) * incl).astype(BF16), r_s, b_x)
    a_rk = smap(lambda p, q: (_dg(p, q, NT) * incl).astype(BF16), r_s, k_x)

    inv = smap(lambda m: (incl - strict) + m * msk_ref[3], a_ab)
    for lvl in range(1, RWKV_LEVELS):
        inv_b = smap(lambda t: t.astype(BF16), inv)
        low = smap(lambda m: (m * msk_ref[3 + lvl]).astype(BF16), a_ab)
        mid = smap(lambda t, lo_: _dg(t, lo_).astype(BF16), inv_b, low)
        inv = smap(lambda t, md, tb: t + _dg(md, tb), inv, mid, inv_b)
    inv_b = smap(lambda t: t.astype(BF16), inv)

    s0 = [s_ref[bb, gi] for bb, gi in chains]
    s0_b = smap(lambda s: s.astype(BF16), s0)
    rhs = smap(lambda p, s, m, vv: (_dg(p, s, NT) + _dg(m, vv)).astype(BF16), a_s, s0_b, a_ak, v_x)
    u = smap(_dg, inv_b, rhs)
    u_b = smap(lambda t: t.astype(BF16), u)
    y = smap(lambda p, s, m1, ub, m2, vv: unstack((_dg(p, s, NT) + _dg(m1, ub) + _dg(m2, vv)) * bd),
             r_s, s0_b, a_rb, u_b, a_rk, v_x)
    uv_t = smap(lambda uu, vv: jnp.concatenate([unstack(uu * bd), vv], axis=0).T.astype(BF16), u, v)
    bk_h = smap(lambda bb_, kk_, e: jnp.concatenate([bb_ * e, kk_ * e], axis=0).astype(BF16), b, k, e_rem)
    s_new = smap(lambda s, ce, p, q: s * jnp.exp(ce) + _dg(p, q) * bd, s0, cum_end, uv_t, bk_h)
    for (bb, gi), s in zip(chains, s_new):
        s_ref[bb, gi] = s

    ones_bd = _head_ones()
    inv_n = 1.0 / RWKV_HD
    d = smap(lambda yy: yy - _head_sum(yy, ones_bd) * inv_n, y)
    var = smap(lambda dd: _head_sum(dd * dd, ones_bd) * inv_n, d)
    yn = smap(lambda dd, vr, gg, bb_: dd * lax.rsqrt(vr + GN_EPS) * gg + bb_, d, var, par(lng_ref), par(lnb_ref))
    bonus = smap(lambda rr, kk_, rk, vv: _head_sum(rr * kk_ * rk, ones_bd) * vv, r, k, par(rk_ref), v)
    for (bb, gi), yy, bo, gg in zip(chains, yn, bonus, tok(g_ref)):
        o_ref[bb, :, gi * gw:(gi + 1) * gw] = ((yy + bo) * gg).astype(o_ref.dtype)


def _rwkv_scan(r, lw, k, v, a, b, g, r_k, lnx_g, lnx_b, bsz, seq):
    n_groups = RWKV_W // RWKV_GW
    gps = RWKV_GROUPS_PER_STEP
    tok = pl.BlockSpec((bsz, RWKV_CHUNK, gps * RWKV_GW), lambda gg, i: (0, i, gg))
    vec = pl.BlockSpec((1, gps * RWKV_GW), lambda gg, i: (0, gg))
    masks, head_mask = _rwkv_masks()
    return pl.pallas_call(
        _rwkv_scan_kernel,
        grid=(n_groups // gps, seq // RWKV_CHUNK),
        in_specs=[tok] * 7 + [vec] * 3 + [
            pl.BlockSpec((3 + RWKV_LEVELS, RWKV_GW, RWKV_GW), lambda gg, i: (0, 0, 0)),
            pl.BlockSpec((RWKV_GW, RWKV_GW), lambda gg, i: (0, 0))],
        out_specs=tok,
        out_shape=jax.ShapeDtypeStruct((bsz, seq, RWKV_W), BF16),
        scratch_shapes=[pltpu.VMEM((bsz, gps, RWKV_GW, RWKV_GW), F32)],
        compiler_params=_cparams(2),
        name="rwkv_scan",
    )(r, lw, k, v, a, b, g, r_k, lnx_g, lnx_b, masks, head_mask)


def _merge_kernel(yc_ref, ya_ref, yr_ref, wb_ref, g0_ref, g1_ref, g2_ref, o_ref):
    acc = None
    for n, (y_ref, g_ref) in enumerate(((yc_ref, g0_ref), (ya_ref, g1_ref), (yr_ref, g2_ref))):
        term = _sigmoid(g_ref[...]) * _dg(y_ref[...], wb_ref[n])
        acc = term if acc is None else acc + term
    o_ref[...] = acc.astype(o_ref.dtype)


def _merge(y_conv, y_att, y_rwkv, w_branch, p_gate, tm=512, tn=1024):
    m = y_conv.shape[0]
    nj = D_MODEL // tn
    ysp = pl.BlockSpec((tm, BRANCH_W), lambda j, i: (i, 0))
    gsp = lambda n: pl.BlockSpec((tm, tn), lambda j, i: (i, n * nj + j))
    return pl.pallas_call(
        _merge_kernel,
        grid=(nj, m // tm),
        in_specs=[ysp, ysp, ysp, pl.BlockSpec((3, BRANCH_W, tn), lambda j, i: (0, 0, j)),
                  gsp(0), gsp(1), gsp(2)],
        out_specs=pl.BlockSpec((tm, tn), lambda j, i: (i, j)),
        out_shape=jax.ShapeDtypeStruct((m, D_MODEL), BF16),
        compiler_params=_cparams(2),
        name="branch_merge",
    )(y_conv, y_att, y_rwkv, w_branch, p_gate, p_gate, p_gate)


def _layer_norm(y, g, b):
    mu = jnp.mean(y, axis=-1, keepdims=True)
    d = y - mu
    var = jnp.mean(d * d, axis=-1, keepdims=True)
    return d * lax.rsqrt(var + LN_EPS) * g + b


def _pack_rows(x):
    hi = lax.bitcast_convert_type(x[:, :PACK_W].astype(BF16).astype(F32), U32)
    lo = lax.bitcast_convert_type(x[:, PACK_W:].astype(BF16).astype(F32), U32)
    return hi | (lo >> 16)


def _unpack_rows(w):
    hi = lax.bitcast_convert_type(w & jnp.uint32(0xFFFF0000), F32).astype(BF16)
    lo = lax.bitcast_convert_type(w << 16, F32).astype(BF16)
    return hi, lo


def _oproj_ln_kernel(m_ref, w_ref, x_ref, g_ref, b_ref, o_ref, opk_ref):
    y = DEEPNORM_ALPHA * x_ref[...] + _dg(m_ref[...], w_ref[...])
    out = _layer_norm(y, g_ref[...], b_ref[...])
    o_ref[...] = out
    opk_ref[...] = _pack_rows(out)


def _oproj_ln(merged, w_o, x, g, b, tm=256):
    m = x.shape[0]
    row = lambda n: pl.BlockSpec((tm, n), lambda i: (i, 0))
    vec = pl.BlockSpec((1, D_MODEL), lambda i: (0, 0))
    return pl.pallas_call(
        _oproj_ln_kernel,
        grid=(m // tm,),
        in_specs=[row(D_MODEL), pl.BlockSpec((D_MODEL, D_MODEL), lambda i: (0, 0)), row(D_MODEL), vec, vec],
        out_specs=[row(D_MODEL), row(PACK_W)],
        out_shape=[jax.ShapeDtypeStruct((m, D_MODEL), F32), jax.ShapeDtypeStruct((m, PACK_W), U32)],
        compiler_params=_cparams(1),
        name="out_proj_ln",
    )(merged, w_o, x, g, b)


def _first_max(x, idx, n):
    m = jnp.max(x, axis=0, keepdims=True)
    first = jnp.min(jnp.where(x == m, idx, float(n)), axis=0, keepdims=True)
    return m, first


def _router_kernel(x_ref, wr_ref, bias_ref, ti_ref, tw_ref, rk_ref, cnt_ref, run_ref, *, tm):
    @pl.when(pl.program_id(0) == 0)
    def _():
        run_ref[...] = jnp.zeros_like(run_ref)

    xh, xl = _split2(x_ref[...])
    wh, wl = _split2(wr_ref[...])
    logits = _dg(wh, xh, NT) + _dg(wh, xl, NT) + _dg(wl, xh, NT)
    scores = _sigmoid(logits)
    biased = scores + bias_ref[...]
    neg = -jnp.inf

    sub8 = lax.broadcasted_iota(I32, (GROUP_SIZE, tm), 0).astype(F32)
    grp_rows = []
    for gi in range(N_GROUPS):
        blk = biased[gi * GROUP_SIZE:(gi + 1) * GROUP_SIZE, :]
        m1, f1 = _first_max(blk, sub8, GROUP_SIZE)
        m2 = jnp.max(jnp.where(sub8 == f1, neg, blk), axis=0, keepdims=True)
        grp_rows.append(m1 + m2)
    cur = jnp.concatenate(grp_rows, axis=0)
    grp_io = lax.broadcasted_iota(I32, (N_GROUPS, tm), 0).astype(F32)
    keep = jnp.zeros((N_GROUPS, tm), F32)
    for _ in range(TOPK_GROUPS):
        _, f = _first_max(cur, grp_io, N_GROUPS)
        hit = grp_io == f
        keep = jnp.where(hit, 1.0, keep)
        cur = jnp.where(hit, neg, cur)
    exp_keep = jnp.concatenate(
        [jnp.broadcast_to(keep[gi:gi + 1, :], (GROUP_SIZE, tm)) for gi in range(N_GROUPS)], axis=0)
    cand = jnp.where(exp_keep > 0.5, biased, neg)

    exp_io = lax.broadcasted_iota(I32, (N_EXPERTS, tm), 0).astype(F32)
    sel = jnp.zeros((N_EXPERTS, tm), F32)
    idxs, vals = [], []
    for _ in range(TOP_K):
        _, f = _first_max(cand, exp_io, N_EXPERTS)
        hit = exp_io == f
        idxs.append(f)
        vals.append(jnp.sum(jnp.where(hit, scores, 0.0), axis=0, keepdims=True))
        sel = jnp.where(hit, 1.0, sel)
        cand = jnp.where(hit, neg, cand)
    top_s = jnp.concatenate(vals, axis=0)
    ti_ref[...] = jnp.concatenate(idxs, axis=0).astype(I32)
    tw_ref[...] = top_s / jnp.sum(top_s, axis=0, keepdims=True) * ROUTED_SCALE

    before = (lax.broadcasted_iota(I32, (tm, tm), 0) < lax.broadcasted_iota(I32, (tm, tm), 1)).astype(BF16)
    rank_all = _dg(sel.astype(BF16), before) + run_ref[...][:, 0:1]
    ranks = [jnp.sum(jnp.where(exp_io == f, rank_all, 0.0), axis=0, keepdims=True) for f in idxs]
    rk_ref[...] = jnp.concatenate(ranks, axis=0).astype(I32)
    run_ref[...] = run_ref[...] + jnp.sum(sel, axis=1, keepdims=True)
    cnt_ref[...] = run_ref[...]


def _router(x, wr_t, bias, tm=512):
    m = x.shape[0]
    slot = pl.BlockSpec((TOP_K, tm), lambda i: (0, i))
    return pl.pallas_call(
        functools.partial(_router_kernel, tm=tm),
        grid=(m // tm,),
        in_specs=[pl.BlockSpec((tm, D_MODEL), lambda i: (i, 0)),
                  pl.BlockSpec((N_EXPERTS, D_MODEL), lambda i: (0, 0)),
                  pl.BlockSpec((N_EXPERTS, 1), lambda i: (0, 0))],
        out_specs=[slot, slot, slot, pl.BlockSpec((N_EXPERTS, LANES), lambda i: (0, 0))],
        out_shape=[jax.ShapeDtypeStruct((TOP_K, m), I32), jax.ShapeDtypeStruct((TOP_K, m), F32),
                   jax.ShapeDtypeStruct((TOP_K, m), I32), jax.ShapeDtypeStruct((N_EXPERTS, LANES), F32)],
        scratch_shapes=[pltpu.VMEM((N_EXPERTS, LANES), F32)],
        compiler_params=_cparams(1),
        name="moe_router",
    )(x, wr_t, bias)


def _positions_kernel(offs_ref, ti_ref, rk_ref, o_ref):
    ti = ti_ref[...]
    acc = rk_ref[...]
    for e in range(N_EXPERTS):
        acc = acc + jnp.where(ti == e, offs_ref[e], 0)
    o_ref[...] = acc


def _positions(offs, top_i, rank, tm):
    m = top_i.shape[1]
    slot = pl.BlockSpec((TOP_K, tm), lambda i, offs_ref: (0, i))
    return pl.pallas_call(
        _positions_kernel,
        grid_spec=pltpu.PrefetchScalarGridSpec(
            num_scalar_prefetch=1, grid=(m // tm,), in_specs=[slot, slot],
            out_specs=pl.BlockSpec((None, TOP_K, tm), lambda i, offs_ref: (i, 0, 0))),
        out_shape=jax.ShapeDtypeStruct((m // tm, TOP_K, tm), I32),
        compiler_params=_cparams(1),
        name="moe_positions",
    )(offs, top_i, rank)


def _dispatch_kernel(pos_hbm, x_ref, xs_hbm, pos_smem, sem_idx, sem, *, tm):
    i = pl.program_id(0)
    cp = pltpu.make_async_copy(pos_hbm.at[i], pos_smem, sem_idx)
    cp.start()
    cp.wait()

    def row_copy(t, kk):
        return pltpu.make_async_copy(x_ref.at[pl.ds(t, 1), :], xs_hbm.at[pl.ds(pos_smem[kk, t], 1), :], sem)

    def issue(t, carry):
        for kk in range(TOP_K):
            row_copy(t, kk).start()
        return carry

    lax.fori_loop(0, tm, issue, 0)
    for kk in range(TOP_K):
        pltpu.make_async_copy(x_ref, xs_hbm.at[pl.ds(0, tm), :], sem).wait()


def _dispatch(pos_tiles, x_pk, tm):
    n_tiles = pos_tiles.shape[0]
    m = x_pk.shape[0]
    return pl.pallas_call(
        functools.partial(_dispatch_kernel, tm=tm),
        grid=(n_tiles,),
        in_specs=[pl.BlockSpec(memory_space=pl.ANY),
                  pl.BlockSpec((tm, PACK_W), lambda i: (i, 0))],
        out_specs=pl.BlockSpec(memory_space=pl.ANY),
        out_shape=jax.ShapeDtypeStruct((m * TOP_K, PACK_W), U32),
        scratch_shapes=[pltpu.SMEM((TOP_K, tm), I32), pltpu.SemaphoreType.DMA, pltpu.SemaphoreType.DMA],
        compiler_params=_cparams(1),
        name="moe_dispatch",
    )(pos_tiles, x_pk)


def _gmm_kernel(vt_ref, ve_ref, vlo_ref, vhi_ref, vfirst_ref, vnew_ref, x_ref, wg_ref, wu_ref, wd_ref, o_ref,
                wg_b, wu_b, wd_b):
    vi = pl.program_id(0)
    lo = vlo_ref[vi]
    hi = vhi_ref[vi]
    first = vfirst_ref[vi]

    @pl.when(vnew_ref[vi] == 1)
    def _():
        wg_b[...] = wg_ref[...].astype(BF16)
        wu_b[...] = wu_ref[...].astype(BF16)
        wd_b[...] = wd_ref[...].astype(BF16)

    @pl.when(hi > lo)
    def _():
        n_half = 2
        hr = x_ref.shape[0] // n_half
        halves = [slice(s * hr, (s + 1) * hr) for s in range(n_half)]
        xs = [_unpack_rows(x_ref[sl, :]) for sl in halves]
        gate = [_dg(xa, wg_b[0:PACK_W, :]) + _dg(xb, wg_b[PACK_W:, :]) for xa, xb in xs]
        up = [_dg(xa, wu_b[0:PACK_W, :]) + _dg(xb, wu_b[PACK_W:, :]) for xa, xb in xs]
        row = lax.broadcasted_iota(I32, (hr, 1), 0)
        h = [jnp.where((row + s * hr >= lo) & (row + s * hr < hi), g * _sigmoid(g) * u, 0.0).astype(BF16)
             for s, (g, u) in enumerate(zip(gate, up))]
        y = [_dg(hh, wd_b[...]) for hh in h]

        @pl.when(first == 1)
        def _():
            for sl, yy in zip(halves, y):
                o_ref[sl, :] = yy

        @pl.when(first != 1)
        def _():
            for sl, yy in zip(halves, y):
                o_ref[sl, :] += yy


def _grouped_mlp(visits, xs, w_gate, w_up, w_down, layer, n_visits):
    rows = xs.shape[0]
    tm = GMM_TILE
    by_tile = lambda vi, vt, ve, vlo, vhi, vf, vn: (vt[vi], 0)
    by_expert = lambda vi, vt, ve, vlo, vhi, vf, vn: (layer, ve[vi], 0, 0)
    grid_spec = pltpu.PrefetchScalarGridSpec(
        num_scalar_prefetch=6,
        grid=(n_visits,),
        in_specs=[pl.BlockSpec((tm, PACK_W), by_tile),
                  pl.BlockSpec((None, None, D_MODEL, EXPERT_FF), by_expert),
                  pl.BlockSpec((None, None, D_MODEL, EXPERT_FF), by_expert),
                  pl.BlockSpec((None, None, EXPERT_FF, D_MODEL), by_expert)],
        out_specs=pl.BlockSpec((tm, D_MODEL), by_tile),
        scratch_shapes=[pltpu.VMEM((D_MODEL, EXPERT_FF), BF16), pltpu.VMEM((D_MODEL, EXPERT_FF), BF16),
                        pltpu.VMEM((EXPERT_FF, D_MODEL), BF16)],
    )
    return pl.pallas_call(
        _gmm_kernel,
        grid_spec=grid_spec,
        out_shape=jax.ShapeDtypeStruct((rows, D_MODEL), F32),
        compiler_params=_cparams(1),
        name="moe_grouped_mlp",
    )(*visits, xs, w_gate, w_up, w_down)


def _visit_list(counts, n_rows, tm, n_visits):
    ends = jnp.cumsum(counts)
    offs = ends - counts
    first_tile = offs // tm
    last_tile = jnp.where(counts > 0, (ends - 1) // tm, first_tile - 1)
    n_t = last_tile - first_tile + 1
    v_end = jnp.cumsum(n_t)
    v_start = v_end - n_t
    total = v_end[-1]
    vi = jnp.arange(n_visits, dtype=I32)
    e = jnp.minimum(jnp.sum(v_end[None, :] <= vi[:, None], axis=1), N_EXPERTS - 1).astype(I32)
    active = vi < total
    last_e = jnp.max(jnp.where(counts > 0, jnp.arange(N_EXPERTS, dtype=I32), 0))
    e = jnp.where(active, e, last_e).astype(I32)
    is_e = e[:, None] == jnp.arange(N_EXPERTS, dtype=I32)[None, :]
    pick = lambda table: jnp.sum(jnp.where(is_e, table[None, :], 0), axis=1)
    tile = pick(first_tile) + (vi - pick(v_start))
    tile = jnp.where(active, tile, n_rows // tm - 1).astype(I32)
    lo = jnp.where(active, jnp.maximum(pick(offs), tile * tm) - tile * tm, 0).astype(I32)
    hi = jnp.where(active, jnp.minimum(pick(ends), (tile + 1) * tm) - tile * tm, 0).astype(I32)
    prev_tile = jnp.concatenate([jnp.full((1,), -1, I32), tile[:-1]])
    first = (active & (tile != prev_tile)).astype(I32)
    prev_e = jnp.concatenate([jnp.full((1,), -1, I32), e[:-1]])
    new_e = (e != prev_e).astype(I32)
    return tile, e, lo, hi, first, new_e


def _moe_finish_kernel(pos_hbm, ys_hbm, w_ref, xpk_ref, x_ref, wg_ref, wu_ref, wd_ref, g_ref, b_ref,
                       o_ref, obf_ref, pos_smem, buf, sem_idx, sem, *, tm):
    i = pl.program_id(0)
    cp_pos = pltpu.make_async_copy(pos_hbm.at[i], pos_smem, sem_idx)
    cp_pos.start()
    cp_pos.wait()

    def row_copy(t, kk):
        return pltpu.make_async_copy(ys_hbm.at[pl.ds(pos_smem[kk, t], 1), :], buf.at[kk, pl.ds(t, 1), :], sem)

    def issue(t, carry):
        for kk in range(TOP_K):
            row_copy(t, kk).start()
        return carry

    lax.fori_loop(0, tm, issue, 0)

    xa, xb = _unpack_rows(xpk_ref[...])
    gate = _dg(xa, wg_ref[0:PACK_W, :]) + _dg(xb, wg_ref[PACK_W:, :])
    up = _dg(xa, wu_ref[0:PACK_W, :]) + _dg(xb, wu_ref[PACK_W:, :])
    h = gate * _sigmoid(gate) * up
    y = DEEPNORM_ALPHA * x_ref[...] + _dg(h.astype(BF16), wd_ref[...])

    pltpu.make_async_copy(buf, buf, sem).wait()
    w = w_ref[...]
    routed = w[:, 0:1] * buf[0]
    for kk in range(1, TOP_K):
        routed = routed + w[:, kk:kk + 1] * buf[kk]
    out = _layer_norm(y + routed, g_ref[...], b_ref[...])
    o_ref[...] = out
    obf_ref[...] = out.astype(BF16)


def _moe_finish(pos_tiles, ys, w_tok, x_pk, x, w_gate, w_up, w_down, g, b, tm):
    m = x.shape[0]
    row = lambda n: pl.BlockSpec((tm, n), lambda i: (i, 0))
    vec = pl.BlockSpec((1, D_MODEL), lambda i: (0, 0))
    return pl.pallas_call(
        functools.partial(_moe_finish_kernel, tm=tm),
        grid=(m // tm,),
        in_specs=[pl.BlockSpec(memory_space=pl.ANY), pl.BlockSpec(memory_space=pl.ANY),
                  row(TOP_K), row(PACK_W), row(D_MODEL),
                  pl.BlockSpec((D_MODEL, SHARED_FF), lambda i: (0, 0)),
                  pl.BlockSpec((D_MODEL, SHARED_FF), lambda i: (0, 0)),
                  pl.BlockSpec((SHARED_FF, D_MODEL), lambda i: (0, 0)), vec, vec],
        out_specs=[row(D_MODEL), row(D_MODEL)],
        out_shape=[jax.ShapeDtypeStruct((m, D_MODEL), F32), jax.ShapeDtypeStruct((m, D_MODEL), BF16)],
        scratch_shapes=[pltpu.SMEM((TOP_K, tm), I32), pltpu.VMEM((TOP_K, tm, D_MODEL), F32),
                        pltpu.SemaphoreType.DMA, pltpu.SemaphoreType.DMA],
        compiler_params=_cparams(1),
        name="moe_finish",
    )(pos_tiles, ys, w_tok, x_pk, x, w_gate, w_up, w_down, g, b)


def _pad_cols(w, n):
    return jnp.pad(w, ((0, 0), (0, n - w.shape[1])))


def _in_proj_weight(w_in_l):
    c0 = 6 * BRANCH_W
    rw = c0
    lora = rw + 3 * RWKV_W
    parts = [w_in_l[:, :c0],
             w_in_l[:, c0 + 3 * RWKV_W + DECAY_LORA + AAA_LORA + GATE_LORA:],
             w_in_l[:, rw:lora],
             _pad_cols(w_in_l[:, lora:lora + DECAY_LORA], LORA_PAD),
             _pad_cols(w_in_l[:, lora + DECAY_LORA:lora + DECAY_LORA + AAA_LORA], LORA_PAD),
             w_in_l[:, lora + DECAY_LORA + AAA_LORA:lora + DECAY_LORA + AAA_LORA + GATE_LORA]]
    return jnp.concatenate(parts, axis=1).astype(BF16)


def _rwkv_mix_vector(mix_l):
    lora = 3 * RWKV_W
    parts = [mix_l[:lora],
             jnp.pad(mix_l[lora:lora + DECAY_LORA], (0, LORA_PAD - DECAY_LORA)),
             jnp.pad(mix_l[lora + DECAY_LORA:lora + DECAY_LORA + AAA_LORA], (0, LORA_PAD - AAA_LORA)),
             mix_l[lora + DECAY_LORA + AAA_LORA:]]
    return jnp.concatenate(parts)[None, :]


def _pad_rows(w, n):
    return jnp.pad(w, ((0, n - w.shape[0]), (0, 0))).astype(BF16)


def kernel(x, w_in, conv_w, rwkv_mix, rwkv_w0, rwkv_w2, rwkv_a0, rwkv_a2, rwkv_g2, rwkv_kk, rwkv_ka, rwkv_rk,
           rwkv_lnx_g, rwkv_lnx_b, w_branch, w_o, ln1_g, ln1_b, w_router, router_bias, w_e_gate, w_e_up,
           w_e_down, w_s_gate, w_s_up, w_s_down, ln2_g, ln2_b):
    bsz, seq, d = x.shape
    n_tok = bsz * seq
    n_rows = n_tok * TOP_K
    n_visits = n_rows // GMM_TILE + N_EXPERTS
    io_tile = 256
    xf = x.reshape(n_tok, d)
    xb = xf.astype(BF16)
    row1 = lambda v: v[None, :]
    for l in range(DEPTH):
        w_in_r = _in_proj_weight(w_in[l])
        p_conv = _matmul_cols(xb, w_in_r, COL_CONV, 3 * BRANCH_W, 1024, 1024, F32)
        p_att = _matmul_cols(xb, w_in_r, COL_ATT, 3 * BRANCH_W, 1024, 1024, BF16)
        p_gate = _matmul_cols(xb, w_in_r, COL_GATE, 3 * D_MODEL, 1024, 1024, F32)
        p_rwkv = _matmul_cols(xb, w_in_r, COL_RWKV, RWKV_COLS, 512, 1024, F32)

        y_conv = _short_conv(p_conv, conv_w[l], bsz, seq)
        y_att = _stick_breaking(p_att, bsz, seq)
        prep = _rwkv_prep(p_rwkv, _rwkv_mix_vector(rwkv_mix[l]), row1(rwkv_w0[l]),
                          _pad_rows(rwkv_w2[l], LORA_PAD), row1(rwkv_a0[l]), _pad_rows(rwkv_a2[l], LORA_PAD),
                          rwkv_g2[l].astype(BF16), row1(rwkv_kk[l]), row1(rwkv_ka[l]), bsz, seq)
        y_rwkv = _rwkv_scan(*prep, rwkv_rk[l].reshape(1, RWKV_W), row1(rwkv_lnx_g[l]), row1(rwkv_lnx_b[l]),
                            bsz, seq)

        merged = _merge(y_conv.reshape(n_tok, BRANCH_W), y_att.reshape(n_tok, BRANCH_W),
                        y_rwkv.reshape(n_tok, BRANCH_W), w_branch[l].astype(BF16), p_gate)
        x1, x1_pk = _oproj_ln(merged, w_o[l].astype(BF16), xf, row1(ln1_g[l]), row1(ln1_b[l]))

        top_i, top_w, rank, cnt = _router(x1, w_router[l].T, router_bias[l][:, None])
        counts = cnt[:, 0].astype(I32)
        offs = jnp.cumsum(counts) - counts
        pos_tiles = _positions(offs, top_i, rank, io_tile)
        xs = _dispatch(pos_tiles, x1_pk, io_tile)
        visits = _visit_list(counts, n_rows, GMM_TILE, n_visits)
        ys = _grouped_mlp(visits, xs, w_e_gate, w_e_up, w_e_down, l, n_visits)
        xf, xb = _moe_finish(pos_tiles, ys, top_w.T, x1_pk, x1, w_s_gate[l].astype(BF16),
                             w_s_up[l].astype(BF16), w_s_down[l].astype(BF16), row1(ln2_g[l]), row1(ln2_b[l]),
                             io_tile)
    return xf.reshape(bsz, seq, d)
```

```python
import functools

import jax
import jax.numpy as jnp
from jax import lax
from jax.experimental import pallas as pl
from jax.experimental.pallas import tpu as pltpu

F32 = jnp.float32
BF16 = jnp.bfloat16
I32 = jnp.int32
U32 = jnp.uint32

D_MODEL = 2048
DEPTH = 2
BRANCH_W = 1024
CONV_K = 3
ATT_HEADS = 8
ATT_HD = 128
RWKV_HEADS = 16
RWKV_HD = 64
RWKV_W = RWKV_HEADS * RWKV_HD
DECAY_LORA = 96
AAA_LORA = 96
GATE_LORA = 256
LORA_PAD = 128
N_EXPERTS = 64
TOP_K = 8
N_GROUPS = 8
GROUP_SIZE = N_EXPERTS // N_GROUPS
TOPK_GROUPS = 4
EXPERT_FF = 512
SHARED_FF = 512
ROUTED_SCALE = 2.5
LN_EPS = 1e-5
GN_EPS = 64e-5
DEEPNORM_ALPHA = (2 * DEPTH) ** 0.25

COL_CONV = 0
COL_ATT = 3 * BRANCH_W
COL_GATE = 6 * BRANCH_W
COL_RWKV = COL_GATE + 3 * D_MODEL
RWKV_COLS = 3 * RWKV_W + 2 * LORA_PAD + GATE_LORA
IN_COLS_PAD = COL_RWKV + RWKV_COLS

RWKV_CHUNK = 64
RWKV_GROUP = 4
RWKV_GW = RWKV_GROUP * RWKV_HD
RWKV_GROUPS_PER_STEP = 2
ATT_TILE = 512
ATT_KEY_TILE = 256
ATT_ROW_CHUNK = 32
ATT_UNDERFLOW_LOG = -105.0
GMM_TILE = 256
LANES = 128
PACK_W = D_MODEL // 2

VMEM_LIMIT = 56 * 1024 * 1024

NN = ((1,), (0,))
NT = ((1,), (1,))


def _cparams(n_axes):
    return pltpu.CompilerParams(dimension_semantics=("arbitrary",) * n_axes,
                                vmem_limit_bytes=VMEM_LIMIT)


def _dg(a, b, dims=NN):
    return lax.dot_general(a, b, (dims, ((), ())), preferred_element_type=F32)


def _split2(x):
    hi = x.astype(BF16)
    lo = (x - hi.astype(F32)).astype(BF16)
    return hi, lo


def _split3(x):
    hi = x.astype(BF16)
    r1 = x - hi.astype(F32)
    mid = r1.astype(BF16)
    lo = (r1 - mid.astype(F32)).astype(BF16)
    return hi, mid, lo


def _mm3(a, b, dims=NN):
    ah, al = _split2(a)
    bh, bl = _split2(b)
    return _dg(ah, bh, dims) + _dg(ah, bl, dims) + _dg(al, bh, dims)


def _sigmoid(x):
    return 1.0 / (1.0 + jnp.exp(-x))


def _softplus(x):
    return jnp.maximum(x, 0.0) + jnp.log(1.0 + jnp.exp(-jnp.abs(x)))


def _mm_kernel(x_ref, w_ref, o_ref):
    o_ref[...] = _dg(x_ref[...], w_ref[...]).astype(o_ref.dtype)


def _matmul_cols(x, w, col_off, ncols, tn, tm, out_dtype):
    m, k = x.shape
    off = col_off // tn
    return pl.pallas_call(
        _mm_kernel,
        grid=(ncols // tn, m // tm),
        in_specs=[pl.BlockSpec((tm, k), lambda j, i: (i, 0)),
                  pl.BlockSpec((k, tn), lambda j, i: (0, off + j))],
        out_specs=pl.BlockSpec((tm, tn), lambda j, i: (i, j)),
        out_shape=jax.ShapeDtypeStruct((m, ncols), out_dtype),
        compiler_params=_cparams(2),
        name="in_proj",
    )(x, w)


def _conv_kernel(gb_ref, gc_ref, hc_ref, gch_ref, hch_ref, cw_ref, o_ref):
    i = pl.program_id(1)
    u = gc_ref[...] * hc_ref[...]
    uh = gch_ref[...] * hch_ref[...]
    uh = jnp.where(i > 0, uh, 0.0)
    row = lax.broadcasted_iota(I32, u.shape, 0)
    u1 = jnp.where(row == 0, uh[7:8, :], pltpu.roll(u, 1, axis=0))
    u2 = pltpu.roll(u, 2, axis=0)
    u2 = jnp.where(row == 0, uh[6:7, :], jnp.where(row == 1, uh[7:8, :], u2))
    cw = cw_ref[...]
    y = gb_ref[...] * (cw[2:3, :] * u + cw[1:2, :] * u1 + cw[0:1, :] * u2)
    o_ref[...] = y.astype(o_ref.dtype)


def _short_conv(p_conv, conv_w, bsz, seq, ts=512):
    p3 = p_conv.reshape(bsz, seq, 3 * BRANCH_W)
    hb = ts // 8
    halo = lambda c: pl.BlockSpec((None, 8, BRANCH_W),
                                  lambda b, i: (b, jnp.maximum(i * hb - 1, 0), c))
    main = lambda c: pl.BlockSpec((None, ts, BRANCH_W), lambda b, i: (b, i, c))
    return pl.pallas_call(
        _conv_kernel,
        grid=(bsz, seq // ts),
        in_specs=[main(0), main(1), main(2), halo(1), halo(2),
                  pl.BlockSpec((CONV_K, BRANCH_W), lambda b, i: (0, 0))],
        out_specs=pl.BlockSpec((None, ts, BRANCH_W), lambda b, i: (b, i, 0)),
        out_shape=jax.ShapeDtypeStruct((bsz, seq, BRANCH_W), BF16),
        compiler_params=_cparams(2),
        name="short_conv",
    )(p3, p3, p3, p3, p3, conv_w)


def _attn_kernel(q_ref, k_ref, v_ref, o_ref, z_s, lb_s, hl_s, tail_s, w_s, c_s, rs_s, acc_s, *, tq, tk, scale):
    i = pl.program_id(2)
    n_sub = tq // tk
    rc = ATT_ROW_CHUNK
    suffix = (lax.broadcasted_iota(I32, (tk, tk), 0)
              > lax.broadcasted_iota(I32, (tk, tk), 1)).astype(BF16)
    suffix2 = jnp.concatenate([suffix, suffix], axis=0)
    r_io = lax.broadcasted_iota(I32, (rc, tk), 0)
    c_io = lax.broadcasted_iota(I32, (rc, tk), 1)
    acc_s[...] = jnp.zeros_like(acc_s)
    c_s[...] = jnp.zeros_like(c_s)
    n_chunks = tq // rc

    def key_rows(j):
        return pl.ds(pl.multiple_of(j * tk, tk), tk)

    def logits(s, diag_off):
        def step(r, carry):
            rows = pl.ds(pl.multiple_of(r * rc, rc), rc)
            z = z_s[s, rows, :] * scale
            log_beta = jnp.minimum(z, 0.0) - jnp.log(1.0 + jnp.exp(-jnp.abs(z)))
            log_1mb = log_beta - z
            if diag_off is not None:
                causal = (c_io + diag_off * tk) < (r_io + r * rc)
                log_1mb = jnp.where(causal, log_1mb, 0.0)
            hi, lo = _split2(log_1mb)
            hl_s[s, rows, 0:tk] = hi
            hl_s[s, rows, tk:2 * tk] = lo
            lb_s[s, rows, :] = log_beta
            rs_s[s, rows, :] = jnp.sum(log_1mb, axis=1, keepdims=True)
            return carry
        lax.fori_loop(0, n_chunks, step, 0, unroll=True)

    def weights(s, diag_off):
        def step(r, carry):
            rows = pl.ds(pl.multiple_of(r * rc, rc), rc)
            w = jnp.exp(lb_s[s, rows, :] + tail_s[s, rows, :] + c_s[rows, :])
            if diag_off is not None:
                causal = (c_io + diag_off * tk) < (r_io + r * rc)
                w = jnp.where(causal, w, 0.0)
            w_s[s, rows, :] = w.astype(BF16)
            return carry
        lax.fori_loop(0, n_chunks, step, 0, unroll=True)

    def block_pair(j_hi, diag):
        js = (j_hi, j_hi - 1)
        offs = (1, 0) if diag else (None, None)
        for s in range(2):
            z_s[s] = _dg(q_ref[...], k_ref[key_rows(js[s]), :], NT)
        for s in range(2):
            logits(s, offs[s])
        for s in range(2):
            tail_s[s] = _dg(hl_s[s], suffix2)
        for s in range(2):
            weights(s, offs[s])
            c_s[...] += rs_s[s]
        for s in range(2):
            acc_s[...] += _dg(w_s[s], v_ref[key_rows(js[s]), :])

    block_pair(n_sub * i + 1, True)

    def more(state):
        jj, c_max = state
        return jnp.logical_and(jj < i, c_max > ATT_UNDERFLOW_LOG)

    def below(state):
        jj, _ = state
        block_pair(n_sub * i - 1 - 2 * jj, False)
        return jj + 1, jnp.max(c_s[...])

    lax.while_loop(more, below, (jnp.int32(0), jnp.max(c_s[...])))
    o_ref[...] = acc_s[...].astype(o_ref.dtype)


def _stick_breaking(p_att, bsz, seq):
    p3 = p_att.reshape(bsz, seq, 3 * BRANCH_W)
    tq = ATT_TILE
    tk = ATT_KEY_TILE
    assert tq == 2 * tk, "the attention kernel walks the keys in pairs of blocks"
    kern = functools.partial(_attn_kernel, tq=tq, tk=tk, scale=ATT_HD ** -0.5)
    return pl.pallas_call(
        kern,
        grid=(bsz, ATT_HEADS, seq // tq),
        in_specs=[pl.BlockSpec((None, tq, ATT_HD), lambda b, h, i: (b, i, h)),
                  pl.BlockSpec((None, seq, ATT_HD), lambda b, h, i: (b, 0, ATT_HEADS + h)),
                  pl.BlockSpec((None, seq, ATT_HD), lambda b, h, i: (b, 0, 2 * ATT_HEADS + h))],
        out_specs=pl.BlockSpec((None, tq, ATT_HD), lambda b, h, i: (b, i, h)),
        out_shape=jax.ShapeDtypeStruct((bsz, seq, BRANCH_W), BF16),
        scratch_shapes=[pltpu.VMEM((2, tq, tk), F32), pltpu.VMEM((2, tq, tk), F32),
                        pltpu.VMEM((2, tq, 2 * tk), BF16), pltpu.VMEM((2, tq, tk), F32),
                        pltpu.VMEM((2, tq, tk), BF16), pltpu.VMEM((tq, 1), F32),
                        pltpu.VMEM((2, tq, 1), F32), pltpu.VMEM((tq, ATT_HD), F32)],
        compiler_params=_cparams(3),
        name="stick_breaking_attention",
    )(p3, p3, p3)


def _head_sum(x, ones_bd):
    hi, lo = _split2(x)
    outs = []
    for c in range(x.shape[1] // RWKV_GW):
        sl = slice(RWKV_GW * c, RWKV_GW * (c + 1))
        outs.append(_dg(hi[:, sl], ones_bd) + _dg(lo[:, sl], ones_bd))
    return outs[0] if len(outs) == 1 else jnp.concatenate(outs, axis=1)


def _head_ones():
    r = lax.broadcasted_iota(I32, (RWKV_GW, RWKV_GW), 0) // RWKV_HD
    c = lax.broadcasted_iota(I32, (RWKV_GW, RWKV_GW), 1) // RWKV_HD
    return (r == c).astype(BF16)


def _rwkv_prep_kernel(p_ref, ph_ref, mix_ref, w0_ref, w2_ref, a0_ref, a2_ref, g2_ref, kk_ref, ka_ref,
                      r_ref, lw_ref, k_ref, v_ref, a_ref, b_ref, g_ref):
    i = pl.program_id(1)
    p = p_ref[...]
    prev = jnp.where(i > 0, ph_ref[...][7:8, :], 0.0)
    row = lax.broadcasted_iota(I32, p.shape, 0)
    shifted = jnp.where(row == 0, prev, pltpu.roll(p, 1, axis=0))
    ps = p + (shifted - p) * mix_ref[...]
    w = RWKV_W
    r = ps[:, 0:w]
    k = ps[:, w:2 * w]
    v = ps[:, 2 * w:3 * w]
    lw = ps[:, 3 * w:3 * w + LORA_PAD]
    la = ps[:, 3 * w + LORA_PAD:3 * w + 2 * LORA_PAD]
    lg = ps[:, 3 * w + 2 * LORA_PAD:]
    dec_in = w0_ref[...] + _dg(jnp.tanh(lw).astype(BF16), w2_ref[...])
    w_log = -_softplus(-dec_in) - 0.5
    a = _sigmoid(a0_ref[...] + _dg(la.astype(BF16), a2_ref[...]))
    g = _dg(_sigmoid(lg).astype(BF16), g2_ref[...])
    kk = k * kk_ref[...]
    norm = jnp.sqrt(_head_sum(kk * kk, _head_ones()))
    kk = kk / jnp.maximum(norm, 1e-12)
    r_ref[...] = r
    lw_ref[...] = -jnp.exp(w_log)
    k_ref[...] = k * (1.0 + (a - 1.0) * ka_ref[...])
    v_ref[...] = v
    a_ref[...] = -kk
    b_ref[...] = kk * a
    g_ref[...] = g


def _rwkv_prep(p_rwkv, mix, w0, w2, a0, a2, g2, k_k, k_a, bsz, seq, ts=256):
    p3 = p_rwkv.reshape(bsz, seq, RWKV_COLS)
    hb = ts // 8
    vec = lambda n: pl.BlockSpec((1, n), lambda b, i: (0, 0))
    mat = lambda r: pl.BlockSpec((r, RWKV_W), lambda b, i: (0, 0))
    out = pl.BlockSpec((None, ts, RWKV_W), lambda b, i: (b, i, 0))
    return pl.pallas_call(
        _rwkv_prep_kernel,
        grid=(bsz, seq // ts),
        in_specs=[pl.BlockSpec((None, ts, RWKV_COLS), lambda b, i: (b, i, 0)),
                  pl.BlockSpec((None, 8, RWKV_COLS), lambda b, i: (b, jnp.maximum(i * hb - 1, 0), 0)),
                  vec(RWKV_COLS), vec(RWKV_W), mat(LORA_PAD), vec(RWKV_W), mat(LORA_PAD),
                  mat(GATE_LORA), vec(RWKV_W), vec(RWKV_W)],
        out_specs=[out] * 7,
        out_shape=[jax.ShapeDtypeStruct((bsz, seq, RWKV_W), F32)] * 7,
        compiler_params=_cparams(2),
        name="rwkv_prep",
    )(p3, p3, mix, w0, w2, a0, a2, g2, k_k, k_a)


RWKV_LEVELS = RWKV_CHUNK.bit_length() - 1


def _rwkv_masks():
    r = jnp.arange(RWKV_GW)[:, None]
    c = jnp.arange(RWKV_GW)[None, :]
    bd = (r // RWKV_CHUNK) == (c // RWKV_HD)
    planes = [bd, bd & (r > c), bd & (r >= c)]
    for m in range(RWKV_LEVELS):
        s = 1 << m
        planes.append((r // (2 * s) == c // (2 * s)) & (r % (2 * s) >= s) & (c % (2 * s) < s))
    return jnp.stack(planes).astype(F32), bd.astype(BF16)


def _rwkv_scan_kernel(r_ref, lw_ref, k_ref, v_ref, a_ref, b_ref, g_ref, rk_ref, lng_ref, lnb_ref,
                      msk_ref, bdb_ref, o_ref, s_ref):
    @pl.when(pl.program_id(1) == 0)
    def _():
        s_ref[...] = jnp.zeros_like(s_ref)

    c_len = RWKV_CHUNK
    gw = RWKV_GW
    chains = [(bb, gi) for bb in range(r_ref.shape[0]) for gi in range(r_ref.shape[2] // gw)]

    def smap(f, *lists):
        return [f(*xs) for xs in zip(*lists)]

    def tok(ref):
        return [ref[bb, :, gi * gw:(gi + 1) * gw] for bb, gi in chains]

    def par(ref):
        return [ref[:, gi * gw:(gi + 1) * gw] for _, gi in chains]

    def tile_heads(x):
        return jnp.concatenate([x.astype(BF16)] * RWKV_GROUP, axis=0)

    def unstack(z):
        out = z[0:c_len]
        for h in range(1, RWKV_GROUP):
            out = out + z[h * c_len:(h + 1) * c_len]
        return out

    r, lw, k, v, a, b = tok(r_ref), tok(lw_ref), tok(k_ref), tok(v_ref), tok(a_ref), tok(b_ref)
    tril_c = (lax.broadcasted_iota(I32, (c_len, c_len), 0)
              >= lax.broadcasted_iota(I32, (c_len, c_len), 1)).astype(BF16)

    def cum_decay(x):
        hi, mid, lo = _split3(x)
        return _dg(tril_c, hi) + _dg(tril_c, mid) + _dg(tril_c, lo)

    cum = smap(cum_decay, lw)
    cum_end = smap(lambda c: c[c_len - 1:c_len, :], cum)
    e_inv = smap(lambda c: jnp.exp(-c), cum)
    e_rem = smap(lambda ce, c: jnp.exp(ce - c), cum_end, cum)

    bd = msk_ref[0]
    strict = msk_ref[1]
    incl = msk_ref[2]
    head_b = bdb_ref[...]
    a_s = smap(lambda x, c, l: tile_heads(x * jnp.exp(c - l)) * head_b, a, cum, lw)
    r_s = smap(lambda x, c: tile_heads(x * jnp.exp(c)) * head_b, r, cum)
    b_x = smap(lambda x, e: tile_heads(x * e), b, e_inv)
    k_x = smap(lambda x, e: tile_heads(x * e), k, e_inv)
    v_x = smap(tile_heads, v)

    a_ab = smap(lambda p, q: _dg(p, q, NT) * strict, a_s, b_x)
    a_ak = smap(lambda p, q: (_dg(p, q, NT) * strict).astype(BF16), a_s, k_x)
    a_rb = smap(lambda p, q: (_dg(p, q, NT) * incl).astype(BF16), r_s, b_x)
    a_rk = smap(lambda p, q: (_dg(p, q, NT) * incl).astype(BF16), r_s, k_x)

    inv = smap(lambda m: (incl - strict) + m * msk_ref[3], a_ab)
    for lvl in range(1, RWKV_LEVELS):
        inv_b = smap(lambda t: t.astype(BF16), inv)
        low = smap(lambda m: (m * msk_ref[3 + lvl]).astype(BF16), a_ab)
        mid = smap(lambda t, lo_: _dg(t, lo_).astype(BF16), inv_b, low)
        inv = smap(lambda t, md, tb: t + _dg(md, tb), inv, mid, inv_b)
    inv_b = smap(lambda t: t.astype(BF16), inv)

    s0 = [s_ref[bb, gi] for bb, gi in chains]
    s0_b = smap(lambda s: s.astype(BF16), s0)
    rhs = smap(lambda p, s, m, vv: (_dg(p, s, NT) + _dg(m, vv)).astype(BF16), a_s, s0_b, a_ak, v_x)
    u = smap(_dg, inv_b, rhs)
    u_b = smap(lambda t: t.astype(BF16), u)
    y = smap(lambda p, s, m1, ub, m2, vv: unstack((_dg(p, s, NT) + _dg(m1, ub) + _dg(m2, vv)) * bd),
             r_s, s0_b, a_rb, u_b, a_rk, v_x)
    uv_t = smap(lambda uu, vv: jnp.concatenate([unstack(uu * bd), vv], axis=0).T.astype(BF16), u, v)
    bk_h = smap(lambda bb_, kk_, e: jnp.concatenate([bb_ * e, kk_ * e], axis=0).astype(BF16), b, k, e_rem)
    s_new = smap(lambda s, ce, p, q: s * jnp.exp(ce) + _dg(p, q) * bd, s0, cum_end, uv_t, bk_h)
    for (bb, gi), s in zip(chains, s_new):
        s_ref[bb, gi] = s

    ones_bd = _head_ones()
    inv_n = 1.0 / RWKV_HD
    d = smap(lambda yy: yy - _head_sum(yy, ones_bd) * inv_n, y)
    var = smap(lambda dd: _head_sum(dd * dd, ones_bd) * inv_n, d)
    yn = smap(lambda dd, vr, gg, bb_: dd * lax.rsqrt(vr + GN_EPS) * gg + bb_, d, var, par(lng_ref), par(lnb_ref))
    bonus = smap(lambda rr, kk_, rk, vv: _head_sum(rr * kk_ * rk, ones_bd) * vv, r, k, par(rk_ref), v)
    for (bb, gi), yy, bo, gg in zip(chains, yn, bonus, tok(g_ref)):
        o_ref[bb, :, gi * gw:(gi + 1) * gw] = ((yy + bo) * gg).astype(o_ref.dtype)


def _rwkv_scan(r, lw, k, v, a, b, g, r_k, lnx_g, lnx_b, bsz, seq):
    n_groups = RWKV_W // RWKV_GW
    gps = RWKV_GROUPS_PER_STEP
    tok = pl.BlockSpec((bsz, RWKV_CHUNK, gps * RWKV_GW), lambda gg, i: (0, i, gg))
    vec = pl.BlockSpec((1, gps * RWKV_GW), lambda gg, i: (0, gg))
    masks, head_mask = _rwkv_masks()
    return pl.pallas_call(
        _rwkv_scan_kernel,
        grid=(n_groups // gps, seq // RWKV_CHUNK),
        in_specs=[tok] * 7 + [vec] * 3 + [
            pl.BlockSpec((3 + RWKV_LEVELS, RWKV_GW, RWKV_GW), lambda gg, i: (0, 0, 0)),
            pl.BlockSpec((RWKV_GW, RWKV_GW), lambda gg, i: (0, 0))],
        out_specs=tok,
        out_shape=jax.ShapeDtypeStruct((bsz, seq, RWKV_W), BF16),
        scratch_shapes=[pltpu.VMEM((bsz, gps, RWKV_GW, RWKV_GW), F32)],
        compiler_params=_cparams(2),
        name="rwkv_scan",
    )(r, lw, k, v, a, b, g, r_k, lnx_g, lnx_b, masks, head_mask)


def _merge_kernel(yc_ref, ya_ref, yr_ref, wb_ref, g0_ref, g1_ref, g2_ref, o_ref):
    acc = None
    for n, (y_ref, g_ref) in enumerate(((yc_ref, g0_ref), (ya_ref, g1_ref), (yr_ref, g2_ref))):
        term = _sigmoid(g_ref[...]) * _dg(y_ref[...], wb_ref[n])
        acc = term if acc is None else acc + term
    o_ref[...] = acc.astype(o_ref.dtype)


def _merge(y_conv, y_att, y_rwkv, w_branch, p_gate, tm=512, tn=1024):
    m = y_conv.shape[0]
    nj = D_MODEL // tn
    ysp = pl.BlockSpec((tm, BRANCH_W), lambda j, i: (i, 0))
    gsp = lambda n: pl.BlockSpec((tm, tn), lambda j, i: (i, n * nj + j))
    return pl.pallas_call(
        _merge_kernel,
        grid=(nj, m // tm),
        in_specs=[ysp, ysp, ysp, pl.BlockSpec((3, BRANCH_W, tn), lambda j, i: (0, 0, j)),
                  gsp(0), gsp(1), gsp(2)],
        out_specs=pl.BlockSpec((tm, tn), lambda j, i: (i, j)),
        out_shape=jax.ShapeDtypeStruct((m, D_MODEL), BF16),
        compiler_params=_cparams(2),
        name="branch_merge",
    )(y_conv, y_att, y_rwkv, w_branch, p_gate, p_gate, p_gate)


def _layer_norm(y, g, b):
    mu = jnp.mean(y, axis=-1, keepdims=True)
    d = y - mu
    var = jnp.mean(d * d, axis=-1, keepdims=True)
    return d * lax.rsqrt(var + LN_EPS) * g + b


def _pack_rows(x):
    hi = lax.bitcast_convert_type(x[:, :PACK_W].astype(BF16).astype(F32), U32)
    lo = lax.bitcast_convert_type(x[:, PACK_W:].astype(BF16).astype(F32), U32)
    return hi | (lo >> 16)


def _unpack_rows(w):
    hi = lax.bitcast_convert_type(w & jnp.uint32(0xFFFF0000), F32).astype(BF16)
    lo = lax.bitcast_convert_type(w << 16, F32).astype(BF16)
    return hi, lo


def _oproj_ln_kernel(m_ref, w_ref, x_ref, g_ref, b_ref, o_ref, opk_ref):
    y = DEEPNORM_ALPHA * x_ref[...] + _dg(m_ref[...], w_ref[...])
    out = _layer_norm(y, g_ref[...], b_ref[...])
    o_ref[...] = out
    opk_ref[...] = _pack_rows(out)


def _oproj_ln(merged, w_o, x, g, b, tm=256):
    m = x.shape[0]
    row = lambda n: pl.BlockSpec((tm, n), lambda i: (i, 0))
    vec = pl.BlockSpec((1, D_MODEL), lambda i: (0, 0))
    return pl.pallas_call(
        _oproj_ln_kernel,
        grid=(m // tm,),
        in_specs=[row(D_MODEL), pl.BlockSpec((D_MODEL, D_MODEL), lambda i: (0, 0)), row(D_MODEL), vec, vec],
        out_specs=[row(D_MODEL), row(PACK_W)],
        out_shape=[jax.ShapeDtypeStruct((m, D_MODEL), F32), jax.ShapeDtypeStruct((m, PACK_W), U32)],
        compiler_params=_cparams(1),
        name="out_proj_ln",
    )(merged, w_o, x, g, b)


def _first_max(x, idx, n):
    m = jnp.max(x, axis=0, keepdims=True)
    first = jnp.min(jnp.where(x == m, idx, float(n)), axis=0, keepdims=True)
    return m, first


def _router_kernel(x_ref, wr_ref, bias_ref, ti_ref, tw_ref, rk_ref, cnt_ref, run_ref, *, tm):
    @pl.when(pl.program_id(0) == 0)
    def _():
        run_ref[...] = jnp.zeros_like(run_ref)

    xh, xl = _split2(x_ref[...])
    wh, wl = _split2(wr_ref[...])
    logits = _dg(wh, xh, NT) + _dg(wh, xl, NT) + _dg(wl, xh, NT)
    scores = _sigmoid(logits)
    biased = scores + bias_ref[...]
    neg = -jnp.inf

    sub8 = lax.broadcasted_iota(I32, (GROUP_SIZE, tm), 0).astype(F32)
    grp_rows = []
    for gi in range(N_GROUPS):
        blk = biased[gi * GROUP_SIZE:(gi + 1) * GROUP_SIZE, :]
        m1, f1 = _first_max(blk, sub8, GROUP_SIZE)
        m2 = jnp.max(jnp.where(sub8 == f1, neg, blk), axis=0, keepdims=True)
        grp_rows.append(m1 + m2)
    cur = jnp.concatenate(grp_rows, axis=0)
    grp_io = lax.broadcasted_iota(I32, (N_GROUPS, tm), 0).astype(F32)
    keep = jnp.zeros((N_GROUPS, tm), F32)
    for _ in range(TOPK_GROUPS):
        _, f = _first_max(cur, grp_io, N_GROUPS)
        hit = grp_io == f
        keep = jnp.where(hit, 1.0, keep)
        cur = jnp.where(hit, neg, cur)
    exp_keep = jnp.concatenate(
        [jnp.broadcast_to(keep[gi:gi + 1, :], (GROUP_SIZE, tm)) for gi in range(N_GROUPS)], axis=0)
    cand = jnp.where(exp_keep > 0.5, biased, neg)

    exp_io = lax.broadcasted_iota(I32, (N_EXPERTS, tm), 0).astype(F32)
    sel = jnp.zeros((N_EXPERTS, tm), F32)
    idxs, vals = [], []
    for _ in range(TOP_K):
        _, f = _first_max(cand, exp_io, N_EXPERTS)
        hit = exp_io == f
        idxs.append(f)
        vals.append(jnp.sum(jnp.where(hit, scores, 0.0), axis=0, keepdims=True))
        sel = jnp.where(hit, 1.0, sel)
        cand = jnp.where(hit, neg, cand)
    top_s = jnp.concatenate(vals, axis=0)
    ti_ref[...] = jnp.concatenate(idxs, axis=0).astype(I32)
    tw_ref[...] = top_s / jnp.sum(top_s, axis=0, keepdims=True) * ROUTED_SCALE

    before = (lax.broadcasted_iota(I32, (tm, tm), 0) < lax.broadcasted_iota(I32, (tm, tm), 1)).astype(BF16)
    rank_all = _dg(sel.astype(BF16), before) + run_ref[...][:, 0:1]
    ranks = [jnp.sum(jnp.where(exp_io == f, rank_all, 0.0), axis=0, keepdims=True) for f in idxs]
    rk_ref[...] = jnp.concatenate(ranks, axis=0).astype(I32)
    run_ref[...] = run_ref[...] + jnp.sum(sel, axis=1, keepdims=True)
    cnt_ref[...] = run_ref[...]


def _router(x, wr_t, bias, tm=512):
    m = x.shape[0]
    slot = pl.BlockSpec((TOP_K, tm), lambda i: (0, i))
    return pl.pallas_call(
        functools.partial(_router_kernel, tm=tm),
        grid=(m // tm,),
        in_specs=[pl.BlockSpec((tm, D_MODEL), lambda i: (i, 0)),
                  pl.BlockSpec((N_EXPERTS, D_MODEL), lambda i: (0, 0)),
                  pl.BlockSpec((N_EXPERTS, 1), lambda i: (0, 0))],
        out_specs=[slot, slot, slot, pl.BlockSpec((N_EXPERTS, LANES), lambda i: (0, 0))],
        out_shape=[jax.ShapeDtypeStruct((TOP_K, m), I32), jax.ShapeDtypeStruct((TOP_K, m), F32),
                   jax.ShapeDtypeStruct((TOP_K, m), I32), jax.ShapeDtypeStruct((N_EXPERTS, LANES), F32)],
        scratch_shapes=[pltpu.VMEM((N_EXPERTS, LANES), F32)],
        compiler_params=_cparams(1),
        name="moe_router",
    )(x, wr_t, bias)


def _positions_kernel(offs_ref, ti_ref, rk_ref, o_ref):
    ti = ti_ref[...]
    acc = rk_ref[...]
    for e in range(N_EXPERTS):
        acc = acc + jnp.where(ti == e, offs_ref[e], 0)
    o_ref[...] = acc


def _positions(offs, top_i, rank, tm):
    m = top_i.shape[1]
    slot = pl.BlockSpec((TOP_K, tm), lambda i, offs_ref: (0, i))
    return pl.pallas_call(
        _positions_kernel,
        grid_spec=pltpu.PrefetchScalarGridSpec(
            num_scalar_prefetch=1, grid=(m // tm,), in_specs=[slot, slot],
            out_specs=pl.BlockSpec((None, TOP_K, tm), lambda i, offs_ref: (i, 0, 0))),
        out_shape=jax.ShapeDtypeStruct((m // tm, TOP_K, tm), I32),
        compiler_params=_cparams(1),
        name="moe_positions",
    )(offs, top_i, rank)


def _dispatch_kernel(pos_hbm, x_ref, xs_hbm, pos_smem, sem_idx, sem, *, tm):
    i = pl.program_id(0)
    cp = pltpu.make_async_copy(pos_hbm.at[i], pos_smem, sem_idx)
    cp.start()
    cp.wait()

    def row_copy(t, kk):
        return pltpu.make_async_copy(x_ref.at[pl.ds(t, 1), :], xs_hbm.at[pl.ds(pos_smem[kk, t], 1), :], sem)

    def issue(t, carry):
        for kk in range(TOP_K):
            row_copy(t, kk).start()
        return carry

    lax.fori_loop(0, tm, issue, 0)
    for kk in range(TOP_K):
        pltpu.make_async_copy(x_ref, xs_hbm.at[pl.ds(0, tm), :], sem).wait()


def _dispatch(pos_tiles, x_pk, tm):
    n_tiles = pos_tiles.shape[0]
    m = x_pk.shape[0]
    return pl.pallas_call(
        functools.partial(_dispatch_kernel, tm=tm),
        grid=(n_tiles,),
        in_specs=[pl.BlockSpec(memory_space=pl.ANY),
                  pl.BlockSpec((tm, PACK_W), lambda i: (i, 0))],
        out_specs=pl.BlockSpec(memory_space=pl.ANY),
        out_shape=jax.ShapeDtypeStruct((m * TOP_K, PACK_W), U32),
        scratch_shapes=[pltpu.SMEM((TOP_K, tm), I32), pltpu.SemaphoreType.DMA, pltpu.SemaphoreType.DMA],
        compiler_params=_cparams(1),
        name="moe_dispatch",
    )(pos_tiles, x_pk)


def _gmm_kernel(vt_ref, ve_ref, vlo_ref, vhi_ref, vfirst_ref, vnew_ref, x_ref, wg_ref, wu_ref, wd_ref, o_ref,
                wg_b, wu_b, wd_b):
    vi = pl.program_id(0)
    lo = vlo_ref[vi]
    hi = vhi_ref[vi]
    first = vfirst_ref[vi]

    @pl.when(vnew_ref[vi] == 1)
    def _():
        wg_b[...] = wg_ref[...].astype(BF16)
        wu_b[...] = wu_ref[...].astype(BF16)
        wd_b[...] = wd_ref[...].astype(BF16)

    @pl.when(hi > lo)
    def _():
        n_half = 2
        hr = x_ref.shape[0] // n_half
        halves = [slice(s * hr, (s + 1) * hr) for s in range(n_half)]
        xs = [_unpack_rows(x_ref[sl, :]) for sl in halves]
        gate = [_dg(xa, wg_b[0:PACK_W, :]) + _dg(xb, wg_b[PACK_W:, :]) for xa, xb in xs]
        up = [_dg(xa, wu_b[0:PACK_W, :]) + _dg(xb, wu_b[PACK_W:, :]) for xa, xb in xs]
        row = lax.broadcasted_iota(I32, (hr, 1), 0)
        h = [jnp.where((row + s * hr >= lo) & (row + s * hr < hi), g * _sigmoid(g) * u, 0.0).astype(BF16)
             for s, (g, u) in enumerate(zip(gate, up))]
        y = [_dg(hh, wd_b[...]) for hh in h]

        @pl.when(first == 1)
        def _():
            for sl, yy in zip(halves, y):
                o_ref[sl, :] = yy

        @pl.when(first != 1)
        def _():
            for sl, yy in zip(halves, y):
                o_ref[sl, :] += yy


def _grouped_mlp(visits, xs, w_gate, w_up, w_down, layer, n_visits):
    rows = xs.shape[0]
    tm = GMM_TILE
    by_tile = lambda vi, vt, ve, vlo, vhi, vf, vn: (vt[vi], 0)
    by_expert = lambda vi, vt, ve, vlo, vhi, vf, vn: (layer, ve[vi], 0, 0)
    grid_spec = pltpu.PrefetchScalarGridSpec(
        num_scalar_prefetch=6,
        grid=(n_visits,),
        in_specs=[pl.BlockSpec((tm, PACK_W), by_tile),
                  pl.BlockSpec((None, None, D_MODEL, EXPERT_FF), by_expert),
                  pl.BlockSpec((None, None, D_MODEL, EXPERT_FF), by_expert),
                  pl.BlockSpec((None, None, EXPERT_FF, D_MODEL), by_expert)],
        out_specs=pl.BlockSpec((tm, D_MODEL), by_tile),
        scratch_shapes=[pltpu.VMEM((D_MODEL, EXPERT_FF), BF16), pltpu.VMEM((D_MODEL, EXPERT_FF), BF16),
                        pltpu.VMEM((EXPERT_FF, D_MODEL), BF16)],
    )
    return pl.pallas_call(
        _gmm_kernel,
        grid_spec=grid_spec,
        out_shape=jax.ShapeDtypeStruct((rows, D_MODEL), F32),
        compiler_params=_cparams(1),
        name="moe_grouped_mlp",
    )(*visits, xs, w_gate, w_up, w_down)


def _visit_list(counts, n_rows, tm, n_visits):
    ends = jnp.cumsum(counts)
    offs = ends - counts
    first_tile = offs // tm
    last_tile = jnp.where(counts > 0, (ends - 1) // tm, first_tile - 1)
    n_t = last_tile - first_tile + 1
    v_end = jnp.cumsum(n_t)
    v_start = v_end - n_t
    total = v_end[-1]
    vi = jnp.arange(n_visits, dtype=I32)
    e = jnp.minimum(jnp.sum(v_end[None, :] <= vi[:, None], axis=1), N_EXPERTS - 1).astype(I32)
    active = vi < total
    last_e = jnp.max(jnp.where(counts > 0, jnp.arange(N_EXPERTS, dtype=I32), 0))
    e = jnp.where(active, e, last_e).astype(I32)
    is_e = e[:, None] == jnp.arange(N_EXPERTS, dtype=I32)[None, :]
    pick = lambda table: jnp.sum(jnp.where(is_e, table[None, :], 0), axis=1)
    tile = pick(first_tile) + (vi - pick(v_start))
    tile = jnp.where(active, tile, n_rows // tm - 1).astype(I32)
    lo = jnp.where(active, jnp.maximum(pick(offs), tile * tm) - tile * tm, 0).astype(I32)
    hi = jnp.where(active, jnp.minimum(pick(ends), (tile + 1) * tm) - tile * tm, 0).astype(I32)
    prev_tile = jnp.concatenate([jnp.full((1,), -1, I32), tile[:-1]])
    first = (active & (tile != prev_tile)).astype(I32)
    prev_e = jnp.concatenate([jnp.full((1,), -1, I32), e[:-1]])
    new_e = (e != prev_e).astype(I32)
    return tile, e, lo, hi, first, new_e


def _moe_finish_kernel(pos_hbm, ys_hbm, w_ref, xpk_ref, x_ref, wg_ref, wu_ref, wd_ref, g_ref, b_ref,
                       o_ref, obf_ref, pos_smem, buf, sem_idx, sem, *, tm):
    i = pl.program_id(0)
    cp_pos = pltpu.make_async_copy(pos_hbm.at[i], pos_smem, sem_idx)
    cp_pos.start()
    cp_pos.wait()

    def row_copy(t, kk):
        return pltpu.make_async_copy(ys_hbm.at[pl.ds(pos_smem[kk, t], 1), :], buf.at[kk, pl.ds(t, 1), :], sem)

    def issue(t, carry):
        for kk in range(TOP_K):
            row_copy(t, kk).start()
        return carry

    lax.fori_loop(0, tm, issue, 0)

    xa, xb = _unpack_rows(xpk_ref[...])
    gate = _dg(xa, wg_ref[0:PACK_W, :]) + _dg(xb, wg_ref[PACK_W:, :])
    up = _dg(xa, wu_ref[0:PACK_W, :]) + _dg(xb, wu_ref[PACK_W:, :])
    h = gate * _sigmoid(gate) * up
    y = DEEPNORM_ALPHA * x_ref[...] + _dg(h.astype(BF16), wd_ref[...])

    pltpu.make_async_copy(buf, buf, sem).wait()
    w = w_ref[...]
    routed = w[:, 0:1] * buf[0]
    for kk in range(1, TOP_K):
        routed = routed + w[:, kk:kk + 1] * buf[kk]
    out = _layer_norm(y + routed, g_ref[...], b_ref[...])
    o_ref[...] = out
    obf_ref[...] = out.astype(BF16)


def _moe_finish(pos_tiles, ys, w_tok, x_pk, x, w_gate, w_up, w_down, g, b, tm):
    m = x.shape[0]
    row = lambda n: pl.BlockSpec((tm, n), lambda i: (i, 0))
    vec = pl.BlockSpec((1, D_MODEL), lambda i: (0, 0))
    return pl.pallas_call(
        functools.partial(_moe_finish_kernel, tm=tm),
        grid=(m // tm,),
        in_specs=[pl.BlockSpec(memory_space=pl.ANY), pl.BlockSpec(memory_space=pl.ANY),
                  row(TOP_K), row(PACK_W), row(D_MODEL),
                  pl.BlockSpec((D_MODEL, SHARED_FF), lambda i: (0, 0)),
                  pl.BlockSpec((D_MODEL, SHARED_FF), lambda i: (0, 0)),
                  pl.BlockSpec((SHARED_FF, D_MODEL), lambda i: (0, 0)), vec, vec],
        out_specs=[row(D_MODEL), row(D_MODEL)],
        out_shape=[jax.ShapeDtypeStruct((m, D_MODEL), F32), jax.ShapeDtypeStruct((m, D_MODEL), BF16)],
        scratch_shapes=[pltpu.SMEM((TOP_K, tm), I32), pltpu.VMEM((TOP_K, tm, D_MODEL), F32),
                        pltpu.SemaphoreType.DMA, pltpu.SemaphoreType.DMA],
        compiler_params=_cparams(1),
        name="moe_finish",
    )(pos_tiles, ys, w_tok, x_pk, x, w_gate, w_up, w_down, g, b)


def _pad_cols(w, n):
    return jnp.pad(w, ((0, 0), (0, n - w.shape[1])))


def _in_proj_weight(w_in_l):
    c0 = 6 * BRANCH_W
    rw = c0
    lora = rw + 3 * RWKV_W
    parts = [w_in_l[:, :c0],
             w_in_l[:, c0 + 3 * RWKV_W + DECAY_LORA + AAA_LORA + GATE_LORA:],
             w_in_l[:, rw:lora],
             _pad_cols(w_in_l[:, lora:lora + DECAY_LORA], LORA_PAD),
             _pad_cols(w_in_l[:, lora + DECAY_LORA:lora + DECAY_LORA + AAA_LORA], LORA_PAD),
             w_in_l[:, lora + DECAY_LORA + AAA_LORA:lora + DECAY_LORA + AAA_LORA + GATE_LORA]]
    return jnp.concatenate(parts, axis=1).astype(BF16)


def _rwkv_mix_vector(mix_l):
    lora = 3 * RWKV_W
    parts = [mix_l[:lora],
             jnp.pad(mix_l[lora:lora + DECAY_LORA], (0, LORA_PAD - DECAY_LORA)),
             jnp.pad(mix_l[lora + DECAY_LORA:lora + DECAY_LORA + AAA_LORA], (0, LORA_PAD - AAA_LORA)),
             mix_l[lora + DECAY_LORA + AAA_LORA:]]
    return jnp.concatenate(parts)[None, :]


def _pad_rows(w, n):
    return jnp.pad(w, ((0, n - w.shape[0]), (0, 0))).astype(BF16)


def kernel(x, w_in, conv_w, rwkv_mix, rwkv_w0, rwkv_w2, rwkv_a0, rwkv_a2, rwkv_g2, rwkv_kk, rwkv_ka, rwkv_rk,
           rwkv_lnx_g, rwkv_lnx_b, w_branch, w_o, ln1_g, ln1_b, w_router, router_bias, w_e_gate, w_e_up,
           w_e_down, w_s_gate, w_s_up, w_s_down, ln2_g, ln2_b):
    bsz, seq, d = x.shape
    n_tok = bsz * seq
    n_rows = n_tok * TOP_K
    n_visits = n_rows // GMM_TILE + N_EXPERTS
    io_tile = 256
    xf = x.reshape(n_tok, d)
    xb = xf.astype(BF16)
    row1 = lambda v: v[None, :]
    for l in range(DEPTH):
        w_in_r = _in_proj_weight(w_in[l])
        p_conv = _matmul_cols(xb, w_in_r, COL_CONV, 3 * BRANCH_W, 1024, 1024, F32)
        p_att = _matmul_cols(xb, w_in_r, COL_ATT, 3 * BRANCH_W, 1024, 1024, BF16)
        p_gate = _matmul_cols(xb, w_in_r, COL_GATE, 3 * D_MODEL, 1024, 1024, F32)
        p_rwkv = _matmul_cols(xb, w_in_r, COL_RWKV, RWKV_COLS, 512, 1024, F32)

        y_conv = _short_conv(p_conv, conv_w[l], bsz, seq)
        y_att = _stick_breaking(p_att, bsz, seq)
        prep = _rwkv_prep(p_rwkv, _rwkv_mix_vector(rwkv_mix[l]), row1(rwkv_w0[l]),
                          _pad_rows(rwkv_w2[l], LORA_PAD), row1(rwkv_a0[l]), _pad_rows(rwkv_a2[l], LORA_PAD),
                          rwkv_g2[l].astype(BF16), row1(rwkv_kk[l]), row1(rwkv_ka[l]), bsz, seq)
        y_rwkv = _rwkv_scan(*prep, rwkv_rk[l].reshape(1, RWKV_W), row1(rwkv_lnx_g[l]), row1(rwkv_lnx_b[l]),
                            bsz, seq)

        merged = _merge(y_conv.reshape(n_tok, BRANCH_W), y_att.reshape(n_tok, BRANCH_W),
                        y_rwkv.reshape(n_tok, BRANCH_W), w_branch[l].astype(BF16), p_gate)
        x1, x1_pk = _oproj_ln(merged, w_o[l].astype(BF16), xf, row1(ln1_g[l]), row1(ln1_b[l]))

        top_i, top_w, rank, cnt = _router(x1, w_router[l].T, router_bias[l][:, None])
        counts = cnt[:, 0].astype(I32)
        offs = jnp.cumsum(counts) - counts
        pos_tiles = _positions(offs, top_i, rank, io_tile)
        xs = _dispatch(pos_tiles, x1_pk, io_tile)
        visits = _visit_list(counts, n_rows, GMM_TILE, n_visits)
        ys = _grouped_mlp(visits, xs, w_e_gate, w_e_up, w_e_down, l, n_visits)
        xf, xb = _moe_finish(pos_tiles, ys, top_w.T, x1_pk, x1, w_s_gate[l].astype(BF16),
                             w_s_up[l].astype(BF16), w_s_down[l].astype(BF16), row1(ln2_g[l]), row1(ln2_b[l]),
                             io_tile)
    return xf.reshape(bsz, seq, d)
```

```python
import functools

import jax
import jax.numpy as jnp
from jax import lax
from jax.experimental import pallas as pl
from jax.experimental.pallas import tpu as pltpu

F32 = jnp.float32
BF16 = jnp.bfloat16
I32 = jnp.int32
U32 = jnp.uint32

D_MODEL = 2048
DEPTH = 2
BRANCH_W = 1024
CONV_K = 3
ATT_HEADS = 8
ATT_HD = 128
RWKV_HEADS = 16
RWKV_HD = 64
RWKV_W = RWKV_HEADS * RWKV_HD
DECAY_LORA = 96
AAA_LORA = 96
GATE_LORA = 256
LORA_PAD = 128
N_EXPERTS = 64
TOP_K = 8
N_GROUPS = 8
GROUP_SIZE = N_EXPERTS // N_GROUPS
TOPK_GROUPS = 4
EXPERT_FF = 512
SHARED_FF = 512
ROUTED_SCALE = 2.5
LN_EPS = 1e-5
GN_EPS = 64e-5
DEEPNORM_ALPHA = (2 * DEPTH) ** 0.25

COL_CONV = 0
COL_ATT = 3 * BRANCH_W
COL_GATE = 6 * BRANCH_W
COL_RWKV = COL_GATE + 3 * D_MODEL
RWKV_COLS = 3 * RWKV_W + 2 * LORA_PAD + GATE_LORA
IN_COLS_PAD = COL_RWKV + RWKV_COLS

RWKV_CHUNK = 64
RWKV_GROUP = 4
RWKV_GW = RWKV_GROUP * RWKV_HD
RWKV_GROUPS_PER_STEP = 2
ATT_TILE = 512
ATT_KEY_TILE = 256
ATT_ROW_CHUNK = 32
ATT_UNDERFLOW_LOG = -105.0
GMM_TILE = 256
LANES = 128
ROW_GROUP = 8
PACK_W = D_MODEL // 2

VMEM_LIMIT = 56 * 1024 * 1024

NN = ((1,), (0,))
NT = ((1,), (1,))


def _cparams(n_axes):
    return pltpu.CompilerParams(dimension_semantics=("arbitrary",) * n_axes,
                                vmem_limit_bytes=VMEM_LIMIT)


def _dg(a, b, dims=NN):
    return lax.dot_general(a, b, (dims, ((), ())), preferred_element_type=F32)


def _split2(x):
    hi = x.astype(BF16)
    lo = (x - hi.astype(F32)).astype(BF16)
    return hi, lo


def _split3(x):
    hi = x.astype(BF16)
    r1 = x - hi.astype(F32)
    mid = r1.astype(BF16)
    lo = (r1 - mid.astype(F32)).astype(BF16)
    return hi, mid, lo


def _mm3(a, b, dims=NN):
    ah, al = _split2(a)
    bh, bl = _split2(b)
    return _dg(ah, bh, dims) + _dg(ah, bl, dims) + _dg(al, bh, dims)


def _sigmoid(x):
    return 1.0 / (1.0 + jnp.exp(-x))


def _softplus(x):
    return jnp.maximum(x, 0.0) + jnp.log(1.0 + jnp.exp(-jnp.abs(x)))


def _mm_kernel(x_ref, w_ref, o_ref):
    o_ref[...] = _dg(x_ref[...], w_ref[...]).astype(o_ref.dtype)


def _matmul_cols(x, w, col_off, ncols, tn, tm, out_dtype):
    m, k = x.shape
    off = col_off // tn
    return pl.pallas_call(
        _mm_kernel,
        grid=(ncols // tn, m // tm),
        in_specs=[pl.BlockSpec((tm, k), lambda j, i: (i, 0)),
                  pl.BlockSpec((k, tn), lambda j, i: (0, off + j))],
        out_specs=pl.BlockSpec((tm, tn), lambda j, i: (i, j)),
        out_shape=jax.ShapeDtypeStruct((m, ncols), out_dtype),
        compiler_params=_cparams(2),
        name="in_proj",
    )(x, w)


def _conv_kernel(gb_ref, gc_ref, hc_ref, gch_ref, hch_ref, cw_ref, o_ref):
    i = pl.program_id(1)
    u = gc_ref[...] * hc_ref[...]
    uh = gch_ref[...] * hch_ref[...]
    uh = jnp.where(i > 0, uh, 0.0)
    row = lax.broadcasted_iota(I32, u.shape, 0)
    u1 = jnp.where(row == 0, uh[7:8, :], pltpu.roll(u, 1, axis=0))
    u2 = pltpu.roll(u, 2, axis=0)
    u2 = jnp.where(row == 0, uh[6:7, :], jnp.where(row == 1, uh[7:8, :], u2))
    cw = cw_ref[...]
    y = gb_ref[...] * (cw[2:3, :] * u + cw[1:2, :] * u1 + cw[0:1, :] * u2)
    o_ref[...] = y.astype(o_ref.dtype)


def _short_conv(p_conv, conv_w, bsz, seq, ts=512):
    p3 = p_conv.reshape(bsz, seq, 3 * BRANCH_W)
    hb = ts // 8
    halo = lambda c: pl.BlockSpec((None, 8, BRANCH_W),
                                  lambda b, i: (b, jnp.maximum(i * hb - 1, 0), c))
    main = lambda c: pl.BlockSpec((None, ts, BRANCH_W), lambda b, i: (b, i, c))
    return pl.pallas_call(
        _conv_kernel,
        grid=(bsz, seq // ts),
        in_specs=[main(0), main(1), main(2), halo(1), halo(2),
                  pl.BlockSpec((CONV_K, BRANCH_W), lambda b, i: (0, 0))],
        out_specs=pl.BlockSpec((None, ts, BRANCH_W), lambda b, i: (b, i, 0)),
        out_shape=jax.ShapeDtypeStruct((bsz, seq, BRANCH_W), BF16),
        compiler_params=_cparams(2),
        name="short_conv",
    )(p3, p3, p3, p3, p3, conv_w)


def _attn_kernel(q_ref, k_ref, v_ref, o_ref, z_s, lb_s, hl_s, tail_s, w_s, c_s, rs_s, acc_s, *, tq, tk, scale):
    i = pl.program_id(2)
    n_sub = tq // tk
    rc = ATT_ROW_CHUNK
    suffix = (lax.broadcasted_iota(I32, (tk, tk), 0)
              > lax.broadcasted_iota(I32, (tk, tk), 1)).astype(BF16)
    suffix2 = jnp.concatenate([suffix, suffix], axis=0)
    r_io = lax.broadcasted_iota(I32, (rc, tk), 0)
    c_io = lax.broadcasted_iota(I32, (rc, tk), 1)
    acc_s[...] = jnp.zeros_like(acc_s)
    c_s[...] = jnp.zeros_like(c_s)
    n_chunks = tq // rc

    def key_rows(j):
        return pl.ds(pl.multiple_of(j * tk, tk), tk)

    def logits(s, diag_off):
        def step(r, carry):
            rows = pl.ds(pl.multiple_of(r * rc, rc), rc)
            z = z_s[s, rows, :] * scale
            log_beta = jnp.minimum(z, 0.0) - jnp.log(1.0 + jnp.exp(-jnp.abs(z)))
            log_1mb = log_beta - z
            if diag_off is not None:
                causal = (c_io + diag_off * tk) < (r_io + r * rc)
                log_1mb = jnp.where(causal, log_1mb, 0.0)
            hi, lo = _split2(log_1mb)
            hl_s[s, rows, 0:tk] = hi
            hl_s[s, rows, tk:2 * tk] = lo
            lb_s[s, rows, :] = log_beta
            rs_s[s, rows, :] = jnp.sum(log_1mb, axis=1, keepdims=True)
            return carry
        lax.fori_loop(0, n_chunks, step, 0, unroll=True)

    def weights(s, diag_off):
        def step(r, carry):
            rows = pl.ds(pl.multiple_of(r * rc, rc), rc)
            w = jnp.exp(lb_s[s, rows, :] + tail_s[s, rows, :] + c_s[rows, :])
            if diag_off is not None:
                causal = (c_io + diag_off * tk) < (r_io + r * rc)
                w = jnp.where(causal, w, 0.0)
            w_s[s, rows, :] = w.astype(BF16)
            return carry
        lax.fori_loop(0, n_chunks, step, 0, unroll=True)

    def block_pair(j_hi, diag):
        js = (j_hi, j_hi - 1)
        offs = (1, 0) if diag else (None, None)
        for s in range(2):
            z_s[s] = _dg(q_ref[...], k_ref[key_rows(js[s]), :], NT)
        for s in range(2):
            logits(s, offs[s])
        for s in range(2):
            tail_s[s] = _dg(hl_s[s], suffix2)
        for s in range(2):
            weights(s, offs[s])
            c_s[...] += rs_s[s]
        for s in range(2):
            acc_s[...] += _dg(w_s[s], v_ref[key_rows(js[s]), :])

    block_pair(n_sub * i + 1, True)

    def more(state):
        jj, c_max = state
        return jnp.logical_and(jj < i, c_max > ATT_UNDERFLOW_LOG)

    def below(state):
        jj, _ = state
        block_pair(n_sub * i - 1 - 2 * jj, False)
        return jj + 1, jnp.max(c_s[...])

    lax.while_loop(more, below, (jnp.int32(0), jnp.max(c_s[...])))
    o_ref[...] = acc_s[...].astype(o_ref.dtype)


def _stick_breaking(p_att, bsz, seq):
    p3 = p_att.reshape(bsz, seq, 3 * BRANCH_W)
    tq = ATT_TILE
    tk = ATT_KEY_TILE
    assert tq == 2 * tk, "the attention kernel walks the keys in pairs of blocks"
    kern = functools.partial(_attn_kernel, tq=tq, tk=tk, scale=ATT_HD ** -0.5)
    return pl.pallas_call(
        kern,
        grid=(bsz, ATT_HEADS, seq // tq),
        in_specs=[pl.BlockSpec((None, tq, ATT_HD), lambda b, h, i: (b, i, h)),
                  pl.BlockSpec((None, seq, ATT_HD), lambda b, h, i: (b, 0, ATT_HEADS + h)),
                  pl.BlockSpec((None, seq, ATT_HD), lambda b, h, i: (b, 0, 2 * ATT_HEADS + h))],
        out_specs=pl.BlockSpec((None, tq, ATT_HD), lambda b, h, i: (b, i, h)),
        out_shape=jax.ShapeDtypeStruct((bsz, seq, BRANCH_W), BF16),
        scratch_shapes=[pltpu.VMEM((2, tq, tk), F32), pltpu.VMEM((2, tq, tk), F32),
                        pltpu.VMEM((2, tq, 2 * tk), BF16), pltpu.VMEM((2, tq, tk), F32),
                        pltpu.VMEM((2, tq, tk), BF16), pltpu.VMEM((tq, 1), F32),
                        pltpu.VMEM((2, tq, 1), F32), pltpu.VMEM((tq, ATT_HD), F32)],
        compiler_params=_cparams(3),
        name="stick_breaking_attention",
    )(p3, p3, p3)


def _head_sum(x, ones_bd):
    hi, lo = _split2(x)
    outs = []
    for c in range(x.shape[1] // RWKV_GW):
        sl = slice(RWKV_GW * c, RWKV_GW * (c + 1))
        outs.append(_dg(hi[:, sl], ones_bd) + _dg(lo[:, sl], ones_bd))
    return outs[0] if len(outs) == 1 else jnp.concatenate(outs, axis=1)


def _head_ones():
    r = lax.broadcasted_iota(I32, (RWKV_GW, RWKV_GW), 0) // RWKV_HD
    c = lax.broadcasted_iota(I32, (RWKV_GW, RWKV_GW), 1) // RWKV_HD
    return (r == c).astype(BF16)


RWKV_LEVELS = RWKV_CHUNK.bit_length() - 1


def _rwkv_masks():
    r = jnp.arange(RWKV_GW)[:, None]
    c = jnp.arange(RWKV_GW)[None, :]
    bd = (r // RWKV_CHUNK) == (c // RWKV_HD)
    planes = [bd, bd & (r > c), bd & (r >= c)]
    for m in range(RWKV_LEVELS):
        s = 1 << m
        planes.append((r // (2 * s) == c // (2 * s)) & (r % (2 * s) >= s) & (c % (2 * s) < s))
    return jnp.stack(planes).astype(F32), bd.astype(BF16)


def _rwkv_scan_kernel(pr_ref, pk_ref, pv_ref, pl_ref, mr_ref, mk_ref, mv_ref, ml_ref, w0_ref, w2_ref, a0_ref,
                      a2_ref, g2_ref, kk_ref, ka_ref, rk_ref, lng_ref, lnb_ref, msk_ref, bdb_ref,
                      o_ref, s_ref, prev_ref):
    @pl.when(pl.program_id(1) == 0)
    def _():
        s_ref[...] = jnp.zeros_like(s_ref)
        prev_ref[...] = jnp.zeros_like(prev_ref)

    c_len = RWKV_CHUNK
    gw = RWKV_GW
    n_b = pr_ref.shape[0]
    chains = [(bb, gi) for bb in range(n_b) for gi in range(pr_ref.shape[2] // gw)]
    first_row = lax.broadcasted_iota(I32, (c_len, 1), 0) == 0

    def smap(f, *lists):
        return [f(*xs) for xs in zip(*lists)]

    def shifted_mix(ref, mix_ref, slot):
        outs = []
        for bb in range(n_b):
            x = ref[bb]
            shifted = jnp.where(first_row, prev_ref[bb, slot], pltpu.roll(x, 1, axis=0))
            prev_ref[bb, slot] = x[c_len - 1:c_len, :]
            outs.append(x + (shifted - x) * mix_ref[...])
        return outs

    ps_r, ps_k, ps_v = shifted_mix(pr_ref, mr_ref, 0), shifted_mix(pk_ref, mk_ref, 1), shifted_mix(pv_ref, mv_ref, 2)
    ps_l = shifted_mix(pl_ref, ml_ref, 3)
    lora_w = [jnp.tanh(x[:, 0:LORA_PAD]).astype(BF16) for x in ps_l]
    lora_a = [x[:, LORA_PAD:2 * LORA_PAD].astype(BF16) for x in ps_l]
    lora_g = [_sigmoid(x[:, 2 * LORA_PAD:]).astype(BF16) for x in ps_l]
    ones_hd = _head_ones()
    r, lw, k, v, a, b, g = [], [], [], [], [], [], []
    for bb, gi in chains:
        sl = slice(gi * gw, (gi + 1) * gw)
        w_log = -_softplus(-(w0_ref[:, sl] + _dg(lora_w[bb], w2_ref[:, sl]))) - 0.5
        rate = _sigmoid(a0_ref[:, sl] + _dg(lora_a[bb], a2_ref[:, sl]))
        kk = ps_k[bb][:, sl] * kk_ref[:, sl]
        kk = kk / jnp.maximum(jnp.sqrt(_head_sum(kk * kk, ones_hd)), 1e-12)
        r.append(ps_r[bb][:, sl])
        lw.append(-jnp.exp(w_log))
        k.append(ps_k[bb][:, sl] * (1.0 + (rate - 1.0) * ka_ref[:, sl]))
        v.append(ps_v[bb][:, sl])
        a.append(-kk)
        b.append(kk * rate)
        g.append(_dg(lora_g[bb], g2_ref[:, sl]))

    def par(ref):
        return [ref[:, gi * gw:(gi + 1) * gw] for _, gi in chains]

    def tile_heads(x):
        return jnp.concatenate([x.astype(BF16)] * RWKV_GROUP, axis=0)

    def unstack(z):
        out = z[0:c_len]
        for h in range(1, RWKV_GROUP):
            out = out + z[h * c_len:(h + 1) * c_len]
        return out

    tril_c =(lax.broadcasted_iota(I32, (c_len, c_len), 0)
              >= lax.broadcasted_iota(I32, (c_len, c_len), 1)).astype(BF16)

    def cum_decay(x):
        hi, mid, lo = _split3(x)
        return _dg(tril_c, hi) + _dg(tril_c, mid) + _dg(tril_c, lo)

    cum = smap(cum_decay, lw)
    cum_end = smap(lambda c: c[c_len - 1:c_len, :], cum)
    e_inv = smap(lambda c: jnp.exp(-c), cum)
    e_rem = smap(lambda ce, c: jnp.exp(ce - c), cum_end, cum)

    bd = msk_ref[0]
    strict = msk_ref[1]
    incl = msk_ref[2]
    head_b = bdb_ref[...]
    a_s = smap(lambda x, c, l: tile_heads(x * jnp.exp(c - l)) * head_b, a, cum, lw)
    r_s = smap(lambda x, c: tile_heads(x * jnp.exp(c)) * head_b, r, cum)
    b_x = smap(lambda x, e: tile_heads(x * e), b, e_inv)
    k_x = smap(lambda x, e: tile_heads(x * e), k, e_inv)
    v_x = smap(tile_heads, v)

    a_ab = smap(lambda p, q: _dg(p, q, NT) * strict, a_s, b_x)
    a_ak = smap(lambda p, q: (_dg(p, q, NT) * strict).astype(BF16), a_s, k_x)
    a_rb = smap(lambda p, q: (_dg(p, q, NT) * incl).astype(BF16), r_s, b_x)
    a_rk = smap(lambda p, q: (_dg(p, q, NT) * incl).astype(BF16), r_s, k_x)

    inv = smap(lambda m: (incl - strict) + m * msk_ref[3], a_ab)
    for lvl in range(1, RWKV_LEVELS):
        inv_b = smap(lambda t: t.astype(BF16), inv)
        low = smap(lambda m: (m * msk_ref[3 + lvl]).astype(BF16), a_ab)
        mid = smap(lambda t, lo_: _dg(t, lo_).astype(BF16), inv_b, low)
        inv = smap(lambda t, md, tb: t + _dg(md, tb), inv, mid, inv_b)
    inv_b = smap(lambda t: t.astype(BF16), inv)

    s0 = [s_ref[bb, gi] for bb, gi in chains]
    s0_b = smap(lambda s: s.astype(BF16), s0)
    rhs = smap(lambda p, s, m, vv: (_dg(p, s, NT) + _dg(m, vv)).astype(BF16), a_s, s0_b, a_ak, v_x)
    u = smap(_dg, inv_b, rhs)
    u_b = smap(lambda t: t.astype(BF16), u)
    y = smap(lambda p, s, m1, ub, m2, vv: unstack((_dg(p, s, NT) + _dg(m1, ub) + _dg(m2, vv)) * bd),
             r_s, s0_b, a_rb, u_b, a_rk, v_x)
    uv_t = smap(lambda uu, vv: jnp.concatenate([unstack(uu * bd), vv], axis=0).T.astype(BF16), u, v)
    bk_h = smap(lambda bb_, kk_, e: jnp.concatenate([bb_ * e, kk_ * e], axis=0).astype(BF16), b, k, e_rem)
    s_new = smap(lambda s, ce, p, q: s * jnp.exp(ce) + _dg(p, q) * bd, s0, cum_end, uv_t, bk_h)
    for (bb, gi), s in zip(chains, s_new):
        s_ref[bb, gi] = s

    inv_n = 1.0 / RWKV_HD
    d = smap(lambda yy: yy - _head_sum(yy, ones_hd) * inv_n, y)
    var = smap(lambda dd: _head_sum(dd * dd, ones_hd) * inv_n, d)
    yn = smap(lambda dd, vr, gg, bb_: dd * lax.rsqrt(vr + GN_EPS) * gg + bb_, d, var, par(lng_ref), par(lnb_ref))
    bonus = smap(lambda rr, kk_, rk, vv: _head_sum(rr * kk_ * rk, ones_hd) * vv, r, k, par(rk_ref), v)
    for (bb, gi), yy, bo, gg in zip(chains, yn, bonus, g):
        o_ref[bb, :, gi * gw:(gi + 1) * gw] = ((yy + bo) * gg).astype(o_ref.dtype)


def _rwkv_mixer(p_rwkv, mix, w0, w2, a0, a2, g2, k_k, k_a, r_k, lnx_g, lnx_b, bsz, seq):
    p3 = p_rwkv.reshape(bsz, seq, RWKV_COLS)
    n_groups = RWKV_W // RWKV_GW
    gps = RWKV_GROUPS_PER_STEP
    bw = gps * RWKV_GW
    n_cb = RWKV_W // bw
    lora_cb = 3 * n_cb
    assert RWKV_COLS == (lora_cb + 1) * bw
    tok = lambda cb0: pl.BlockSpec((bsz, RWKV_CHUNK, bw), lambda gg, i: (0, i, cb0 + gg))
    tok_lora = pl.BlockSpec((bsz, RWKV_CHUNK, bw), lambda gg, i: (0, i, lora_cb))
    mixv = lambda cb0: pl.BlockSpec((1, bw), lambda gg, i: (0, cb0 + gg))
    mix_lora = pl.BlockSpec((1, bw), lambda gg, i: (0, lora_cb))
    vec = pl.BlockSpec((1, bw), lambda gg, i: (0, gg))
    mat = lambda rows: pl.BlockSpec((rows, bw), lambda gg, i: (0, gg))
    masks, head_mask = _rwkv_masks()
    return pl.pallas_call(
        _rwkv_scan_kernel,
        grid=(n_groups // gps, seq // RWKV_CHUNK),
        in_specs=[tok(0), tok(n_cb), tok(2 * n_cb), tok_lora,
                  mixv(0), mixv(n_cb), mixv(2 * n_cb), mix_lora,
                  vec, mat(LORA_PAD), vec, mat(LORA_PAD), mat(GATE_LORA), vec, vec, vec, vec, vec,
                  pl.BlockSpec((3 + RWKV_LEVELS, RWKV_GW, RWKV_GW), lambda gg, i: (0, 0, 0)),
                  pl.BlockSpec((RWKV_GW, RWKV_GW), lambda gg, i: (0, 0))],
        out_specs=pl.BlockSpec((bsz, RWKV_CHUNK, bw), lambda gg, i: (0, i, gg)),
        out_shape=jax.ShapeDtypeStruct((bsz, seq, RWKV_W), BF16),
        scratch_shapes=[pltpu.VMEM((bsz, gps, RWKV_GW, RWKV_GW), F32),
                        pltpu.VMEM((bsz, 4, 1, bw), F32)],
        compiler_params=_cparams(2),
        name="rwkv_mixer",
    )(p3, p3, p3, p3, mix, mix, mix, mix, w0, w2, a0, a2, g2, k_k, k_a, r_k, lnx_g, lnx_b, masks, head_mask)


def _merge_kernel(yc_ref, ya_ref, yr_ref, wb_ref, g0_ref, g1_ref, g2_ref, o_ref):
    acc = None
    for n, (y_ref, g_ref) in enumerate(((yc_ref, g0_ref), (ya_ref, g1_ref), (yr_ref, g2_ref))):
        term = _sigmoid(g_ref[...]) * _dg(y_ref[...], wb_ref[n])
        acc = term if acc is None else acc + term
    o_ref[...] = acc.astype(o_ref.dtype)


def _merge(y_conv, y_att, y_rwkv, w_branch, p_gate, tm=512, tn=1024):
    m = y_conv.shape[0]
    nj = D_MODEL // tn
    ysp = pl.BlockSpec((tm, BRANCH_W), lambda j, i: (i, 0))
    gsp = lambda n: pl.BlockSpec((tm, tn), lambda j, i: (i, n * nj + j))
    return pl.pallas_call(
        _merge_kernel,
        grid=(nj, m // tm),
        in_specs=[ysp, ysp, ysp, pl.BlockSpec((3, BRANCH_W, tn), lambda j, i: (0, 0, j)),
                  gsp(0), gsp(1), gsp(2)],
        out_specs=pl.BlockSpec((tm, tn), lambda j, i: (i, j)),
        out_shape=jax.ShapeDtypeStruct((m, D_MODEL), BF16),
        compiler_params=_cparams(2),
        name="branch_merge",
    )(y_conv, y_att, y_rwkv, w_branch, p_gate, p_gate, p_gate)


def _layer_norm(y, g, b):
    mu = jnp.mean(y, axis=-1, keepdims=True)
    d = y - mu
    var = jnp.mean(d * d, axis=-1, keepdims=True)
    return d * lax.rsqrt(var + LN_EPS) * g + b


def _pack_rows(x):
    hi = lax.bitcast_convert_type(x[:, :PACK_W].astype(BF16).astype(F32), U32)
    lo = lax.bitcast_convert_type(x[:, PACK_W:].astype(BF16).astype(F32), U32)
    return hi | (lo >> 16)


def _unpack_rows(w):
    hi = lax.bitcast_convert_type(w & jnp.uint32(0xFFFF0000), F32).astype(BF16)
    lo = lax.bitcast_convert_type(w << 16, F32).astype(BF16)
    return hi, lo


def _oproj_ln_kernel(m_ref, w_ref, x_ref, g_ref, b_ref, o_ref, opk_ref):
    y = DEEPNORM_ALPHA * x_ref[...] + _dg(m_ref[...], w_ref[...])
    out = _layer_norm(y, g_ref[...], b_ref[...])
    o_ref[...] = out
    opk_ref[...] = _pack_rows(out)


def _oproj_ln(merged, w_o, x, g, b, tm=256):
    m = x.shape[0]
    row = lambda n: pl.BlockSpec((tm, n), lambda i: (i, 0))
    vec = pl.BlockSpec((1, D_MODEL), lambda i: (0, 0))
    return pl.pallas_call(
        _oproj_ln_kernel,
        grid=(m // tm,),
        in_specs=[row(D_MODEL), pl.BlockSpec((D_MODEL, D_MODEL), lambda i: (0, 0)), row(D_MODEL), vec, vec],
        out_specs=[row(D_MODEL), row(PACK_W)],
        out_shape=[jax.ShapeDtypeStruct((m, D_MODEL), F32), jax.ShapeDtypeStruct((m, PACK_W), U32)],
        compiler_params=_cparams(1),
        name="out_proj_ln",
    )(merged, w_o, x, g, b)


def _first_max(x, idx, n):
    m = jnp.max(x, axis=0, keepdims=True)
    first = jnp.min(jnp.where(x == m, idx, float(n)), axis=0, keepdims=True)
    return m, first


def _router_kernel(x_ref, wr_ref, bias_ref, ti_ref, tw_ref, rk_ref, cnt_ref, run_ref, *, tm):
    @pl.when(pl.program_id(0) == 0)
    def _():
        run_ref[...] = jnp.zeros_like(run_ref)

    xh, xl = _split2(x_ref[...])
    wh, wl = _split2(wr_ref[...])
    logits = _dg(wh, xh, NT) + _dg(wh, xl, NT) + _dg(wl, xh, NT)
    scores = _sigmoid(logits)
    biased = scores + bias_ref[...]
    neg = -jnp.inf

    sub8 = lax.broadcasted_iota(I32, (GROUP_SIZE, tm), 0).astype(F32)
    grp_rows = []
    for gi in range(N_GROUPS):
        blk = biased[gi * GROUP_SIZE:(gi + 1) * GROUP_SIZE, :]
        m1, f1 = _first_max(blk, sub8, GROUP_SIZE)
        m2 = jnp.max(jnp.where(sub8 == f1, neg, blk), axis=0, keepdims=True)
        grp_rows.append(m1 + m2)
    cur = jnp.concatenate(grp_rows, axis=0)
    grp_io = lax.broadcasted_iota(I32, (N_GROUPS, tm), 0).astype(F32)
    keep = jnp.zeros((N_GROUPS, tm), F32)
    for _ in range(TOPK_GROUPS):
        _, f = _first_max(cur, grp_io, N_GROUPS)
        hit = grp_io == f
        keep = jnp.where(hit, 1.0, keep)
        cur = jnp.where(hit, neg, cur)
    exp_keep = jnp.concatenate(
        [jnp.broadcast_to(keep[gi:gi + 1, :], (GROUP_SIZE, tm)) for gi in range(N_GROUPS)], axis=0)
    cand = jnp.where(exp_keep > 0.5, biased, neg)

    exp_io = lax.broadcasted_iota(I32, (N_EXPERTS, tm), 0).astype(F32)
    sel = jnp.zeros((N_EXPERTS, tm), F32)
    idxs, vals = [], []
    for _ in range(TOP_K):
        _, f = _first_max(cand, exp_io, N_EXPERTS)
        hit = exp_io == f
        idxs.append(f)
        vals.append(jnp.sum(jnp.where(hit, scores, 0.0), axis=0, keepdims=True))
        sel = jnp.where(hit, 1.0, sel)
        cand = jnp.where(hit, neg, cand)
    top_s = jnp.concatenate(vals, axis=0)
    ti_ref[...] = jnp.concatenate(idxs, axis=0).astype(I32)
    tw_ref[...] = top_s / jnp.sum(top_s, axis=0, keepdims=True) * ROUTED_SCALE

    before = (lax.broadcasted_iota(I32, (tm, tm), 0) < lax.broadcasted_iota(I32, (tm, tm), 1)).astype(BF16)
    rank_all = _dg(sel.astype(BF16), before) + run_ref[...][:, 0:1]
    ranks = [jnp.sum(jnp.where(exp_io == f, rank_all, 0.0), axis=0, keepdims=True) for f in idxs]
    rk_ref[...] = jnp.concatenate(ranks, axis=0).astype(I32)
    run_ref[...] = run_ref[...] + jnp.sum(sel, axis=1, keepdims=True)
    cnt_ref[...] = run_ref[...]


def _router(x, wr_t, bias, tm=512):
    m = x.shape[0]
    slot = pl.BlockSpec((TOP_K, tm), lambda i: (0, i))
    return pl.pallas_call(
        functools.partial(_router_kernel, tm=tm),
        grid=(m // tm,),
        in_specs=[pl.BlockSpec((tm, D_MODEL), lambda i: (i, 0)),
                  pl.BlockSpec((N_EXPERTS, D_MODEL), lambda i: (0, 0)),
                  pl.BlockSpec((N_EXPERTS, 1), lambda i: (0, 0))],
        out_specs=[slot, slot, slot, pl.BlockSpec((N_EXPERTS, LANES), lambda i: (0, 0))],
        out_shape=[jax.ShapeDtypeStruct((TOP_K, m), I32), jax.ShapeDtypeStruct((TOP_K, m), F32),
                   jax.ShapeDtypeStruct((TOP_K, m), I32), jax.ShapeDtypeStruct((N_EXPERTS, LANES), F32)],
        scratch_shapes=[pltpu.VMEM((N_EXPERTS, LANES), F32)],
        compiler_params=_cparams(1),
        name="moe_router",
    )(x, wr_t, bias)


def _positions_kernel(offs_ref, ti_ref, rk_ref, o_ref):
    ti = ti_ref[...]
    acc = rk_ref[...]
    for e in range(N_EXPERTS):
        acc = acc + jnp.where(ti == e, offs_ref[e], 0)
    o_ref[...] = acc


def _positions(offs, top_i, rank, tm):
    m = top_i.shape[1]
    slot = pl.BlockSpec((TOP_K, tm), lambda i, offs_ref: (0, i))
    return pl.pallas_call(
        _positions_kernel,
        grid_spec=pltpu.PrefetchScalarGridSpec(
            num_scalar_prefetch=1, grid=(m // tm,), in_specs=[slot, slot],
            out_specs=pl.BlockSpec((None, TOP_K, tm), lambda i, offs_ref: (i, 0, 0))),
        out_shape=jax.ShapeDtypeStruct((m // tm, TOP_K, tm), I32),
        compiler_params=_cparams(1),
        name="moe_positions",
    )(offs, top_i, rank)


def _dispatch_kernel(pos_hbm, x_ref, xs_hbm, pos_smem, sem_idx, sem, *, tm):
    i = pl.program_id(0)
    cp = pltpu.make_async_copy(pos_hbm.at[i], pos_smem, sem_idx)
    cp.start()
    cp.wait()

    def row_copy(t, kk):
        return pltpu.make_async_copy(x_ref.at[pl.ds(t, 1), :], xs_hbm.at[pl.ds(pos_smem[kk, t], 1), :], sem)

    def issue(g, carry):
        base = pl.multiple_of(g * ROW_GROUP, ROW_GROUP)
        for u in range(ROW_GROUP):
            for kk in range(TOP_K):
                row_copy(base + u, kk).start()
        return carry

    lax.fori_loop(0, tm // ROW_GROUP, issue, 0)
    for kk in range(TOP_K):
        pltpu.make_async_copy(x_ref, xs_hbm.at[pl.ds(0, tm), :], sem).wait()


def _dispatch(pos_tiles, x_pk, tm):
    n_tiles = pos_tiles.shape[0]
    m = x_pk.shape[0]
    return pl.pallas_call(
        functools.partial(_dispatch_kernel, tm=tm),
        grid=(n_tiles,),
        in_specs=[pl.BlockSpec(memory_space=pl.ANY),
                  pl.BlockSpec((tm, PACK_W), lambda i: (i, 0))],
        out_specs=pl.BlockSpec(memory_space=pl.ANY),
        out_shape=jax.ShapeDtypeStruct((m * TOP_K, PACK_W), U32),
        scratch_shapes=[pltpu.SMEM((TOP_K, tm), I32), pltpu.SemaphoreType.DMA, pltpu.SemaphoreType.DMA],
        compiler_params=_cparams(1),
        name="moe_dispatch",
    )(pos_tiles, x_pk)


def _gmm_kernel(vt_ref, ve_ref, vlo_ref, vhi_ref, vfirst_ref, vnew_ref, x_ref, wg_ref, wu_ref, wd_ref, o_ref,
                wg_b, wu_b, wd_b):
    vi = pl.program_id(0)
    lo = vlo_ref[vi]
    hi = vhi_ref[vi]
    first = vfirst_ref[vi]

    @pl.when(vnew_ref[vi] == 1)
    def _():
        wg_b[...] = wg_ref[...].astype(BF16)
        wu_b[...] = wu_ref[...].astype(BF16)
        wd_b[...] = wd_ref[...].astype(BF16)

    @pl.when(hi > lo)
    def _():
        n_half = 2
        hr = x_ref.shape[0] // n_half
        halves = [slice(s * hr, (s + 1) * hr) for s in range(n_half)]
        xs = [_unpack_rows(x_ref[sl, :]) for sl in halves]
        gate = [_dg(xa, wg_b[0:PACK_W, :]) + _dg(xb, wg_b[PACK_W:, :]) for xa, xb in xs]
        up = [_dg(xa, wu_b[0:PACK_W, :]) + _dg(xb, wu_b[PACK_W:, :]) for xa, xb in xs]
        row = lax.broadcasted_iota(I32, (hr, 1), 0)
        h = [jnp.where((row + s * hr >= lo) & (row + s * hr < hi), g * _sigmoid(g) * u, 0.0).astype(BF16)
             for s, (g, u) in enumerate(zip(gate, up))]
        y = [_dg(hh, wd_b[...]) for hh in h]

        @pl.when(first == 1)
        def _():
            for sl, yy in zip(halves, y):
                o_ref[sl, :] = yy

        @pl.when(first != 1)
        def _():
            for sl, yy in zip(halves, y):
                o_ref[sl, :] += yy


def _grouped_mlp(visits, xs, w_gate, w_up, w_down, layer, n_visits):
    rows = xs.shape[0]
    tm = GMM_TILE
    by_tile = lambda vi, vt, ve, vlo, vhi, vf, vn: (vt[vi], 0)
    by_expert = lambda vi, vt, ve, vlo, vhi, vf, vn: (layer, ve[vi], 0, 0)
    grid_spec = pltpu.PrefetchScalarGridSpec(
        num_scalar_prefetch=6,
        grid=(n_visits,),
        in_specs=[pl.BlockSpec((tm, PACK_W), by_tile),
                  pl.BlockSpec((None, None, D_MODEL, EXPERT_FF), by_expert),
                  pl.BlockSpec((None, None, D_MODEL, EXPERT_FF), by_expert),
                  pl.BlockSpec((None, None, EXPERT_FF, D_MODEL), by_expert)],
        out_specs=pl.BlockSpec((tm, D_MODEL), by_tile),
        scratch_shapes=[pltpu.VMEM((D_MODEL, EXPERT_FF), BF16), pltpu.VMEM((D_MODEL, EXPERT_FF), BF16),
                        pltpu.VMEM((EXPERT_FF, D_MODEL), BF16)],
    )
    return pl.pallas_call(
        _gmm_kernel,
        grid_spec=grid_spec,
        out_shape=jax.ShapeDtypeStruct((rows, D_MODEL), F32),
        compiler_params=_cparams(1),
        name="moe_grouped_mlp",
    )(*visits, xs, w_gate, w_up, w_down)


def _visit_list(counts, n_rows, tm, n_visits):
    ends = jnp.cumsum(counts)
    offs = ends - counts
    first_tile = offs // tm
    last_tile = jnp.where(counts > 0, (ends - 1) // tm, first_tile - 1)
    n_t = last_tile - first_tile + 1
    v_end = jnp.cumsum(n_t)
    v_start = v_end - n_t
    total = v_end[-1]
    vi = jnp.arange(n_visits, dtype=I32)
    e = jnp.minimum(jnp.sum(v_end[None, :] <= vi[:, None], axis=1), N_EXPERTS - 1).astype(I32)
    active = vi < total
    last_e = jnp.max(jnp.where(counts > 0, jnp.arange(N_EXPERTS, dtype=I32), 0))
    e = jnp.where(active, e, last_e).astype(I32)
    is_e = e[:, None] == jnp.arange(N_EXPERTS, dtype=I32)[None, :]
    pick = lambda table: jnp.sum(jnp.where(is_e, table[None, :], 0), axis=1)
    tile = pick(first_tile) + (vi - pick(v_start))
    tile = jnp.where(active, tile, n_rows // tm - 1).astype(I32)
    lo = jnp.where(active, jnp.maximum(pick(offs), tile * tm) - tile * tm, 0).astype(I32)
    hi = jnp.where(active, jnp.minimum(pick(ends), (tile + 1) * tm) - tile * tm, 0).astype(I32)
    prev_tile = jnp.concatenate([jnp.full((1,), -1, I32), tile[:-1]])
    first = (active & (tile != prev_tile)).astype(I32)
    prev_e = jnp.concatenate([jnp.full((1,), -1, I32), e[:-1]])
    new_e = (e != prev_e).astype(I32)
    return tile, e, lo, hi, first, new_e


def _moe_finish_kernel(pos_hbm, ys_hbm, w_ref, xpk_ref, x_ref, wg_ref, wu_ref, wd_ref, g_ref, b_ref,
                       o_ref, obf_ref, pos_smem, buf, sem_idx, sem, *, tm):
    i = pl.program_id(0)
    cp_pos = pltpu.make_async_copy(pos_hbm.at[i], pos_smem, sem_idx)
    cp_pos.start()
    cp_pos.wait()

    def row_copy(t, kk):
        return pltpu.make_async_copy(ys_hbm.at[pl.ds(pos_smem[kk, t], 1), :], buf.at[kk, pl.ds(t, 1), :], sem)

    def issue(g, carry):
        base = pl.multiple_of(g * ROW_GROUP, ROW_GROUP)
        for u in range(ROW_GROUP):
            for kk in range(TOP_K):
                row_copy(base + u, kk).start()
        return carry

    lax.fori_loop(0, tm // ROW_GROUP, issue, 0)

    xa, xb = _unpack_rows(xpk_ref[...])
    gate = _dg(xa, wg_ref[0:PACK_W, :]) + _dg(xb, wg_ref[PACK_W:, :])
    up = _dg(xa, wu_ref[0:PACK_W, :]) + _dg(xb, wu_ref[PACK_W:, :])
    h = gate * _sigmoid(gate) * up
    y = DEEPNORM_ALPHA * x_ref[...] + _dg(h.astype(BF16), wd_ref[...])

    pltpu.make_async_copy(buf, buf, sem).wait()
    w = w_ref[...]
    routed = w[:, 0:1] * buf[0]
    for kk in range(1, TOP_K):
        routed = routed + w[:, kk:kk + 1] * buf[kk]
    out = _layer_norm(y + routed, g_ref[...], b_ref[...])
    o_ref[...] = out
    obf_ref[...] = out.astype(BF16)


def _moe_finish(pos_tiles, ys, w_tok, x_pk, x, w_gate, w_up, w_down, g, b, tm):
    m = x.shape[0]
    row = lambda n: pl.BlockSpec((tm, n), lambda i: (i, 0))
    vec = pl.BlockSpec((1, D_MODEL), lambda i: (0, 0))
    return pl.pallas_call(
        functools.partial(_moe_finish_kernel, tm=tm),
        grid=(m // tm,),
        in_specs=[pl.BlockSpec(memory_space=pl.ANY), pl.BlockSpec(memory_space=pl.ANY),
                  row(TOP_K), row(PACK_W), row(D_MODEL),
                  pl.BlockSpec((D_MODEL, SHARED_FF), lambda i: (0, 0)),
                  pl.BlockSpec((D_MODEL, SHARED_FF), lambda i: (0, 0)),
                  pl.BlockSpec((SHARED_FF, D_MODEL), lambda i: (0, 0)), vec, vec],
        out_specs=[row(D_MODEL), row(D_MODEL)],
        out_shape=[jax.ShapeDtypeStruct((m, D_MODEL), F32), jax.ShapeDtypeStruct((m, D_MODEL), BF16)],
        scratch_shapes=[pltpu.SMEM((TOP_K, tm), I32), pltpu.VMEM((TOP_K, tm, D_MODEL), F32),
                        pltpu.SemaphoreType.DMA, pltpu.SemaphoreType.DMA],
        compiler_params=_cparams(1),
        name="moe_finish",
    )(pos_tiles, ys, w_tok, x_pk, x, w_gate, w_up, w_down, g, b)


def _pad_cols(w, n):
    return jnp.pad(w, ((0, 0), (0, n - w.shape[1])))


def _in_proj_weight(w_in_l):
    c0 = 6 * BRANCH_W
    rw = c0
    lora = rw + 3 * RWKV_W
    parts = [w_in_l[:, :c0],
             w_in_l[:, c0 + 3 * RWKV_W + DECAY_LORA + AAA_LORA + GATE_LORA:],
             w_in_l[:, rw:lora],
             _pad_cols(w_in_l[:, lora:lora + DECAY_LORA], LORA_PAD),
             _pad_cols(w_in_l[:, lora + DECAY_LORA:lora + DECAY_LORA + AAA_LORA], LORA_PAD),
             w_in_l[:, lora + DECAY_LORA + AAA_LORA:lora + DECAY_LORA + AAA_LORA + GATE_LORA]]
    return jnp.concatenate(parts, axis=1).astype(BF16)


def _rwkv_mix_vector(mix_l):
    lora = 3 * RWKV_W
    parts = [mix_l[:lora],
             jnp.pad(mix_l[lora:lora + DECAY_LORA], (0, LORA_PAD - DECAY_LORA)),
             jnp.pad(mix_l[lora + DECAY_LORA:lora + DECAY_LORA + AAA_LORA], (0, LORA_PAD - AAA_LORA)),
             mix_l[lora + DECAY_LORA + AAA_LORA:]]
    return jnp.concatenate(parts)[None, :]


def _pad_rows(w, n):
    return jnp.pad(w, ((0, n - w.shape[0]), (0, 0))).astype(BF16)


def kernel(x, w_in, conv_w, rwkv_mix, rwkv_w0, rwkv_w2, rwkv_a0, rwkv_a2, rwkv_g2, rwkv_kk, rwkv_ka, rwkv_rk,
           rwkv_lnx_g, rwkv_lnx_b, w_branch, w_o, ln1_g, ln1_b, w_router, router_bias, w_e_gate, w_e_up,
           w_e_down, w_s_gate, w_s_up, w_s_down, ln2_g, ln2_b):
    bsz, seq, d = x.shape
    n_tok = bsz * seq
    n_rows = n_tok * TOP_K
    n_visits = n_rows // GMM_TILE + N_EXPERTS
    io_tile = 256
    xf = x.reshape(n_tok, d)
    xb = xf.astype(BF16)
    row1 = lambda v: v[None, :]
    for l in range(DEPTH):
        w_in_r = _in_proj_weight(w_in[l])
        p_conv = _matmul_cols(xb, w_in_r, COL_CONV, 3 * BRANCH_W, 1024, 1024, F32)
        p_att = _matmul_cols(xb, w_in_r, COL_ATT, 3 * BRANCH_W, 1024, 1024, BF16)
        p_gate = _matmul_cols(xb, w_in_r, COL_GATE, 3 * D_MODEL, 1024, 1024, F32)
        p_rwkv = _matmul_cols(xb, w_in_r, COL_RWKV, RWKV_COLS, 512, 1024, F32)

        y_conv = _short_conv(p_conv, conv_w[l], bsz, seq)
        y_att = _stick_breaking(p_att, bsz, seq)
        y_rwkv = _rwkv_mixer(p_rwkv, _rwkv_mix_vector(rwkv_mix[l]), row1(rwkv_w0[l]),
                             _pad_rows(rwkv_w2[l], LORA_PAD), row1(rwkv_a0[l]), _pad_rows(rwkv_a2[l], LORA_PAD),
                             rwkv_g2[l].astype(BF16), row1(rwkv_kk[l]), row1(rwkv_ka[l]),
                             rwkv_rk[l].reshape(1, RWKV_W), row1(rwkv_lnx_g[l]), row1(rwkv_lnx_b[l]), bsz, seq)

        merged = _merge(y_conv.reshape(n_tok, BRANCH_W), y_att.reshape(n_tok, BRANCH_W),
                        y_rwkv.reshape(n_tok, BRANCH_W), w_branch[l].astype(BF16), p_gate)
        x1, x1_pk = _oproj_ln(merged, w_o[l].astype(BF16), xf, row1(ln1_g[l]), row1(ln1_b[l]))

        top_i, top_w, rank, cnt = _router(x1, w_router[l].T, router_bias[l][:, None])
        counts = cnt[:, 0].astype(I32)
        offs = jnp.cumsum(counts) - counts
        pos_tiles = _positions(offs, top_i, rank, io_tile)
        xs = _dispatch(pos_tiles, x1_pk, io_tile)
        visits = _visit_list(counts, n_rows, GMM_TILE, n_visits)
        ys = _grouped_mlp(visits, xs, w_e_gate, w_e_up, w_e_down, l, n_visits)
        xf, xb = _moe_finish(pos_tiles, ys, top_w.T, x1_pk, x1, w_s_gate[l].astype(BF16),
                             w_s_up[l].astype(BF16), w_s_down[l].astype(BF16), row1(ln2_g[l]), row1(ln2_b[l]),
                             io_tile)
    return xf.reshape(bsz, seq, d)
```

```python
import functools

import jax
import jax.numpy as jnp
from jax import lax
from jax.experimental import pallas as pl
from jax.experimental.pallas import tpu as pltpu

F32 = jnp.float32
BF16 = jnp.bfloat16
I32 = jnp.int32
U32 = jnp.uint32

D_MODEL = 2048
DEPTH = 2
BRANCH_W = 1024
CONV_K = 3
ATT_HEADS = 8
ATT_HD = 128
RWKV_HEADS = 16
RWKV_HD = 64
RWKV_W = RWKV_HEADS * RWKV_HD
DECAY_LORA = 96
AAA_LORA = 96
GATE_LORA = 256
LORA_PAD = 128
N_EXPERTS = 64
TOP_K = 8
N_GROUPS = 8
GROUP_SIZE = N_EXPERTS // N_GROUPS
TOPK_GROUPS = 4
EXPERT_FF = 512
SHARED_FF = 512
ROUTED_SCALE = 2.5
LN_EPS = 1e-5
GN_EPS = 64e-5
DEEPNORM_ALPHA = (2 * DEPTH) ** 0.25

COL_CONV = 0
COL_ATT = 3 * BRANCH_W
COL_RKV = 6 * BRANCH_W
GL_GATE = 0
GL_LORA = 3 * D_MODEL
LORA_COLS = 2 * LORA_PAD + GATE_LORA

RWKV_CHUNK = 64
RWKV_GROUP = 4
RWKV_GW = RWKV_GROUP * RWKV_HD
RWKV_GROUPS_PER_STEP = 2
ATT_TILE = 512
ATT_KEY_TILE = 256
ATT_ROW_CHUNK = 32
ATT_UNDERFLOW_LOG = -105.0
GMM_TILE = 256
LANES = 128
ROW_GROUP = 8
PACK_W = D_MODEL // 2

VMEM_LIMIT = 56 * 1024 * 1024

NN = ((1,), (0,))
NT = ((1,), (1,))


def _cparams(n_axes):
    return pltpu.CompilerParams(dimension_semantics=("arbitrary",) * n_axes,
                                vmem_limit_bytes=VMEM_LIMIT)


def _dg(a, b, dims=NN):
    return lax.dot_general(a, b, (dims, ((), ())), preferred_element_type=F32)


def _split2(x):
    hi = x.astype(BF16)
    lo = (x - hi.astype(F32)).astype(BF16)
    return hi, lo


def _split3(x):
    hi = x.astype(BF16)
    r1 = x - hi.astype(F32)
    mid = r1.astype(BF16)
    lo = (r1 - mid.astype(F32)).astype(BF16)
    return hi, mid, lo


def _mm3(a, b, dims=NN):
    ah, al = _split2(a)
    bh, bl = _split2(b)
    return _dg(ah, bh, dims) + _dg(ah, bl, dims) + _dg(al, bh, dims)


def _sigmoid(x):
    return 1.0 / (1.0 + jnp.exp(-x))


def _softplus(x):
    return jnp.maximum(x, 0.0) + jnp.log(1.0 + jnp.exp(-jnp.abs(x)))


def _mm_kernel(x_ref, w_ref, o_ref):
    o_ref[...] = _dg(x_ref[...], w_ref[...]).astype(o_ref.dtype)


def _matmul_cols(x, w, col_off, ncols, tn, tm, out_dtype):
    m, k = x.shape
    off = col_off // tn
    return pl.pallas_call(
        _mm_kernel,
        grid=(ncols // tn, m // tm),
        in_specs=[pl.BlockSpec((tm, k), lambda j, i: (i, 0)),
                  pl.BlockSpec((k, tn), lambda j, i: (0, off + j))],
        out_specs=pl.BlockSpec((tm, tn), lambda j, i: (i, j)),
        out_shape=jax.ShapeDtypeStruct((m, ncols), out_dtype),
        compiler_params=_cparams(2),
        name="in_proj",
    )(x, w)


def _mm_f32w_kernel(x_ref, w_ref, o_ref, wb_ref):
    @pl.when(pl.program_id(1) == 0)
    def _():
        wb_ref[...] = w_ref[...].astype(BF16)

    o_ref[...] = _dg(x_ref[...], wb_ref[...]).astype(o_ref.dtype)


def _matmul_cols_f32w(x, w_stack, layer, col_off, ncols, tn, tm, out_dtype):
    m, k = x.shape
    off = col_off // tn
    return pl.pallas_call(
        _mm_f32w_kernel,
        grid=(ncols // tn, m // tm),
        in_specs=[pl.BlockSpec((tm, k), lambda j, i: (i, 0)),
                  pl.BlockSpec((None, k, tn), lambda j, i: (layer, 0, off + j))],
        out_specs=pl.BlockSpec((tm, tn), lambda j, i: (i, j)),
        out_shape=jax.ShapeDtypeStruct((m, ncols), out_dtype),
        scratch_shapes=[pltpu.VMEM((k, tn), BF16)],
        compiler_params=_cparams(2),
        name="in_proj_f32w",
    )(x, w_stack)


def _conv_kernel(gb_ref, gc_ref, hc_ref, gch_ref, hch_ref, cw_ref, o_ref):
    i = pl.program_id(1)
    u = gc_ref[...] * hc_ref[...]
    uh = gch_ref[...] * hch_ref[...]
    uh = jnp.where(i > 0, uh, 0.0)
    row = lax.broadcasted_iota(I32, u.shape, 0)
    u1 = jnp.where(row == 0, uh[7:8, :], pltpu.roll(u, 1, axis=0))
    u2 = pltpu.roll(u, 2, axis=0)
    u2 = jnp.where(row == 0, uh[6:7, :], jnp.where(row == 1, uh[7:8, :], u2))
    cw = cw_ref[...]
    y = gb_ref[...] * (cw[2:3, :] * u + cw[1:2, :] * u1 + cw[0:1, :] * u2)
    o_ref[...] = y.astype(o_ref.dtype)


def _short_conv(p_conv, conv_w, bsz, seq, ts=512):
    p3 = p_conv.reshape(bsz, seq, 3 * BRANCH_W)
    hb = ts // 8
    halo = lambda c: pl.BlockSpec((None, 8, BRANCH_W),
                                  lambda b, i: (b, jnp.maximum(i * hb - 1, 0), c))
    main = lambda c: pl.BlockSpec((None, ts, BRANCH_W), lambda b, i: (b, i, c))
    return pl.pallas_call(
        _conv_kernel,
        grid=(bsz, seq // ts),
        in_specs=[main(0), main(1), main(2), halo(1), halo(2),
                  pl.BlockSpec((CONV_K, BRANCH_W), lambda b, i: (0, 0))],
        out_specs=pl.BlockSpec((None, ts, BRANCH_W), lambda b, i: (b, i, 0)),
        out_shape=jax.ShapeDtypeStruct((bsz, seq, BRANCH_W), BF16),
        compiler_params=_cparams(2),
        name="short_conv",
    )(p3, p3, p3, p3, p3, conv_w)


def _attn_kernel(q_ref, k_ref, v_ref, o_ref, z_s, lb_s, hl_s, tail_s, w_s, c_s, rs_s, acc_s, *, tq, tk, scale):
    i = pl.program_id(2)
    n_sub = tq // tk
    rc = ATT_ROW_CHUNK
    suffix = (lax.broadcasted_iota(I32, (tk, tk), 0)
              > lax.broadcasted_iota(I32, (tk, tk), 1)).astype(BF16)
    suffix2 = jnp.concatenate([suffix, suffix], axis=0)
    r_io = lax.broadcasted_iota(I32, (rc, tk), 0)
    c_io = lax.broadcasted_iota(I32, (rc, tk), 1)
    acc_s[...] = jnp.zeros_like(acc_s)
    c_s[...] = jnp.zeros_like(c_s)
    n_chunks = tq // rc

    def key_rows(j):
        return pl.ds(pl.multiple_of(j * tk, tk), tk)

    def logits(s, diag_off):
        def step(r, carry):
            rows = pl.ds(pl.multiple_of(r * rc, rc), rc)
            z = z_s[s, rows, :] * scale
            log_beta = jnp.minimum(z, 0.0) - jnp.log(1.0 + jnp.exp(-jnp.abs(z)))
            log_1mb = log_beta - z
            if diag_off is not None:
                causal = (c_io + diag_off * tk) < (r_io + r * rc)
                log_1mb = jnp.where(causal, log_1mb, 0.0)
            hi, lo = _split2(log_1mb)
            hl_s[s, rows, 0:tk] = hi
            hl_s[s, rows, tk:2 * tk] = lo
            lb_s[s, rows, :] = log_beta
            rs_s[s, rows, :] = jnp.sum(log_1mb, axis=1, keepdims=True)
            return carry
        lax.fori_loop(0, n_chunks, step, 0, unroll=True)

    def weights(s, diag_off):
        def step(r, carry):
            rows = pl.ds(pl.multiple_of(r * rc, rc), rc)
            w = jnp.exp(lb_s[s, rows, :] + tail_s[s, rows, :] + c_s[rows, :])
            if diag_off is not None:
                causal = (c_io + diag_off * tk) < (r_io + r * rc)
                w = jnp.where(causal, w, 0.0)
            w_s[s, rows, :] = w.astype(BF16)
            return carry
        lax.fori_loop(0, n_chunks, step, 0, unroll=True)

    def block_pair(j_hi, diag):
        js = (j_hi, j_hi - 1)
        offs = (1, 0) if diag else (None, None)
        for s in range(2):
            z_s[s] = _dg(q_ref[...], k_ref[key_rows(js[s]), :], NT)
        for s in range(2):
            logits(s, offs[s])
        for s in range(2):
            tail_s[s] = _dg(hl_s[s], suffix2)
        for s in range(2):
            weights(s, offs[s])
            c_s[...] += rs_s[s]
        for s in range(2):
            acc_s[...] += _dg(w_s[s], v_ref[key_rows(js[s]), :])

    block_pair(n_sub * i + 1, True)

    def more(state):
        jj, c_max = state
        return jnp.logical_and(jj < i, c_max > ATT_UNDERFLOW_LOG)

    def below(state):
        jj, _ = state
        block_pair(n_sub * i - 1 - 2 * jj, False)
        return jj + 1, jnp.max(c_s[...])

    lax.while_loop(more, below, (jnp.int32(0), jnp.max(c_s[...])))
    o_ref[...] = acc_s[...].astype(o_ref.dtype)


def _stick_breaking(p_att, bsz, seq):
    p3 = p_att.reshape(bsz, seq, 3 * BRANCH_W)
    tq = ATT_TILE
    tk = ATT_KEY_TILE
    assert tq == 2 * tk, "the attention kernel walks the keys in pairs of blocks"
    kern = functools.partial(_attn_kernel, tq=tq, tk=tk, scale=ATT_HD ** -0.5)
    return pl.pallas_call(
        kern,
        grid=(bsz, ATT_HEADS, seq // tq),
        in_specs=[pl.BlockSpec((None, tq, ATT_HD), lambda b, h, i: (b, i, h)),
                  pl.BlockSpec((None, seq, ATT_HD), lambda b, h, i: (b, 0, ATT_HEADS + h)),
                  pl.BlockSpec((None, seq, ATT_HD), lambda b, h, i: (b, 0, 2 * ATT_HEADS + h))],
        out_specs=pl.BlockSpec((None, tq, ATT_HD), lambda b, h, i: (b, i, h)),
        out_shape=jax.ShapeDtypeStruct((bsz, seq, BRANCH_W), BF16),
        scratch_shapes=[pltpu.VMEM((2, tq, tk), F32), pltpu.VMEM((2, tq, tk), F32),
                        pltpu.VMEM((2, tq, 2 * tk), BF16), pltpu.VMEM((2, tq, tk), F32),
                        pltpu.VMEM((2, tq, tk), BF16), pltpu.VMEM((tq, 1), F32),
                        pltpu.VMEM((2, tq, 1), F32), pltpu.VMEM((tq, ATT_HD), F32)],
        compiler_params=_cparams(3),
        name="stick_breaking_attention",
    )(p3, p3, p3)


def _head_sum(x, ones_bd):
    hi, lo = _split2(x)
    outs = []
    for c in range(x.shape[1] // RWKV_GW):
        sl = slice(RWKV_GW * c, RWKV_GW * (c + 1))
        outs.append(_dg(hi[:, sl], ones_bd) + _dg(lo[:, sl], ones_bd))
    return outs[0] if len(outs) == 1 else jnp.concatenate(outs, axis=1)


def _head_ones():
    r = lax.broadcasted_iota(I32, (RWKV_GW, RWKV_GW), 0) // RWKV_HD
    c = lax.broadcasted_iota(I32, (RWKV_GW, RWKV_GW), 1) // RWKV_HD
    return (r == c).astype(BF16)


RWKV_LEVELS = RWKV_CHUNK.bit_length() - 1


def _rwkv_masks():
    r = jnp.arange(RWKV_GW)[:, None]
    c = jnp.arange(RWKV_GW)[None, :]
    bd = (r // RWKV_CHUNK) == (c // RWKV_HD)
    planes = [bd, bd & (r > c), bd & (r >= c)]
    for m in range(RWKV_LEVELS):
        s = 1 << m
        planes.append((r // (2 * s) == c // (2 * s)) & (r % (2 * s) >= s) & (c % (2 * s) < s))
    return jnp.stack(planes).astype(F32), bd.astype(BF16)


def _rwkv_scan_kernel(pr_ref, pk_ref, pv_ref, pl_ref, mr_ref, mk_ref, mv_ref, ml_ref, w0_ref, w2_ref, a0_ref,
                      a2_ref, g2_ref, kk_ref, ka_ref, rk_ref, lng_ref, lnb_ref, msk_ref, bdb_ref,
                      o_ref, s_ref, prev_ref):
    @pl.when(pl.program_id(1) == 0)
    def _():
        s_ref[...] = jnp.zeros_like(s_ref)
        prev_ref[...] = jnp.zeros_like(prev_ref)

    c_len = RWKV_CHUNK
    gw = RWKV_GW
    n_b = pr_ref.shape[0]
    chains = [(bb, gi) for bb in range(n_b) for gi in range(pr_ref.shape[2] // gw)]
    first_row = lax.broadcasted_iota(I32, (c_len, 1), 0) == 0

    def smap(f, *lists):
        return [f(*xs) for xs in zip(*lists)]

    def shifted_mix(ref, mix_ref, slot):
        outs = []
        for bb in range(n_b):
            x = ref[bb]
            shifted = jnp.where(first_row, prev_ref[bb, slot], pltpu.roll(x, 1, axis=0))
            prev_ref[bb, slot] = x[c_len - 1:c_len, :]
            outs.append(x + (shifted - x) * mix_ref[...])
        return outs

    ps_r, ps_k, ps_v = shifted_mix(pr_ref, mr_ref, 0), shifted_mix(pk_ref, mk_ref, 1), shifted_mix(pv_ref, mv_ref, 2)
    ps_l = shifted_mix(pl_ref, ml_ref, 3)
    lora_w = [jnp.tanh(x[:, 0:LORA_PAD]).astype(BF16) for x in ps_l]
    lora_a = [x[:, LORA_PAD:2 * LORA_PAD].astype(BF16) for x in ps_l]
    lora_g = [_sigmoid(x[:, 2 * LORA_PAD:]).astype(BF16) for x in ps_l]
    ones_hd = _head_ones()
    r, lw, k, v, a, b, g = [], [], [], [], [], [], []
    for bb, gi in chains:
        sl = slice(gi * gw, (gi + 1) * gw)
        w_log = -_softplus(-(w0_ref[:, sl] + _dg(lora_w[bb], w2_ref[:, sl]))) - 0.5
        rate = _sigmoid(a0_ref[:, sl] + _dg(lora_a[bb], a2_ref[:, sl]))
        kk = ps_k[bb][:, sl] * kk_ref[:, sl]
        kk = kk / jnp.maximum(jnp.sqrt(_head_sum(kk * kk, ones_hd)), 1e-12)
        r.append(ps_r[bb][:, sl])
        lw.append(-jnp.exp(w_log))
        k.append(ps_k[bb][:, sl] * (1.0 + (rate - 1.0) * ka_ref[:, sl]))
        v.append(ps_v[bb][:, sl])
        a.append(-kk)
        b.append(kk * rate)
        g.append(_dg(lora_g[bb], g2_ref[:, sl]))

    def par(ref):
        return [ref[:, gi * gw:(gi + 1) * gw] for _, gi in chains]

    def tile_heads(x):
        return jnp.concatenate([x.astype(BF16)] * RWKV_GROUP, axis=0)

    def unstack(z):
        out = z[0:c_len]
        for h in range(1, RWKV_GROUP):
            out = out + z[h * c_len:(h + 1) * c_len]
        return out

    tril_c =(lax.broadcasted_iota(I32, (c_len, c_len), 0)
              >= lax.broadcasted_iota(I32, (c_len, c_len), 1)).astype(BF16)

    def cum_decay(x):
        hi, mid, lo = _split3(x)
        return _dg(tril_c, hi) + _dg(tril_c, mid) + _dg(tril_c, lo)

    cum = smap(cum_decay, lw)
    cum_end = smap(lambda c: c[c_len - 1:c_len, :], cum)
    e_inv = smap(lambda c: jnp.exp(-c), cum)
    e_rem = smap(lambda ce, c: jnp.exp(ce - c), cum_end, cum)

    bd = msk_ref[0]
    strict = msk_ref[1]
    incl = msk_ref[2]
    head_b = bdb_ref[...]
    a_s = smap(lambda x, c, l: tile_heads(x * jnp.exp(c - l)) * head_b, a, cum, lw)
    r_s = smap(lambda x, c: tile_heads(x * jnp.exp(c)) * head_b, r, cum)
    b_x = smap(lambda x, e: tile_heads(x * e), b, e_inv)
    k_x = smap(lambda x, e: tile_heads(x * e), k, e_inv)
    v_x = smap(tile_heads, v)

    a_ab = smap(lambda p, q: _dg(p, q, NT) * strict, a_s, b_x)
    a_ak = smap(lambda p, q: (_dg(p, q, NT) * strict).astype(BF16), a_s, k_x)
    a_rb = smap(lambda p, q: (_dg(p, q, NT) * incl).astype(BF16), r_s, b_x)
    a_rk = smap(lambda p, q: (_dg(p, q, NT) * incl).astype(BF16), r_s, k_x)

    inv = smap(lambda m: (incl - strict) + m * msk_ref[3], a_ab)
    for lvl in range(1, RWKV_LEVELS):
        inv_b = smap(lambda t: t.astype(BF16), inv)
        low = smap(lambda m: (m * msk_ref[3 + lvl]).astype(BF16), a_ab)
        mid = smap(lambda t, lo_: _dg(t, lo_).astype(BF16), inv_b, low)
        inv = smap(lambda t, md, tb: t + _dg(md, tb), inv, mid, inv_b)
    inv_b = smap(lambda t: t.astype(BF16), inv)

    s0 = [s_ref[bb, gi] for bb, gi in chains]
    s0_b = smap(lambda s: s.astype(BF16), s0)
    rhs = smap(lambda p, s, m, vv: (_dg(p, s, NT) + _dg(m, vv)).astype(BF16), a_s, s0_b, a_ak, v_x)
    u = smap(_dg, inv_b, rhs)
    u_b = smap(lambda t: t.astype(BF16), u)
    y = smap(lambda p, s, m1, ub, m2, vv: unstack((_dg(p, s, NT) + _dg(m1, ub) + _dg(m2, vv)) * bd),
             r_s, s0_b, a_rb, u_b, a_rk, v_x)
    uv_t = smap(lambda uu, vv: jnp.concatenate([unstack(uu * bd), vv], axis=0).T.astype(BF16), u, v)
    bk_h = smap(lambda bb_, kk_, e: jnp.concatenate([bb_ * e, kk_ * e], axis=0).astype(BF16), b, k, e_rem)
    s_new = smap(lambda s, ce, p, q: s * jnp.exp(ce) + _dg(p, q) * bd, s0, cum_end, uv_t, bk_h)
    for (bb, gi), s in zip(chains, s_new):
        s_ref[bb, gi] = s

    inv_n = 1.0 / RWKV_HD
    d = smap(lambda yy: yy - _head_sum(yy, ones_hd) * inv_n, y)
    var = smap(lambda dd: _head_sum(dd * dd, ones_hd) * inv_n, d)
    yn = smap(lambda dd, vr, gg, bb_: dd * lax.rsqrt(vr + GN_EPS) * gg + bb_, d, var, par(lng_ref), par(lnb_ref))
    bonus = smap(lambda rr, kk_, rk, vv: _head_sum(rr * kk_ * rk, ones_hd) * vv, r, k, par(rk_ref), v)
    for (bb, gi), yy, bo, gg in zip(chains, yn, bonus, g):
        o_ref[bb, :, gi * gw:(gi + 1) * gw] = ((yy + bo) * gg).astype(o_ref.dtype)


def _rwkv_mixer(p_rkv, p_lora, mix, mix_lo, w0, w2, a0, a2, g2, k_k, k_a, r_k, lnx_g, lnx_b, bsz, seq):
    p3 = p_rkv.reshape(bsz, seq, 3 * RWKV_W)
    pl3 = p_lora.reshape(bsz, seq, LORA_COLS)
    n_groups = RWKV_W // RWKV_GW
    gps = RWKV_GROUPS_PER_STEP
    bw = gps * RWKV_GW
    n_cb = RWKV_W // bw
    assert LORA_COLS == bw
    tok = lambda cb0: pl.BlockSpec((bsz, RWKV_CHUNK, bw), lambda gg, i: (0, i, cb0 + gg))
    tok_lora = pl.BlockSpec((bsz, RWKV_CHUNK, bw), lambda gg, i: (0, i, 0))
    mixv = lambda cb0: pl.BlockSpec((1, bw), lambda gg, i: (0, cb0 + gg))
    mix_lora = pl.BlockSpec((1, bw), lambda gg, i: (0, 0))
    vec = pl.BlockSpec((1, bw), lambda gg, i: (0, gg))
    mat = lambda rows: pl.BlockSpec((rows, bw), lambda gg, i: (0, gg))
    masks, head_mask = _rwkv_masks()
    return pl.pallas_call(
        _rwkv_scan_kernel,
        grid=(n_groups // gps, seq // RWKV_CHUNK),
        in_specs=[tok(0), tok(n_cb), tok(2 * n_cb), tok_lora,
                  mixv(0), mixv(n_cb), mixv(2 * n_cb), mix_lora,
                  vec, mat(LORA_PAD), vec, mat(LORA_PAD), mat(GATE_LORA), vec, vec, vec, vec, vec,
                  pl.BlockSpec((3 + RWKV_LEVELS, RWKV_GW, RWKV_GW), lambda gg, i: (0, 0, 0)),
                  pl.BlockSpec((RWKV_GW, RWKV_GW), lambda gg, i: (0, 0))],
        out_specs=pl.BlockSpec((bsz, RWKV_CHUNK, bw), lambda gg, i: (0, i, gg)),
        out_shape=jax.ShapeDtypeStruct((bsz, seq, RWKV_W), BF16),
        scratch_shapes=[pltpu.VMEM((bsz, gps, RWKV_GW, RWKV_GW), F32),
                        pltpu.VMEM((bsz, 4, 1, bw), F32)],
        compiler_params=_cparams(2),
        name="rwkv_mixer",
    )(p3, p3, p3, pl3, mix, mix, mix, mix_lo, w0, w2, a0, a2, g2, k_k, k_a, r_k, lnx_g, lnx_b, masks, head_mask)


def _merge_kernel(yc_ref, ya_ref, yr_ref, wb_ref, g0_ref, g1_ref, g2_ref, o_ref):
    acc = None
    for n, (y_ref, g_ref) in enumerate(((yc_ref, g0_ref), (ya_ref, g1_ref), (yr_ref, g2_ref))):
        term = _sigmoid(g_ref[...]) * _dg(y_ref[...], wb_ref[n])
        acc = term if acc is None else acc + term
    o_ref[...] = acc.astype(o_ref.dtype)


def _merge(y_conv, y_att, y_rwkv, w_branch, p_gate, tm=512, tn=1024):
    m = y_conv.shape[0]
    nj = D_MODEL // tn
    ysp = pl.BlockSpec((tm, BRANCH_W), lambda j, i: (i, 0))
    gsp = lambda n: pl.BlockSpec((tm, tn), lambda j, i: (i, n * nj + j))
    return pl.pallas_call(
        _merge_kernel,
        grid=(nj, m // tm),
        in_specs=[ysp, ysp, ysp, pl.BlockSpec((3, BRANCH_W, tn), lambda j, i: (0, 0, j)),
                  gsp(0), gsp(1), gsp(2)],
        out_specs=pl.BlockSpec((tm, tn), lambda j, i: (i, j)),
        out_shape=jax.ShapeDtypeStruct((m, D_MODEL), BF16),
        compiler_params=_cparams(2),
        name="branch_merge",
    )(y_conv, y_att, y_rwkv, w_branch, p_gate, p_gate, p_gate)


def _layer_norm(y, g, b):
    mu = jnp.mean(y, axis=-1, keepdims=True)
    d = y - mu
    var = jnp.mean(d * d, axis=-1, keepdims=True)
    return d * lax.rsqrt(var + LN_EPS) * g + b


def _pack_rows(x):
    hi = lax.bitcast_convert_type(x[:, :PACK_W].astype(BF16).astype(F32), U32)
    lo = lax.bitcast_convert_type(x[:, PACK_W:].astype(BF16).astype(F32), U32)
    return hi | (lo >> 16)


def _unpack_rows(w):
    hi = lax.bitcast_convert_type(w & jnp.uint32(0xFFFF0000), F32).astype(BF16)
    lo = lax.bitcast_convert_type(w << 16, F32).astype(BF16)
    return hi, lo


def _oproj_ln_kernel(m_ref, w_ref, x_ref, g_ref, b_ref, o_ref, opk_ref):
    y = DEEPNORM_ALPHA * x_ref[...] + _dg(m_ref[...], w_ref[...])
    out = _layer_norm(y, g_ref[...], b_ref[...])
    o_ref[...] = out
    opk_ref[...] = _pack_rows(out)


def _oproj_ln(merged, w_o, x, g, b, tm=256):
    m = x.shape[0]
    row = lambda n: pl.BlockSpec((tm, n), lambda i: (i, 0))
    vec = pl.BlockSpec((1, D_MODEL), lambda i: (0, 0))
    return pl.pallas_call(
        _oproj_ln_kernel,
        grid=(m // tm,),
        in_specs=[row(D_MODEL), pl.BlockSpec((D_MODEL, D_MODEL), lambda i: (0, 0)), row(D_MODEL), vec, vec],
        out_specs=[row(D_MODEL), row(PACK_W)],
        out_shape=[jax.ShapeDtypeStruct((m, D_MODEL), F32), jax.ShapeDtypeStruct((m, PACK_W), U32)],
        compiler_params=_cparams(1),
        name="out_proj_ln",
    )(merged, w_o, x, g, b)


def _first_max(x, idx, n):
    m = jnp.max(x, axis=0, keepdims=True)
    first = jnp.min(jnp.where(x == m, idx, float(n)), axis=0, keepdims=True)
    return m, first


def _router_kernel(x_ref, wr_ref, bias_ref, ti_ref, tw_ref, rk_ref, cnt_ref, run_ref, *, tm):
    @pl.when(pl.program_id(0) == 0)
    def _():
        run_ref[...] = jnp.zeros_like(run_ref)

    xh, xl = _split2(x_ref[...])
    wh, wl = _split2(wr_ref[...])
    logits = _dg(wh, xh, NT) + _dg(wh, xl, NT) + _dg(wl, xh, NT)
    scores = _sigmoid(logits)
    biased = scores + bias_ref[...]
    neg = -jnp.inf

    sub8 = lax.broadcasted_iota(I32, (GROUP_SIZE, tm), 0).astype(F32)
    grp_rows = []
    for gi in range(N_GROUPS):
        blk = biased[gi * GROUP_SIZE:(gi + 1) * GROUP_SIZE, :]
        m1, f1 = _first_max(blk, sub8, GROUP_SIZE)
        m2 = jnp.max(jnp.where(sub8 == f1, neg, blk), axis=0, keepdims=True)
        grp_rows.append(m1 + m2)
    cur = jnp.concatenate(grp_rows, axis=0)
    grp_io = lax.broadcasted_iota(I32, (N_GROUPS, tm), 0).astype(F32)
    keep = jnp.zeros((N_GROUPS, tm), F32)
    for _ in range(TOPK_GROUPS):
        _, f = _first_max(cur, grp_io, N_GROUPS)
        hit = grp_io == f
        keep = jnp.where(hit, 1.0, keep)
        cur = jnp.where(hit, neg, cur)
    exp_keep = jnp.concatenate(
        [jnp.broadcast_to(keep[gi:gi + 1, :], (GROUP_SIZE, tm)) for gi in range(N_GROUPS)], axis=0)
    cand = jnp.where(exp_keep > 0.5, biased, neg)

    exp_io = lax.broadcasted_iota(I32, (N_EXPERTS, tm), 0).astype(F32)
    sel = jnp.zeros((N_EXPERTS, tm), F32)
    idxs, vals = [], []
    for _ in range(TOP_K):
        _, f = _first_max(cand, exp_io, N_EXPERTS)
        hit = exp_io == f
        idxs.append(f)
        vals.append(jnp.sum(jnp.where(hit, scores, 0.0), axis=0, keepdims=True))
        sel = jnp.where(hit, 1.0, sel)
        cand = jnp.where(hit, neg, cand)
    top_s = jnp.concatenate(vals, axis=0)
    ti_ref[...] = jnp.concatenate(idxs, axis=0).astype(I32)
    tw_ref[...] = top_s / jnp.sum(top_s, axis=0, keepdims=True) * ROUTED_SCALE

    before = (lax.broadcasted_iota(I32, (tm, tm), 0) < lax.broadcasted_iota(I32, (tm, tm), 1)).astype(BF16)
    rank_all = _dg(sel.astype(BF16), before) + run_ref[...][:, 0:1]
    ranks = [jnp.sum(jnp.where(exp_io == f, rank_all, 0.0), axis=0, keepdims=True) for f in idxs]
    rk_ref[...] = jnp.concatenate(ranks, axis=0).astype(I32)
    run_ref[...] = run_ref[...] + jnp.sum(sel, axis=1, keepdims=True)
    cnt_ref[...] = run_ref[...]


def _router(x, wr_t, bias, tm=512):
    m = x.shape[0]
    slot = pl.BlockSpec((TOP_K, tm), lambda i: (0, i))
    return pl.pallas_call(
        functools.partial(_router_kernel, tm=tm),
        grid=(m // tm,),
        in_specs=[pl.BlockSpec((tm, D_MODEL), lambda i: (i, 0)),
                  pl.BlockSpec((N_EXPERTS, D_MODEL), lambda i: (0, 0)),
                  pl.BlockSpec((N_EXPERTS, 1), lambda i: (0, 0))],
        out_specs=[slot, slot, slot, pl.BlockSpec((N_EXPERTS, LANES), lambda i: (0, 0))],
        out_shape=[jax.ShapeDtypeStruct((TOP_K, m), I32), jax.ShapeDtypeStruct((TOP_K, m), F32),
                   jax.ShapeDtypeStruct((TOP_K, m), I32), jax.ShapeDtypeStruct((N_EXPERTS, LANES), F32)],
        scratch_shapes=[pltpu.VMEM((N_EXPERTS, LANES), F32)],
        compiler_params=_cparams(1),
        name="moe_router",
    )(x, wr_t, bias)


def _positions_kernel(offs_ref, ti_ref, rk_ref, o_ref):
    ti = ti_ref[...]
    acc = rk_ref[...]
    for e in range(N_EXPERTS):
        acc = acc + jnp.where(ti == e, offs_ref[e], 0)
    o_ref[...] = acc


def _positions(offs, top_i, rank, tm):
    m = top_i.shape[1]
    slot = pl.BlockSpec((TOP_K, tm), lambda i, offs_ref: (0, i))
    return pl.pallas_call(
        _positions_kernel,
        grid_spec=pltpu.PrefetchScalarGridSpec(
            num_scalar_prefetch=1, grid=(m // tm,), in_specs=[slot, slot],
            out_specs=pl.BlockSpec((None, TOP_K, tm), lambda i, offs_ref: (i, 0, 0))),
        out_shape=jax.ShapeDtypeStruct((m // tm, TOP_K, tm), I32),
        compiler_params=_cparams(1),
        name="moe_positions",
    )(offs, top_i, rank)


def _dispatch_kernel(pos_hbm, x_ref, xs_hbm, pos_smem, sem_idx, sem, *, tm):
    i = pl.program_id(0)
    cp = pltpu.make_async_copy(pos_hbm.at[i], pos_smem, sem_idx)
    cp.start()
    cp.wait()

    def row_copy(t, kk):
        return pltpu.make_async_copy(x_ref.at[pl.ds(t, 1), :], xs_hbm.at[pl.ds(pos_smem[kk, t], 1), :], sem)

    def issue(g, carry):
        base = pl.multiple_of(g * ROW_GROUP, ROW_GROUP)
        for u in range(ROW_GROUP):
            for kk in range(TOP_K):
                row_copy(base + u, kk).start()
        return carry

    lax.fori_loop(0, tm // ROW_GROUP, issue, 0)
    for kk in range(TOP_K):
        pltpu.make_async_copy(x_ref, xs_hbm.at[pl.ds(0, tm), :], sem).wait()


def _dispatch(pos_tiles, x_pk, tm):
    n_tiles = pos_tiles.shape[0]
    m = x_pk.shape[0]
    return pl.pallas_call(
        functools.partial(_dispatch_kernel, tm=tm),
        grid=(n_tiles,),
        in_specs=[pl.BlockSpec(memory_space=pl.ANY),
                  pl.BlockSpec((tm, PACK_W), lambda i: (i, 0))],
        out_specs=pl.BlockSpec(memory_space=pl.ANY),
        out_shape=jax.ShapeDtypeStruct((m * TOP_K, PACK_W), U32),
        scratch_shapes=[pltpu.SMEM((TOP_K, tm), I32), pltpu.SemaphoreType.DMA, pltpu.SemaphoreType.DMA],
        compiler_params=_cparams(1),
        name="moe_dispatch",
    )(pos_tiles, x_pk)


def _gmm_kernel(vt_ref, ve_ref, vlo_ref, vhi_ref, vfirst_ref, vnew_ref, x_ref, wg_ref, wu_ref, wd_ref, o_ref,
                wg_b, wu_b, wd_b):
    vi = pl.program_id(0)
    lo = vlo_ref[vi]
    hi = vhi_ref[vi]
    first = vfirst_ref[vi]

    @pl.when(vnew_ref[vi] == 1)
    def _():
        wg_b[...] = wg_ref[...].astype(BF16)
        wu_b[...] = wu_ref[...].astype(BF16)
        wd_b[...] = wd_ref[...].astype(BF16)

    @pl.when(hi > lo)
    def _():
        n_half = 2
        hr = x_ref.shape[0] // n_half
        halves = [slice(s * hr, (s + 1) * hr) for s in range(n_half)]
        xs = [_unpack_rows(x_ref[sl, :]) for sl in halves]
        gate = [_dg(xa, wg_b[0:PACK_W, :]) + _dg(xb, wg_b[PACK_W:, :]) for xa, xb in xs]
        up = [_dg(xa, wu_b[0:PACK_W, :]) + _dg(xb, wu_b[PACK_W:, :]) for xa, xb in xs]
        row = lax.broadcasted_iota(I32, (hr, 1), 0)
        h = [jnp.where((row + s * hr >= lo) & (row + s * hr < hi), g * _sigmoid(g) * u, 0.0).astype(BF16)
             for s, (g, u) in enumerate(zip(gate, up))]
        y = [_dg(hh, wd_b[...]) for hh in h]

        @pl.when(first == 1)
        def _():
            for sl, yy in zip(halves, y):
                o_ref[sl, :] = yy

        @pl.when(first != 1)
        def _():
            for sl, yy in zip(halves, y):
                o_ref[sl, :] += yy


def _grouped_mlp(visits, xs, w_gate, w_up, w_down, layer, n_visits):
    rows = xs.shape[0]
    tm = GMM_TILE
    by_tile = lambda vi, vt, ve, vlo, vhi, vf, vn: (vt[vi], 0)
    by_expert = lambda vi, vt, ve, vlo, vhi, vf, vn: (layer, ve[vi], 0, 0)
    grid_spec = pltpu.PrefetchScalarGridSpec(
        num_scalar_prefetch=6,
        grid=(n_visits,),
        in_specs=[pl.BlockSpec((tm, PACK_W), by_tile),
                  pl.BlockSpec((None, None, D_MODEL, EXPERT_FF), by_expert),
                  pl.BlockSpec((None, None, D_MODEL, EXPERT_FF), by_expert),
                  pl.BlockSpec((None, None, EXPERT_FF, D_MODEL), by_expert)],
        out_specs=pl.BlockSpec((tm, D_MODEL), by_tile),
        scratch_shapes=[pltpu.VMEM((D_MODEL, EXPERT_FF), BF16), pltpu.VMEM((D_MODEL, EXPERT_FF), BF16),
                        pltpu.VMEM((EXPERT_FF, D_MODEL), BF16)],
    )
    return pl.pallas_call(
        _gmm_kernel,
        grid_spec=grid_spec,
        out_shape=jax.ShapeDtypeStruct((rows, D_MODEL), F32),
        compiler_params=_cparams(1),
        name="moe_grouped_mlp",
    )(*visits, xs, w_gate, w_up, w_down)


def _visit_list(counts, n_rows, tm, n_visits):
    ends = jnp.cumsum(counts)
    offs = ends - counts
    first_tile = offs // tm
    last_tile = jnp.where(counts > 0, (ends - 1) // tm, first_tile - 1)
    n_t = last_tile - first_tile + 1
    v_end = jnp.cumsum(n_t)
    v_start = v_end - n_t
    total = v_end[-1]
    vi = jnp.arange(n_visits, dtype=I32)
    e = jnp.minimum(jnp.sum(v_end[None, :] <= vi[:, None], axis=1), N_EXPERTS - 1).astype(I32)
    active = vi < total
    last_e = jnp.max(jnp.where(counts > 0, jnp.arange(N_EXPERTS, dtype=I32), 0))
    e = jnp.where(active, e, last_e).astype(I32)
    is_e = e[:, None] == jnp.arange(N_EXPERTS, dtype=I32)[None, :]
    pick = lambda table: jnp.sum(jnp.where(is_e, table[None, :], 0), axis=1)
    tile = pick(first_tile) + (vi - pick(v_start))
    tile = jnp.where(active, tile, n_rows // tm - 1).astype(I32)
    lo = jnp.where(active, jnp.maximum(pick(offs), tile * tm) - tile * tm, 0).astype(I32)
    hi = jnp.where(active, jnp.minimum(pick(ends), (tile + 1) * tm) - tile * tm, 0).astype(I32)
    prev_tile = jnp.concatenate([jnp.full((1,), -1, I32), tile[:-1]])
    first = (active & (tile != prev_tile)).astype(I32)
    prev_e = jnp.concatenate([jnp.full((1,), -1, I32), e[:-1]])
    new_e = (e != prev_e).astype(I32)
    return tile, e, lo, hi, first, new_e


def _moe_finish_kernel(pos_hbm, ys_hbm, w_ref, xpk_ref, x_ref, wg_ref, wu_ref, wd_ref, g_ref, b_ref,
                       o_ref, obf_ref, pos_smem, buf, sem_idx, sem, *, tm):
    i = pl.program_id(0)
    cp_pos = pltpu.make_async_copy(pos_hbm.at[i], pos_smem, sem_idx)
    cp_pos.start()
    cp_pos.wait()

    def row_copy(t, kk):
        return pltpu.make_async_copy(ys_hbm.at[pl.ds(pos_smem[kk, t], 1), :], buf.at[kk, pl.ds(t, 1), :], sem)

    def issue(g, carry):
        base = pl.multiple_of(g * ROW_GROUP, ROW_GROUP)
        for u in range(ROW_GROUP):
            for kk in range(TOP_K):
                row_copy(base + u, kk).start()
        return carry

    lax.fori_loop(0, tm // ROW_GROUP, issue, 0)

    xa, xb = _unpack_rows(xpk_ref[...])
    gate = _dg(xa, wg_ref[0:PACK_W, :]) + _dg(xb, wg_ref[PACK_W:, :])
    up = _dg(xa, wu_ref[0:PACK_W, :]) + _dg(xb, wu_ref[PACK_W:, :])
    h = gate * _sigmoid(gate) * up
    y = DEEPNORM_ALPHA * x_ref[...] + _dg(h.astype(BF16), wd_ref[...])

    pltpu.make_async_copy(buf, buf, sem).wait()
    w = w_ref[...]
    routed = w[:, 0:1] * buf[0]
    for kk in range(1, TOP_K):
        routed = routed + w[:, kk:kk + 1] * buf[kk]
    out = _layer_norm(y + routed, g_ref[...], b_ref[...])
    o_ref[...] = out
    obf_ref[...] = out.astype(BF16)


def _moe_finish(pos_tiles, ys, w_tok, x_pk, x, w_gate, w_up, w_down, g, b, tm):
    m = x.shape[0]
    row = lambda n: pl.BlockSpec((tm, n), lambda i: (i, 0))
    vec = pl.BlockSpec((1, D_MODEL), lambda i: (0, 0))
    return pl.pallas_call(
        functools.partial(_moe_finish_kernel, tm=tm),
        grid=(m // tm,),
        in_specs=[pl.BlockSpec(memory_space=pl.ANY), pl.BlockSpec(memory_space=pl.ANY),
                  row(TOP_K), row(PACK_W), row(D_MODEL),
                  pl.BlockSpec((D_MODEL, SHARED_FF), lambda i: (0, 0)),
                  pl.BlockSpec((D_MODEL, SHARED_FF), lambda i: (0, 0)),
                  pl.BlockSpec((SHARED_FF, D_MODEL), lambda i: (0, 0)), vec, vec],
        out_specs=[row(D_MODEL), row(D_MODEL)],
        out_shape=[jax.ShapeDtypeStruct((m, D_MODEL), F32), jax.ShapeDtypeStruct((m, D_MODEL), BF16)],
        scratch_shapes=[pltpu.SMEM((TOP_K, tm), I32), pltpu.VMEM((TOP_K, tm, D_MODEL), F32),
                        pltpu.SemaphoreType.DMA, pltpu.SemaphoreType.DMA],
        compiler_params=_cparams(1),
        name="moe_finish",
    )(pos_tiles, ys, w_tok, x_pk, x, w_gate, w_up, w_down, g, b)


def _pad_cols(w, n):
    return jnp.pad(w, ((0, 0), (0, n - w.shape[1])))


def _gate_lora_weight(w_in_l):
    lora = COL_RKV + 3 * RWKV_W
    parts = [w_in_l[:, lora + DECAY_LORA + AAA_LORA + GATE_LORA:],
             _pad_cols(w_in_l[:, lora:lora + DECAY_LORA], LORA_PAD),
             _pad_cols(w_in_l[:, lora + DECAY_LORA:lora + DECAY_LORA + AAA_LORA], LORA_PAD),
             w_in_l[:, lora + DECAY_LORA + AAA_LORA:lora + DECAY_LORA + AAA_LORA + GATE_LORA]]
    return jnp.concatenate(parts, axis=1).astype(BF16)


def _rwkv_mix_vectors(mix_l):
    lora = 3 * RWKV_W
    parts = [jnp.pad(mix_l[lora:lora + DECAY_LORA], (0, LORA_PAD - DECAY_LORA)),
             jnp.pad(mix_l[lora + DECAY_LORA:lora + DECAY_LORA + AAA_LORA], (0, LORA_PAD - AAA_LORA)),
             mix_l[lora + DECAY_LORA + AAA_LORA:]]
    return mix_l[None, :lora], jnp.concatenate(parts)[None, :]


def _pad_rows(w, n):
    return jnp.pad(w, ((0, n - w.shape[0]), (0, 0))).astype(BF16)


def kernel(x, w_in, conv_w, rwkv_mix, rwkv_w0, rwkv_w2, rwkv_a0, rwkv_a2, rwkv_g2, rwkv_kk, rwkv_ka, rwkv_rk,
           rwkv_lnx_g, rwkv_lnx_b, w_branch, w_o, ln1_g, ln1_b, w_router, router_bias, w_e_gate, w_e_up,
           w_e_down, w_s_gate, w_s_up, w_s_down, ln2_g, ln2_b):
    bsz, seq, d = x.shape
    n_tok = bsz * seq
    n_rows = n_tok * TOP_K
    n_visits = n_rows // GMM_TILE + N_EXPERTS
    io_tile = 256
    xf = x.reshape(n_tok, d)
    xb = xf.astype(BF16)
    row1 = lambda v: v[None, :]
    for l in range(DEPTH):
        p_conv = _matmul_cols_f32w(xb, w_in, l, COL_CONV, 3 * BRANCH_W, 1024, 1024, F32)
        p_att = _matmul_cols_f32w(xb, w_in, l, COL_ATT, 3 * BRANCH_W, 1024, 1024, BF16)
        p_rkv = _matmul_cols_f32w(xb, w_in, l, COL_RKV, 3 * RWKV_W, 1024, 1024, F32)
        w_gl = _gate_lora_weight(w_in[l])
        p_gate = _matmul_cols(xb, w_gl, GL_GATE, 3 * D_MODEL, 1024, 1024, F32)
        p_lora = _matmul_cols(xb, w_gl, GL_LORA, LORA_COLS, 512, 1024, F32)

        y_conv = _short_conv(p_conv, conv_w[l], bsz, seq)
        y_att = _stick_breaking(p_att, bsz, seq)
        y_rwkv = _rwkv_mixer(p_rkv, p_lora, *_rwkv_mix_vectors(rwkv_mix[l]), row1(rwkv_w0[l]),
                             _pad_rows(rwkv_w2[l], LORA_PAD), row1(rwkv_a0[l]), _pad_rows(rwkv_a2[l], LORA_PAD),
                             rwkv_g2[l].astype(BF16), row1(rwkv_kk[l]), row1(rwkv_ka[l]),
                             rwkv_rk[l].reshape(1, RWKV_W), row1(rwkv_lnx_g[l]), row1(rwkv_lnx_b[l]), bsz, seq)

        merged = _merge(y_conv.reshape(n_tok, BRANCH_W), y_att.reshape(n_tok, BRANCH_W),
                        y_rwkv.reshape(n_tok, BRANCH_W), w_branch[l].astype(BF16), p_gate)
        x1, x1_pk = _oproj_ln(merged, w_o[l].astype(BF16), xf, row1(ln1_g[l]), row1(ln1_b[l]))

        top_i, top_w, rank, cnt = _router(x1, w_router[l].T, router_bias[l][:, None])
        counts = cnt[:, 0].astype(I32)
        offs = jnp.cumsum(counts) - counts
        pos_tiles = _positions(offs, top_i, rank, io_tile)
        xs = _dispatch(pos_tiles, x1_pk, io_tile)
        visits = _visit_list(counts, n_rows, GMM_TILE, n_visits)
        ys = _grouped_mlp(visits, xs, w_e_gate, w_e_up, w_e_down, l, n_visits)
        xf, xb = _moe_finish(pos_tiles, ys, top_w.T, x1_pk, x1, w_s_gate[l].astype(BF16),
                             w_s_up[l].astype(BF16), w_s_down[l].astype(BF16), row1(ln2_g[l]), row1(ln2_b[l]),
                             io_tile)
    return xf.reshape(bsz, seq, d)
```

```python
import functools

import jax
import jax.numpy as jnp
from jax import lax
from jax.experimental import pallas as pl
from jax.experimental.pallas import tpu as pltpu

F32 = jnp.float32
BF16 = jnp.bfloat16
I32 = jnp.int32
U32 = jnp.uint32

D_MODEL = 2048
DEPTH = 2
BRANCH_W = 1024
CONV_K = 3
ATT_HEADS = 8
ATT_HD = 128
RWKV_HEADS = 16
RWKV_HD = 64
RWKV_W = RWKV_HEADS * RWKV_HD
DECAY_LORA = 96
AAA_LORA = 96
GATE_LORA = 256
LORA_PAD = 128
N_EXPERTS = 64
TOP_K = 8
N_GROUPS = 8
GROUP_SIZE = N_EXPERTS // N_GROUPS
TOPK_GROUPS = 4
EXPERT_FF = 512
SHARED_FF = 512
ROUTED_SCALE = 2.5
LN_EPS = 1e-5
GN_EPS = 64e-5
DEEPNORM_ALPHA = (2 * DEPTH) ** 0.25

COL_CONV = 0
COL_ATT = 3 * BRANCH_W
COL_GATE = 6 * BRANCH_W
COL_RWKV = COL_GATE + 3 * D_MODEL
RWKV_COLS = 3 * RWKV_W + 2 * LORA_PAD + GATE_LORA
IN_COLS_PAD = COL_RWKV + RWKV_COLS

RWKV_CHUNK = 64
RWKV_GROUP = 4
RWKV_GW = RWKV_GROUP * RWKV_HD
RWKV_GROUPS_PER_STEP = 2
ATT_TILE = 512
ATT_KEY_TILE = 256
ATT_ROW_CHUNK = 32
ATT_UNDERFLOW_LOG = -105.0
GMM_TILE = 256
LANES = 128
ROW_GROUP = 8
PACK_W = D_MODEL // 2

VMEM_LIMIT = 56 * 1024 * 1024

NN = ((1,), (0,))
NT = ((1,), (1,))


def _cparams(n_axes):
    return pltpu.CompilerParams(dimension_semantics=("arbitrary",) * n_axes,
                                vmem_limit_bytes=VMEM_LIMIT)


def _dg(a, b, dims=NN):
    return lax.dot_general(a, b, (dims, ((), ())), preferred_element_type=F32)


def _split2(x):
    hi = x.astype(BF16)
    lo = (x - hi.astype(F32)).astype(BF16)
    return hi, lo


def _split3(x):
    hi = x.astype(BF16)
    r1 = x - hi.astype(F32)
    mid = r1.astype(BF16)
    lo = (r1 - mid.astype(F32)).astype(BF16)
    return hi, mid, lo


def _mm3(a, b, dims=NN):
    ah, al = _split2(a)
    bh, bl = _split2(b)
    return _dg(ah, bh, dims) + _dg(ah, bl, dims) + _dg(al, bh, dims)


def _sigmoid(x):
    return 1.0 / (1.0 + jnp.exp(-x))


def _softplus(x):
    return jnp.maximum(x, 0.0) + jnp.log(1.0 + jnp.exp(-jnp.abs(x)))


def _mm_kernel(x_ref, w_ref, o_ref):
    o_ref[...] = _dg(x_ref[...], w_ref[...]).astype(o_ref.dtype)


def _matmul_cols(x, w, col_off, ncols, tn, tm, out_dtype):
    m, k = x.shape
    off = col_off // tn
    return pl.pallas_call(
        _mm_kernel,
        grid=(ncols // tn, m // tm),
        in_specs=[pl.BlockSpec((tm, k), lambda j, i: (i, 0)),
                  pl.BlockSpec((k, tn), lambda j, i: (0, off + j))],
        out_specs=pl.BlockSpec((tm, tn), lambda j, i: (i, j)),
        out_shape=jax.ShapeDtypeStruct((m, ncols), out_dtype),
        compiler_params=_cparams(2),
        name="in_proj",
    )(x, w)


def _conv_kernel(gb_ref, gc_ref, hc_ref, gch_ref, hch_ref, cw_ref, o_ref):
    i = pl.program_id(1)
    u = gc_ref[...] * hc_ref[...]
    uh = gch_ref[...] * hch_ref[...]
    uh = jnp.where(i > 0, uh, 0.0)
    row = lax.broadcasted_iota(I32, u.shape, 0)
    u1 = jnp.where(row == 0, uh[7:8, :], pltpu.roll(u, 1, axis=0))
    u2 = pltpu.roll(u, 2, axis=0)
    u2 = jnp.where(row == 0, uh[6:7, :], jnp.where(row == 1, uh[7:8, :], u2))
    cw = cw_ref[...]
    y = gb_ref[...] * (cw[2:3, :] * u + cw[1:2, :] * u1 + cw[0:1, :] * u2)
    o_ref[...] = y.astype(o_ref.dtype)


def _short_conv(p_conv, conv_w, bsz, seq, ts=512):
    p3 = p_conv.reshape(bsz, seq, 3 * BRANCH_W)
    hb = ts // 8
    halo = lambda c: pl.BlockSpec((None, 8, BRANCH_W),
                                  lambda b, i: (b, jnp.maximum(i * hb - 1, 0), c))
    main = lambda c: pl.BlockSpec((None, ts, BRANCH_W), lambda b, i: (b, i, c))
    return pl.pallas_call(
        _conv_kernel,
        grid=(bsz, seq // ts),
        in_specs=[main(0), main(1), main(2), halo(1), halo(2),
                  pl.BlockSpec((CONV_K, BRANCH_W), lambda b, i: (0, 0))],
        out_specs=pl.BlockSpec((None, ts, BRANCH_W), lambda b, i: (b, i, 0)),
        out_shape=jax.ShapeDtypeStruct((bsz, seq, BRANCH_W), BF16),
        compiler_params=_cparams(2),
        name="short_conv",
    )(p3, p3, p3, p3, p3, conv_w)


def _attn_kernel(q_ref, k_ref, v_ref, o_ref, z_s, lb_s, hl_s, tail_s, w_s, c_s, rs_s, acc_s, *, tq, tk, scale):
    i = pl.program_id(2)
    n_sub = tq // tk
    rc = ATT_ROW_CHUNK
    suffix = (lax.broadcasted_iota(I32, (tk, tk), 0)
              > lax.broadcasted_iota(I32, (tk, tk), 1)).astype(BF16)
    suffix2 = jnp.concatenate([suffix, suffix], axis=0)
    r_io = lax.broadcasted_iota(I32, (rc, tk), 0)
    c_io = lax.broadcasted_iota(I32, (rc, tk), 1)
    acc_s[...] = jnp.zeros_like(acc_s)
    c_s[...] = jnp.zeros_like(c_s)
    n_chunks = tq // rc

    def key_rows(j):
        return pl.ds(pl.multiple_of(j * tk, tk), tk)

    def logits(s, diag_off):
        def step(r, carry):
            rows = pl.ds(pl.multiple_of(r * rc, rc), rc)
            z = z_s[s, rows, :] * scale
            log_beta = jnp.minimum(z, 0.0) - jnp.log(1.0 + jnp.exp(-jnp.abs(z)))
            log_1mb = log_beta - z
            if diag_off is not None:
                causal = (c_io + diag_off * tk) < (r_io + r * rc)
                log_1mb = jnp.where(causal, log_1mb, 0.0)
            hi, lo = _split2(log_1mb)
            hl_s[s, rows, 0:tk] = hi
            hl_s[s, rows, tk:2 * tk] = lo
            lb_s[s, rows, :] = log_beta
            rs_s[s, rows, :] = jnp.sum(log_1mb, axis=1, keepdims=True)
            return carry
        lax.fori_loop(0, n_chunks, step, 0, unroll=True)

    def weights(s, diag_off):
        def step(r, carry):
            rows = pl.ds(pl.multiple_of(r * rc, rc), rc)
            w = jnp.exp(lb_s[s, rows, :] + tail_s[s, rows, :] + c_s[rows, :])
            if diag_off is not None:
                causal = (c_io + diag_off * tk) < (r_io + r * rc)
                w = jnp.where(causal, w, 0.0)
            w_s[s, rows, :] = w.astype(BF16)
            return carry
        lax.fori_loop(0, n_chunks, step, 0, unroll=True)

    def block_pair(j_hi, diag):
        js = (j_hi, j_hi - 1)
        offs = (1, 0) if diag else (None, None)
        for s in range(2):
            z_s[s] = _dg(q_ref[...], k_ref[key_rows(js[s]), :], NT)
        for s in range(2):
            logits(s, offs[s])
        for s in range(2):
            tail_s[s] = _dg(hl_s[s], suffix2)
        for s in range(2):
            weights(s, offs[s])
            c_s[...] += rs_s[s]
        for s in range(2):
            acc_s[...] += _dg(w_s[s], v_ref[key_rows(js[s]), :])

    block_pair(n_sub * i + 1, True)

    def more(state):
        jj, c_max = state
        return jnp.logical_and(jj < i, c_max > ATT_UNDERFLOW_LOG)

    def below(state):
        jj, _ = state
        block_pair(n_sub * i - 1 - 2 * jj, False)
        return jj + 1, jnp.max(c_s[...])

    lax.while_loop(more, below, (jnp.int32(0), jnp.max(c_s[...])))
    o_ref[...] = acc_s[...].astype(o_ref.dtype)


def _stick_breaking(p_att, bsz, seq):
    p3 = p_att.reshape(bsz, seq, 3 * BRANCH_W)
    tq = ATT_TILE
    tk = ATT_KEY_TILE
    assert tq == 2 * tk, "the attention kernel walks the keys in pairs of blocks"
    kern = functools.partial(_attn_kernel, tq=tq, tk=tk, scale=ATT_HD ** -0.5)
    return pl.pallas_call(
        kern,
        grid=(bsz, ATT_HEADS, seq // tq),
        in_specs=[pl.BlockSpec((None, tq, ATT_HD), lambda b, h, i: (b, i, h)),
                  pl.BlockSpec((None, seq, ATT_HD), lambda b, h, i: (b, 0, ATT_HEADS + h)),
                  pl.BlockSpec((None, seq, ATT_HD), lambda b, h, i: (b, 0, 2 * ATT_HEADS + h))],
        out_specs=pl.BlockSpec((None, tq, ATT_HD), lambda b, h, i: (b, i, h)),
        out_shape=jax.ShapeDtypeStruct((bsz, seq, BRANCH_W), BF16),
        scratch_shapes=[pltpu.VMEM((2, tq, tk), F32), pltpu.VMEM((2, tq, tk), F32),
                        pltpu.VMEM((2, tq, 2 * tk), BF16), pltpu.VMEM((2, tq, tk), F32),
                        pltpu.VMEM((2, tq, tk), BF16), pltpu.VMEM((tq, 1), F32),
                        pltpu.VMEM((2, tq, 1), F32), pltpu.VMEM((tq, ATT_HD), F32)],
        compiler_params=_cparams(3),
        name="stick_breaking_attention",
    )(p3, p3, p3)


def _head_sum(x, ones_bd):
    hi, lo = _split2(x)
    outs = []
    for c in range(x.shape[1] // RWKV_GW):
        sl = slice(RWKV_GW * c, RWKV_GW * (c + 1))
        outs.append(_dg(hi[:, sl], ones_bd) + _dg(lo[:, sl], ones_bd))
    return outs[0] if len(outs) == 1 else jnp.concatenate(outs, axis=1)


def _head_ones():
    r = lax.broadcasted_iota(I32, (RWKV_GW, RWKV_GW), 0) // RWKV_HD
    c = lax.broadcasted_iota(I32, (RWKV_GW, RWKV_GW), 1) // RWKV_HD
    return (r == c).astype(BF16)


RWKV_LEVELS = RWKV_CHUNK.bit_length() - 1


def _rwkv_masks():
    r = jnp.arange(RWKV_GW)[:, None]
    c = jnp.arange(RWKV_GW)[None, :]
    bd = (r // RWKV_CHUNK) == (c // RWKV_HD)
    planes = [bd, bd & (r > c), bd & (r >= c)]
    for m in range(RWKV_LEVELS):
        s = 1 << m
        planes.append((r // (2 * s) == c // (2 * s)) & (r % (2 * s) >= s) & (c % (2 * s) < s))
    return jnp.stack(planes).astype(F32), bd.astype(BF16)


def _rwkv_scan_kernel(pr_ref, pk_ref, pv_ref, pl_ref, mr_ref, mk_ref, mv_ref, ml_ref, w0_ref, w2_ref, a0_ref,
                      a2_ref, g2_ref, kk_ref, ka_ref, rk_ref, lng_ref, lnb_ref, msk_ref, bdb_ref,
                      o_ref, s_ref, prev_ref):
    @pl.when(pl.program_id(1) == 0)
    def _():
        s_ref[...] = jnp.zeros_like(s_ref)
        prev_ref[...] = jnp.zeros_like(prev_ref)

    c_len = RWKV_CHUNK
    gw = RWKV_GW
    n_b = pr_ref.shape[0]
    chains = [(bb, gi) for bb in range(n_b) for gi in range(pr_ref.shape[2] // gw)]
    first_row = lax.broadcasted_iota(I32, (c_len, 1), 0) == 0

    def smap(f, *lists):
        return [f(*xs) for xs in zip(*lists)]

    def shifted_mix(ref, mix_ref, slot):
        outs = []
        for bb in range(n_b):
            x = ref[bb]
            shifted = jnp.where(first_row, prev_ref[bb, slot], pltpu.roll(x, 1, axis=0))
            prev_ref[bb, slot] = x[c_len - 1:c_len, :]
            outs.append(x + (shifted - x) * mix_ref[...])
        return outs

    ps_r, ps_k, ps_v = shifted_mix(pr_ref, mr_ref, 0), shifted_mix(pk_ref, mk_ref, 1), shifted_mix(pv_ref, mv_ref, 2)
    ps_l = shifted_mix(pl_ref, ml_ref, 3)
    lora_w = [jnp.tanh(x[:, 0:LORA_PAD]).astype(BF16) for x in ps_l]
    lora_a = [x[:, LORA_PAD:2 * LORA_PAD].astype(BF16) for x in ps_l]
    lora_g = [_sigmoid(x[:, 2 * LORA_PAD:]).astype(BF16) for x in ps_l]
    ones_hd = _head_ones()
    r, lw, k, v, a, b, g = [], [], [], [], [], [], []
    for bb, gi in chains:
        sl = slice(gi * gw, (gi + 1) * gw)
        w_log = -_softplus(-(w0_ref[:, sl] + _dg(lora_w[bb], w2_ref[:, sl]))) - 0.5
        rate = _sigmoid(a0_ref[:, sl] + _dg(lora_a[bb], a2_ref[:, sl]))
        kk = ps_k[bb][:, sl] * kk_ref[:, sl]
        kk = kk / jnp.maximum(jnp.sqrt(_head_sum(kk * kk, ones_hd)), 1e-12)
        r.append(ps_r[bb][:, sl])
        lw.append(-jnp.exp(w_log))
        k.append(ps_k[bb][:, sl] * (1.0 + (rate - 1.0) * ka_ref[:, sl]))
        v.append(ps_v[bb][:, sl])
        a.append(-kk)
        b.append(kk * rate)
        g.append(_dg(lora_g[bb], g2_ref[:, sl]))

    def par(ref):
        return [ref[:, gi * gw:(gi + 1) * gw] for _, gi in chains]

    def tile_heads(x):
        return jnp.concatenate([x.astype(BF16)] * RWKV_GROUP, axis=0)

    def unstack(z):
        out = z[0:c_len]
        for h in range(1, RWKV_GROUP):
            out = out + z[h * c_len:(h + 1) * c_len]
        return out

    tril_c =(lax.broadcasted_iota(I32, (c_len, c_len), 0)
              >= lax.broadcasted_iota(I32, (c_len, c_len), 1)).astype(BF16)

    def cum_decay(x):
        hi, mid, lo = _split3(x)
        return _dg(tril_c, hi) + _dg(tril_c, mid) + _dg(tril_c, lo)

    cum = smap(cum_decay, lw)
    cum_end = smap(lambda c: c[c_len - 1:c_len, :], cum)
    e_inv = smap(lambda c: jnp.exp(-c), cum)
    e_rem = smap(lambda ce, c: jnp.exp(ce - c), cum_end, cum)

    bd = msk_ref[0]
    strict = msk_ref[1]
    incl = msk_ref[2]
    head_b = bdb_ref[...]
    a_s = smap(lambda x, c, l: tile_heads(x * jnp.exp(c - l)) * head_b, a, cum, lw)
    r_s = smap(lambda x, c: tile_heads(x * jnp.exp(c)) * head_b, r, cum)
    b_x = smap(lambda x, e: tile_heads(x * e), b, e_inv)
    k_x = smap(lambda x, e: tile_heads(x * e), k, e_inv)
    v_x = smap(tile_heads, v)

    a_ab = smap(lambda p, q: _dg(p, q, NT) * strict, a_s, b_x)
    a_ak = smap(lambda p, q: (_dg(p, q, NT) * strict).astype(BF16), a_s, k_x)
    a_rb = smap(lambda p, q: (_dg(p, q, NT) * incl).astype(BF16), r_s, b_x)
    a_rk = smap(lambda p, q: (_dg(p, q, NT) * incl).astype(BF16), r_s, k_x)

    inv = smap(lambda m: (incl - strict) + m * msk_ref[3], a_ab)
    for lvl in range(1, RWKV_LEVELS):
        inv_b = smap(lambda t: t.astype(BF16), inv)
        low = smap(lambda m: (m * msk_ref[3 + lvl]).astype(BF16), a_ab)
        mid = smap(lambda t, lo_: _dg(t, lo_).astype(BF16), inv_b, low)
        inv = smap(lambda t, md, tb: t + _dg(md, tb), inv, mid, inv_b)
    inv_b = smap(lambda t: t.astype(BF16), inv)

    s0 = [s_ref[bb, gi] for bb, gi in chains]
    s0_b = smap(lambda s: s.astype(BF16), s0)
    rhs = smap(lambda p, s, m, vv: (_dg(p, s, NT) + _dg(m, vv)).astype(BF16), a_s, s0_b, a_ak, v_x)
    u = smap(_dg, inv_b, rhs)
    u_b = smap(lambda t: t.astype(BF16), u)
    y = smap(lambda p, s, m1, ub, m2, vv: unstack((_dg(p, s, NT) + _dg(m1, ub) + _dg(m2, vv)) * bd),
             r_s, s0_b, a_rb, u_b, a_rk, v_x)
    uv_t = smap(lambda uu, vv: jnp.concatenate([unstack(uu * bd), vv], axis=0).T.astype(BF16), u, v)
    bk_h = smap(lambda bb_, kk_, e: jnp.concatenate([bb_ * e, kk_ * e], axis=0).astype(BF16), b, k, e_rem)
    s_new = smap(lambda s, ce, p, q: s * jnp.exp(ce) + _dg(p, q) * bd, s0, cum_end, uv_t, bk_h)
    for (bb, gi), s in zip(chains, s_new):
        s_ref[bb, gi] = s

    inv_n = 1.0 / RWKV_HD
    d = smap(lambda yy: yy - _head_sum(yy, ones_hd) * inv_n, y)
    var = smap(lambda dd: _head_sum(dd * dd, ones_hd) * inv_n, d)
    yn = smap(lambda dd, vr, gg, bb_: dd * lax.rsqrt(vr + GN_EPS) * gg + bb_, d, var, par(lng_ref), par(lnb_ref))
    bonus = smap(lambda rr, kk_, rk, vv: _head_sum(rr * kk_ * rk, ones_hd) * vv, r, k, par(rk_ref), v)
    for (bb, gi), yy, bo, gg in zip(chains, yn, bonus, g):
        o_ref[bb, :, gi * gw:(gi + 1) * gw] = ((yy + bo) * gg).astype(o_ref.dtype)


def _rwkv_mixer(p_rwkv, mix, w0, w2, a0, a2, g2, k_k, k_a, r_k, lnx_g, lnx_b, bsz, seq):
    p3 = p_rwkv.reshape(bsz, seq, RWKV_COLS)
    n_groups = RWKV_W // RWKV_GW
    gps = RWKV_GROUPS_PER_STEP
    bw = gps * RWKV_GW
    n_cb = RWKV_W // bw
    lora_cb = 3 * n_cb
    assert RWKV_COLS == (lora_cb + 1) * bw
    tok = lambda cb0: pl.BlockSpec((bsz, RWKV_CHUNK, bw), lambda gg, i: (0, i, cb0 + gg))
    tok_lora = pl.BlockSpec((bsz, RWKV_CHUNK, bw), lambda gg, i: (0, i, lora_cb))
    mixv = lambda cb0: pl.BlockSpec((1, bw), lambda gg, i: (0, cb0 + gg))
    mix_lora = pl.BlockSpec((1, bw), lambda gg, i: (0, lora_cb))
    vec = pl.BlockSpec((1, bw), lambda gg, i: (0, gg))
    mat = lambda rows: pl.BlockSpec((rows, bw), lambda gg, i: (0, gg))
    masks, head_mask = _rwkv_masks()
    return pl.pallas_call(
        _rwkv_scan_kernel,
        grid=(n_groups // gps, seq // RWKV_CHUNK),
        in_specs=[tok(0), tok(n_cb), tok(2 * n_cb), tok_lora,
                  mixv(0), mixv(n_cb), mixv(2 * n_cb), mix_lora,
                  vec, mat(LORA_PAD), vec, mat(LORA_PAD), mat(GATE_LORA), vec, vec, vec, vec, vec,
                  pl.BlockSpec((3 + RWKV_LEVELS, RWKV_GW, RWKV_GW), lambda gg, i: (0, 0, 0)),
                  pl.BlockSpec((RWKV_GW, RWKV_GW), lambda gg, i: (0, 0))],
        out_specs=pl.BlockSpec((bsz, RWKV_CHUNK, bw), lambda gg, i: (0, i, gg)),
        out_shape=jax.ShapeDtypeStruct((bsz, seq, RWKV_W), BF16),
        scratch_shapes=[pltpu.VMEM((bsz, gps, RWKV_GW, RWKV_GW), F32),
                        pltpu.VMEM((bsz, 4, 1, bw), F32)],
        compiler_params=_cparams(2),
        name="rwkv_mixer",
    )(p3, p3, p3, p3, mix, mix, mix, mix, w0, w2, a0, a2, g2, k_k, k_a, r_k, lnx_g, lnx_b, masks, head_mask)


def _merge_kernel(yc_ref, ya_ref, yr_ref, wb_ref, g0_ref, g1_ref, g2_ref, o_ref):
    acc = None
    for n, (y_ref, g_ref) in enumerate(((yc_ref, g0_ref), (ya_ref, g1_ref), (yr_ref, g2_ref))):
        term = _sigmoid(g_ref[...]) * _dg(y_ref[...], wb_ref[n])
        acc = term if acc is None else acc + term
    o_ref[...] = acc.astype(o_ref.dtype)


def _merge(y_conv, y_att, y_rwkv, w_branch, p_gate, tm=512, tn=1024):
    m = y_conv.shape[0]
    nj = D_MODEL // tn
    ysp = pl.BlockSpec((tm, BRANCH_W), lambda j, i: (i, 0))
    gsp = lambda n: pl.BlockSpec((tm, tn), lambda j, i: (i, n * nj + j))
    return pl.pallas_call(
        _merge_kernel,
        grid=(nj, m // tm),
        in_specs=[ysp, ysp, ysp, pl.BlockSpec((3, BRANCH_W, tn), lambda j, i: (0, 0, j)),
                  gsp(0), gsp(1), gsp(2)],
        out_specs=pl.BlockSpec((tm, tn), lambda j, i: (i, j)),
        out_shape=jax.ShapeDtypeStruct((m, D_MODEL), BF16),
        compiler_params=_cparams(2),
        name="branch_merge",
    )(y_conv, y_att, y_rwkv, w_branch, p_gate, p_gate, p_gate)


def _layer_norm(y, g, b):
    mu = jnp.mean(y, axis=-1, keepdims=True)
    d = y - mu
    var = jnp.mean(d * d, axis=-1, keepdims=True)
    return d * lax.rsqrt(var + LN_EPS) * g + b


def _pack_rows(x):
    hi = lax.bitcast_convert_type(x[:, :PACK_W].astype(BF16).astype(F32), U32)
    lo = lax.bitcast_convert_type(x[:, PACK_W:].astype(BF16).astype(F32), U32)
    return hi | (lo >> 16)


def _unpack_rows(w):
    hi = lax.bitcast_convert_type(w & jnp.uint32(0xFFFF0000), F32).astype(BF16)
    lo = lax.bitcast_convert_type(w << 16, F32).astype(BF16)
    return hi, lo


def _oproj_ln_kernel(m_ref, w_ref, x_ref, g_ref, b_ref, o_ref, opk_ref):
    y = DEEPNORM_ALPHA * x_ref[...] + _dg(m_ref[...], w_ref[...])
    out = _layer_norm(y, g_ref[...], b_ref[...])
    o_ref[...] = out
    opk_ref[...] = _pack_rows(out)


def _oproj_ln(merged, w_o, x, g, b, tm=256):
    m = x.shape[0]
    row = lambda n: pl.BlockSpec((tm, n), lambda i: (i, 0))
    vec = pl.BlockSpec((1, D_MODEL), lambda i: (0, 0))
    return pl.pallas_call(
        _oproj_ln_kernel,
        grid=(m // tm,),
        in_specs=[row(D_MODEL), pl.BlockSpec((D_MODEL, D_MODEL), lambda i: (0, 0)), row(D_MODEL), vec, vec],
        out_specs=[row(D_MODEL), row(PACK_W)],
        out_shape=[jax.ShapeDtypeStruct((m, D_MODEL), F32), jax.ShapeDtypeStruct((m, PACK_W), U32)],
        compiler_params=_cparams(1),
        name="out_proj_ln",
    )(merged, w_o, x, g, b)


def _first_max(x, idx, n):
    m = jnp.max(x, axis=0, keepdims=True)
    first = jnp.min(jnp.where(x == m, idx, float(n)), axis=0, keepdims=True)
    return m, first


def _router_kernel(x_ref, wr_ref, bias_ref, ti_ref, tw_ref, rk_ref, cnt_ref, run_ref, *, tm):
    @pl.when(pl.program_id(0) == 0)
    def _():
        run_ref[...] = jnp.zeros_like(run_ref)

    xh, xl = _split2(x_ref[...])
    wh, wl = _split2(wr_ref[...])
    logits = _dg(wh, xh, NT) + _dg(wh, xl, NT) + _dg(wl, xh, NT)
    scores = _sigmoid(logits)
    biased = scores + bias_ref[...]
    neg = -jnp.inf

    sub8 = lax.broadcasted_iota(I32, (GROUP_SIZE, tm), 0).astype(F32)
    grp_rows = []
    for gi in range(N_GROUPS):
        blk = biased[gi * GROUP_SIZE:(gi + 1) * GROUP_SIZE, :]
        m1, f1 = _first_max(blk, sub8, GROUP_SIZE)
        m2 = jnp.max(jnp.where(sub8 == f1, neg, blk), axis=0, keepdims=True)
        grp_rows.append(m1 + m2)
    cur = jnp.concatenate(grp_rows, axis=0)
    grp_io = lax.broadcasted_iota(I32, (N_GROUPS, tm), 0).astype(F32)
    keep = jnp.zeros((N_GROUPS, tm), F32)
    for _ in range(TOPK_GROUPS):
        _, f = _first_max(cur, grp_io, N_GROUPS)
        hit = grp_io == f
        keep = jnp.where(hit, 1.0, keep)
        cur = jnp.where(hit, neg, cur)
    exp_keep = jnp.concatenate(
        [jnp.broadcast_to(keep[gi:gi + 1, :], (GROUP_SIZE, tm)) for gi in range(N_GROUPS)], axis=0)
    cand = jnp.where(exp_keep > 0.5, biased, neg)

    exp_io = lax.broadcasted_iota(I32, (N_EXPERTS, tm), 0).astype(F32)
    sel = jnp.zeros((N_EXPERTS, tm), F32)
    idxs, vals = [], []
    for _ in range(TOP_K):
        _, f = _first_max(cand, exp_io, N_EXPERTS)
        hit = exp_io == f
        idxs.append(f)
        vals.append(jnp.sum(jnp.where(hit, scores, 0.0), axis=0, keepdims=True))
        sel = jnp.where(hit, 1.0, sel)
        cand = jnp.where(hit, neg, cand)
    top_s = jnp.concatenate(vals, axis=0)
    ti_ref[...] = jnp.concatenate(idxs, axis=0).astype(I32)
    tw_ref[...] = top_s / jnp.sum(top_s, axis=0, keepdims=True) * ROUTED_SCALE

    before = (lax.broadcasted_iota(I32, (tm, tm), 0) < lax.broadcasted_iota(I32, (tm, tm), 1)).astype(BF16)
    rank_all = _dg(sel.astype(BF16), before) + run_ref[...][:, 0:1]
    ranks = [jnp.sum(jnp.where(exp_io == f, rank_all, 0.0), axis=0, keepdims=True) for f in idxs]
    rk_ref[...] = jnp.concatenate(ranks, axis=0).astype(I32)
    run_ref[...] = run_ref[...] + jnp.sum(sel, axis=1, keepdims=True)
    cnt_ref[...] = run_ref[...]


def _router(x, wr_t, bias, tm=512):
    m = x.shape[0]
    slot = pl.BlockSpec((TOP_K, tm), lambda i: (0, i))
    return pl.pallas_call(
        functools.partial(_router_kernel, tm=tm),
        grid=(m // tm,),
        in_specs=[pl.BlockSpec((tm, D_MODEL), lambda i: (i, 0)),
                  pl.BlockSpec((N_EXPERTS, D_MODEL), lambda i: (0, 0)),
                  pl.BlockSpec((N_EXPERTS, 1), lambda i: (0, 0))],
        out_specs=[slot, slot, slot, pl.BlockSpec((N_EXPERTS, LANES), lambda i: (0, 0))],
        out_shape=[jax.ShapeDtypeStruct((TOP_K, m), I32), jax.ShapeDtypeStruct((TOP_K, m), F32),
                   jax.ShapeDtypeStruct((TOP_K, m), I32), jax.ShapeDtypeStruct((N_EXPERTS, LANES), F32)],
        scratch_shapes=[pltpu.VMEM((N_EXPERTS, LANES), F32)],
        compiler_params=_cparams(1),
        name="moe_router",
    )(x, wr_t, bias)


def _positions_kernel(offs_ref, ti_ref, rk_ref, o_ref):
    ti = ti_ref[...]
    acc = rk_ref[...]
    for e in range(N_EXPERTS):
        acc = acc + jnp.where(ti == e, offs_ref[e], 0)
    o_ref[...] = acc


def _positions(offs, top_i, rank, tm):
    m = top_i.shape[1]
    slot = pl.BlockSpec((TOP_K, tm), lambda i, offs_ref: (0, i))
    return pl.pallas_call(
        _positions_kernel,
        grid_spec=pltpu.PrefetchScalarGridSpec(
            num_scalar_prefetch=1, grid=(m // tm,), in_specs=[slot, slot],
            out_specs=pl.BlockSpec((None, TOP_K, tm), lambda i, offs_ref: (i, 0, 0))),
        out_shape=jax.ShapeDtypeStruct((m // tm, TOP_K, tm), I32),
        compiler_params=_cparams(1),
        name="moe_positions",
    )(offs, top_i, rank)


def _dispatch_kernel(pos_hbm, x_ref, xs_hbm, pos_smem, sem_idx, sem, *, tm):
    i = pl.program_id(0)
    cp = pltpu.make_async_copy(pos_hbm.at[i], pos_smem, sem_idx)
    cp.start()
    cp.wait()

    def row_copy(t, kk):
        return pltpu.make_async_copy(x_ref.at[pl.ds(t, 1), :], xs_hbm.at[pl.ds(pos_smem[kk, t], 1), :], sem)

    def issue(g, carry):
        base = pl.multiple_of(g * ROW_GROUP, ROW_GROUP)
        for u in range(ROW_GROUP):
            for kk in range(TOP_K):
                row_copy(base + u, kk).start()
        return carry

    lax.fori_loop(0, tm // ROW_GROUP, issue, 0)
    for kk in range(TOP_K):
        pltpu.make_async_copy(x_ref, xs_hbm.at[pl.ds(0, tm), :], sem).wait()


def _dispatch(pos_tiles, x_pk, tm):
    n_tiles = pos_tiles.shape[0]
    m = x_pk.shape[0]
    return pl.pallas_call(
        functools.partial(_dispatch_kernel, tm=tm),
        grid=(n_tiles,),
        in_specs=[pl.BlockSpec(memory_space=pl.ANY),
                  pl.BlockSpec((tm, PACK_W), lambda i: (i, 0))],
        out_specs=pl.BlockSpec(memory_space=pl.ANY),
        out_shape=jax.ShapeDtypeStruct((m * TOP_K, PACK_W), U32),
        scratch_shapes=[pltpu.SMEM((TOP_K, tm), I32), pltpu.SemaphoreType.DMA, pltpu.SemaphoreType.DMA],
        compiler_params=_cparams(1),
        name="moe_dispatch",
    )(pos_tiles, x_pk)


def _gmm_kernel(vt_ref, ve_ref, vlo_ref, vhi_ref, vfirst_ref, vnew_ref, vslot_ref, vnext_ref,
                x_ref, wg_hbm, wu_hbm, wd_hbm, o_ref, wg_f, wu_f, wd_f, wg_b, wu_b, wd_b, sem, *, layer):
    vi = pl.program_id(0)
    lo = vlo_ref[vi]
    hi = vhi_ref[vi]
    first = vfirst_ref[vi]
    slot = vslot_ref[vi]

    def weight_copies(e, s):
        return (pltpu.make_async_copy(wg_hbm.at[layer, e], wg_f.at[s], sem.at[s, 0]),
                pltpu.make_async_copy(wu_hbm.at[layer, e], wu_f.at[s], sem.at[s, 1]),
                pltpu.make_async_copy(wd_hbm.at[layer, e], wd_f.at[s], sem.at[s, 2]))

    @pl.when(vi == 0)
    def _():
        for c in weight_copies(ve_ref[0], slot):
            c.start()

    @pl.when(vnew_ref[vi] == 1)
    def _():
        for c in weight_copies(ve_ref[vi], slot):
            c.wait()
        wg_b[...] = wg_f[slot].astype(BF16)
        wu_b[...] = wu_f[slot].astype(BF16)
        wd_b[...] = wd_f[slot].astype(BF16)
        nxt = vnext_ref[vi]

        @pl.when(nxt >= 0)
        def _():
            for c in weight_copies(nxt, 1 - slot):
                c.start()

    @pl.when(hi > lo)
    def _():
        n_half = 2
        hr = x_ref.shape[0] // n_half
        halves = [slice(s * hr, (s + 1) * hr) for s in range(n_half)]
        xs = [_unpack_rows(x_ref[sl, :]) for sl in halves]
        gate = [_dg(xa, wg_b[0:PACK_W, :]) + _dg(xb, wg_b[PACK_W:, :]) for xa, xb in xs]
        up = [_dg(xa, wu_b[0:PACK_W, :]) + _dg(xb, wu_b[PACK_W:, :]) for xa, xb in xs]
        row = lax.broadcasted_iota(I32, (hr, 1), 0)
        h = [jnp.where((row + s * hr >= lo) & (row + s * hr < hi), g * _sigmoid(g) * u, 0.0).astype(BF16)
             for s, (g, u) in enumerate(zip(gate, up))]
        y = [_dg(hh, wd_b[...]) for hh in h]

        @pl.when(first == 1)
        def _():
            for sl, yy in zip(halves, y):
                o_ref[sl, :] = yy

        @pl.when(first != 1)
        def _():
            for sl, yy in zip(halves, y):
                o_ref[sl, :] += yy


def _grouped_mlp(visits, xs, w_gate, w_up, w_down, layer, n_visits):
    rows = xs.shape[0]
    tm = GMM_TILE
    by_tile = lambda vi, vt, *_: (vt[vi], 0)
    grid_spec = pltpu.PrefetchScalarGridSpec(
        num_scalar_prefetch=8,
        grid=(n_visits,),
        in_specs=[pl.BlockSpec((tm, PACK_W), by_tile),
                  pl.BlockSpec(memory_space=pl.ANY), pl.BlockSpec(memory_space=pl.ANY),
                  pl.BlockSpec(memory_space=pl.ANY)],
        out_specs=pl.BlockSpec((tm, D_MODEL), by_tile),
        scratch_shapes=[pltpu.VMEM((2, D_MODEL, EXPERT_FF), F32), pltpu.VMEM((2, D_MODEL, EXPERT_FF), F32),
                        pltpu.VMEM((2, EXPERT_FF, D_MODEL), F32),
                        pltpu.VMEM((D_MODEL, EXPERT_FF), BF16), pltpu.VMEM((D_MODEL, EXPERT_FF), BF16),
                        pltpu.VMEM((EXPERT_FF, D_MODEL), BF16), pltpu.SemaphoreType.DMA((2, 3))],
    )
    return pl.pallas_call(
        functools.partial(_gmm_kernel, layer=layer),
        grid_spec=grid_spec,
        out_shape=jax.ShapeDtypeStruct((rows, D_MODEL), F32),
        compiler_params=_cparams(1),
        name="moe_grouped_mlp",
    )(*visits, xs, w_gate, w_up, w_down)


def _visit_list(counts, n_rows, tm, n_visits):
    ends = jnp.cumsum(counts)
    offs = ends - counts
    first_tile = offs // tm
    last_tile = jnp.where(counts > 0, (ends - 1) // tm, first_tile - 1)
    n_t = last_tile - first_tile + 1
    v_end = jnp.cumsum(n_t)
    v_start = v_end - n_t
    total = v_end[-1]
    vi = jnp.arange(n_visits, dtype=I32)
    e = jnp.minimum(jnp.sum(v_end[None, :] <= vi[:, None], axis=1), N_EXPERTS - 1).astype(I32)
    active = vi < total
    last_e = jnp.max(jnp.where(counts > 0, jnp.arange(N_EXPERTS, dtype=I32), 0))
    e = jnp.where(active, e, last_e).astype(I32)
    is_e = e[:, None] == jnp.arange(N_EXPERTS, dtype=I32)[None, :]
    pick = lambda table: jnp.sum(jnp.where(is_e, table[None, :], 0), axis=1)
    tile = pick(first_tile) + (vi - pick(v_start))
    tile = jnp.where(active, tile, n_rows // tm - 1).astype(I32)
    lo = jnp.where(active, jnp.maximum(pick(offs), tile * tm) - tile * tm, 0).astype(I32)
    hi = jnp.where(active, jnp.minimum(pick(ends), (tile + 1) * tm) - tile * tm, 0).astype(I32)
    prev_tile = jnp.concatenate([jnp.full((1,), -1, I32), tile[:-1]])
    first = (active & (tile != prev_tile)).astype(I32)
    prev_e = jnp.concatenate([jnp.full((1,), -1, I32), e[:-1]])
    new_e = (e != prev_e).astype(I32)
    slot = ((jnp.cumsum(new_e) - 1) % 2).astype(I32)
    new_at = jnp.where(new_e == 1, vi, n_visits)
    next_new = jnp.concatenate([lax.cummin(new_at[::-1])[::-1][1:], jnp.full((1,), n_visits, I32)])
    next_e = jnp.sum(jnp.where(next_new[:, None] == vi[None, :], e[None, :], 0), axis=1)
    next_e = jnp.where(next_new < n_visits, next_e, -1).astype(I32)
    return tile, e, lo, hi, first, new_e, slot, next_e


def _moe_finish_kernel(pos_hbm, ys_hbm, w_ref, xpk_ref, x_ref, wg_ref, wu_ref, wd_ref, g_ref, b_ref,
                       o_ref, obf_ref, pos_smem, buf, sem_idx, sem, *, tm):
    i = pl.program_id(0)
    cp_pos = pltpu.make_async_copy(pos_hbm.at[i], pos_smem, sem_idx)
    cp_pos.start()
    cp_pos.wait()

    def row_copy(t, kk):
        return pltpu.make_async_copy(ys_hbm.at[pl.ds(pos_smem[kk, t], 1), :], buf.at[kk, pl.ds(t, 1), :], sem)

    def issue(g, carry):
        base = pl.multiple_of(g * ROW_GROUP, ROW_GROUP)
        for u in range(ROW_GROUP):
            for kk in range(TOP_K):
                row_copy(base + u, kk).start()
        return carry

    lax.fori_loop(0, tm // ROW_GROUP, issue, 0)

    xa, xb = _unpack_rows(xpk_ref[...])
    gate = _dg(xa, wg_ref[0:PACK_W, :]) + _dg(xb, wg_ref[PACK_W:, :])
    up = _dg(xa, wu_ref[0:PACK_W, :]) + _dg(xb, wu_ref[PACK_W:, :])
    h = gate * _sigmoid(gate) * up
    y = DEEPNORM_ALPHA * x_ref[...] + _dg(h.astype(BF16), wd_ref[...])

    pltpu.make_async_copy(buf, buf, sem).wait()
    w = w_ref[...]
    routed = w[:, 0:1] * buf[0]
    for kk in range(1, TOP_K):
        routed = routed + w[:, kk:kk + 1] * buf[kk]
    out = _layer_norm(y + routed, g_ref[...], b_ref[...])
    o_ref[...] = out
    obf_ref[...] = out.astype(BF16)


def _moe_finish(pos_tiles, ys, w_tok, x_pk, x, w_gate, w_up, w_down, g, b, tm):
    m = x.shape[0]
    row = lambda n: pl.BlockSpec((tm, n), lambda i: (i, 0))
    vec = pl.BlockSpec((1, D_MODEL), lambda i: (0, 0))
    return pl.pallas_call(
        functools.partial(_moe_finish_kernel, tm=tm),
        grid=(m // tm,),
        in_specs=[pl.BlockSpec(memory_space=pl.ANY), pl.BlockSpec(memory_space=pl.ANY),
                  row(TOP_K), row(PACK_W), row(D_MODEL),
                  pl.BlockSpec((D_MODEL, SHARED_FF), lambda i: (0, 0)),
                  pl.BlockSpec((D_MODEL, SHARED_FF), lambda i: (0, 0)),
                  pl.BlockSpec((SHARED_FF, D_MODEL), lambda i: (0, 0)), vec, vec],
        out_specs=[row(D_MODEL), row(D_MODEL)],
        out_shape=[jax.ShapeDtypeStruct((m, D_MODEL), F32), jax.ShapeDtypeStruct((m, D_MODEL), BF16)],
        scratch_shapes=[pltpu.SMEM((TOP_K, tm), I32), pltpu.VMEM((TOP_K, tm, D_MODEL), F32),
                        pltpu.SemaphoreType.DMA, pltpu.SemaphoreType.DMA],
        compiler_params=_cparams(1),
        name="moe_finish",
    )(pos_tiles, ys, w_tok, x_pk, x, w_gate, w_up, w_down, g, b)


def _pad_cols(w, n):
    return jnp.pad(w, ((0, 0), (0, n - w.shape[1])))


def _in_proj_weight(w_in_l):
    c0 = 6 * BRANCH_W
    rw = c0
    lora = rw + 3 * RWKV_W
    parts = [w_in_l[:, :c0],
             w_in_l[:, c0 + 3 * RWKV_W + DECAY_LORA + AAA_LORA + GATE_LORA:],
             w_in_l[:, rw:lora],
             _pad_cols(w_in_l[:, lora:lora + DECAY_LORA], LORA_PAD),
             _pad_cols(w_in_l[:, lora + DECAY_LORA:lora + DECAY_LORA + AAA_LORA], LORA_PAD),
             w_in_l[:, lora + DECAY_LORA + AAA_LORA:lora + DECAY_LORA + AAA_LORA + GATE_LORA]]
    return jnp.concatenate(parts, axis=1).astype(BF16)


def _rwkv_mix_vector(mix_l):
    lora = 3 * RWKV_W
    parts = [mix_l[:lora],
             jnp.pad(mix_l[lora:lora + DECAY_LORA], (0, LORA_PAD - DECAY_LORA)),
             jnp.pad(mix_l[lora + DECAY_LORA:lora + DECAY_LORA + AAA_LORA], (0, LORA_PAD - AAA_LORA)),
             mix_l[lora + DECAY_LORA + AAA_LORA:]]
    return jnp.concatenate(parts)[None, :]


def _pad_rows(w, n):
    return jnp.pad(w, ((0, n - w.shape[0]), (0, 0))).astype(BF16)


def kernel(x, w_in, conv_w, rwkv_mix, rwkv_w0, rwkv_w2, rwkv_a0, rwkv_a2, rwkv_g2, rwkv_kk, rwkv_ka, rwkv_rk,
           rwkv_lnx_g, rwkv_lnx_b, w_branch, w_o, ln1_g, ln1_b, w_router, router_bias, w_e_gate, w_e_up,
           w_e_down, w_s_gate, w_s_up, w_s_down, ln2_g, ln2_b):
    bsz, seq, d = x.shape
    n_tok = bsz * seq
    n_rows = n_tok * TOP_K
    n_visits = n_rows // GMM_TILE + N_EXPERTS
    io_tile = 256
    xf = x.reshape(n_tok, d)
    xb = xf.astype(BF16)
    row1 = lambda v: v[None, :]
    for l in range(DEPTH):
        w_in_r = _in_proj_weight(w_in[l])
        p_conv = _matmul_cols(xb, w_in_r, COL_CONV, 3 * BRANCH_W, 1024, 1024, F32)
        p_att = _matmul_cols(xb, w_in_r, COL_ATT, 3 * BRANCH_W, 1024, 1024, BF16)
        p_gate = _matmul_cols(xb, w_in_r, COL_GATE, 3 * D_MODEL, 1024, 1024, F32)
        p_rwkv = _matmul_cols(xb, w_in_r, COL_RWKV, RWKV_COLS, 512, 1024, F32)

        y_conv = _short_conv(p_conv, conv_w[l], bsz, seq)
        y_att = _stick_breaking(p_att, bsz, seq)
        y_rwkv = _rwkv_mixer(p_rwkv, _rwkv_mix_vector(rwkv_mix[l]), row1(rwkv_w0[l]),
                             _pad_rows(rwkv_w2[l], LORA_PAD), row1(rwkv_a0[l]), _pad_rows(rwkv_a2[l], LORA_PAD),
                             rwkv_g2[l].astype(BF16), row1(rwkv_kk[l]), row1(rwkv_ka[l]),
                             rwkv_rk[l].reshape(1, RWKV_W), row1(rwkv_lnx_g[l]), row1(rwkv_lnx_b[l]), bsz, seq)

        merged = _merge(y_conv.reshape(n_tok, BRANCH_W), y_att.reshape(n_tok, BRANCH_W),
                        y_rwkv.reshape(n_tok, BRANCH_W), w_branch[l].astype(BF16), p_gate)
        x1, x1_pk = _oproj_ln(merged, w_o[l].astype(BF16), xf, row1(ln1_g[l]), row1(ln1_b[l]))

        top_i, top_w, rank, cnt = _router(x1, w_router[l].T, router_bias[l][:, None])
        counts = cnt[:, 0].astype(I32)
        offs = jnp.cumsum(counts) - counts
        pos_tiles = _positions(offs, top_i, rank, io_tile)
        xs = _dispatch(pos_tiles, x1_pk, io_tile)
        visits = _visit_list(counts, n_rows, GMM_TILE, n_visits)
        ys = _grouped_mlp(visits, xs, w_e_gate, w_e_up, w_e_down, l, n_visits)
        xf, xb = _moe_finish(pos_tiles, ys, top_w.T, x1_pk, x1, w_s_gate[l].astype(BF16),
                             w_s_up[l].astype(BF16), w_s_down[l].astype(BF16), row1(ln2_g[l]), row1(ln2_b[l]),
                             io_tile)
    return xf.reshape(bsz, seq, d)
```

```python
import functools

import jax
import jax.numpy as jnp
from jax import lax
from jax.experimental import pallas as pl
from jax.experimental.pallas import tpu as pltpu

F32 = jnp.float32
BF16 = jnp.bfloat16
I32 = jnp.int32
U32 = jnp.uint32

D_MODEL = 2048
DEPTH = 2
BRANCH_W = 1024
CONV_K = 3
ATT_HEADS = 8
ATT_HD = 128
RWKV_HEADS = 16
RWKV_HD = 64
RWKV_W = RWKV_HEADS * RWKV_HD
DECAY_LORA = 96
AAA_LORA = 96
GATE_LORA = 256
LORA_PAD = 128
N_EXPERTS = 64
TOP_K = 8
N_GROUPS = 8
GROUP_SIZE = N_EXPERTS // N_GROUPS
TOPK_GROUPS = 4
EXPERT_FF = 512
SHARED_FF = 512
ROUTED_SCALE = 2.5
LN_EPS = 1e-5
GN_EPS = 64e-5
DEEPNORM_ALPHA = (2 * DEPTH) ** 0.25

COL_CONV = 0
COL_ATT = 3 * BRANCH_W
COL_GATE = 6 * BRANCH_W
COL_RWKV = COL_GATE + 3 * D_MODEL
RWKV_COLS = 3 * RWKV_W + 2 * LORA_PAD + GATE_LORA
IN_COLS_PAD = COL_RWKV + RWKV_COLS

RWKV_CHUNK = 64
RWKV_GROUP = 4
RWKV_GW = RWKV_GROUP * RWKV_HD
RWKV_GROUPS_PER_STEP = 2
ATT_TILE = 512
ATT_KEY_TILE = 256
ATT_ROW_CHUNK = 32
ATT_UNDERFLOW_LOG = -105.0
GMM_TILE = 256
LANES = 128
ROW_GROUP = 8
PACK_W = D_MODEL // 2

VMEM_LIMIT = 56 * 1024 * 1024

NN = ((1,), (0,))
NT = ((1,), (1,))


def _cparams(n_axes):
    return pltpu.CompilerParams(dimension_semantics=("arbitrary",) * n_axes,
                                vmem_limit_bytes=VMEM_LIMIT)


def _dg(a, b, dims=NN):
    return lax.dot_general(a, b, (dims, ((), ())), preferred_element_type=F32)


def _split2(x):
    hi = x.astype(BF16)
    lo = (x - hi.astype(F32)).astype(BF16)
    return hi, lo


def _split3(x):
    hi = x.astype(BF16)
    r1 = x - hi.astype(F32)
    mid = r1.astype(BF16)
    lo = (r1 - mid.astype(F32)).astype(BF16)
    return hi, mid, lo


def _mm3(a, b, dims=NN):
    ah, al = _split2(a)
    bh, bl = _split2(b)
    return _dg(ah, bh, dims) + _dg(ah, bl, dims) + _dg(al, bh, dims)


def _sigmoid(x):
    return 1.0 / (1.0 + jnp.exp(-x))


def _softplus(x):
    return jnp.maximum(x, 0.0) + jnp.log(1.0 + jnp.exp(-jnp.abs(x)))


def _mm_kernel(x_ref, w_ref, o_ref):
    o_ref[...] = _dg(x_ref[...], w_ref[...]).astype(o_ref.dtype)


def _matmul_cols(x, w, col_off, ncols, tn, tm, out_dtype):
    m, k = x.shape
    off = col_off // tn
    return pl.pallas_call(
        _mm_kernel,
        grid=(ncols // tn, m // tm),
        in_specs=[pl.BlockSpec((tm, k), lambda j, i: (i, 0)),
                  pl.BlockSpec((k, tn), lambda j, i: (0, off + j))],
        out_specs=pl.BlockSpec((tm, tn), lambda j, i: (i, j)),
        out_shape=jax.ShapeDtypeStruct((m, ncols), out_dtype),
        compiler_params=_cparams(2),
        name="in_proj",
    )(x, w)


def _conv_kernel(gb_ref, gc_ref, hc_ref, gch_ref, hch_ref, cw_ref, o_ref):
    i = pl.program_id(1)
    u = gc_ref[...] * hc_ref[...]
    uh = gch_ref[...] * hch_ref[...]
    uh = jnp.where(i > 0, uh, 0.0)
    row = lax.broadcasted_iota(I32, u.shape, 0)
    u1 = jnp.where(row == 0, uh[7:8, :], pltpu.roll(u, 1, axis=0))
    u2 = pltpu.roll(u, 2, axis=0)
    u2 = jnp.where(row == 0, uh[6:7, :], jnp.where(row == 1, uh[7:8, :], u2))
    cw = cw_ref[...]
    y = gb_ref[...] * (cw[2:3, :] * u + cw[1:2, :] * u1 + cw[0:1, :] * u2)
    o_ref[...] = y.astype(o_ref.dtype)


def _short_conv(p_conv, conv_w, bsz, seq, ts=512):
    p3 = p_conv.reshape(bsz, seq, 3 * BRANCH_W)
    hb = ts // 8
    halo = lambda c: pl.BlockSpec((None, 8, BRANCH_W),
                                  lambda b, i: (b, jnp.maximum(i * hb - 1, 0), c))
    main = lambda c: pl.BlockSpec((None, ts, BRANCH_W), lambda b, i: (b, i, c))
    return pl.pallas_call(
        _conv_kernel,
        grid=(bsz, seq // ts),
        in_specs=[main(0), main(1), main(2), halo(1), halo(2),
                  pl.BlockSpec((CONV_K, BRANCH_W), lambda b, i: (0, 0))],
        out_specs=pl.BlockSpec((None, ts, BRANCH_W), lambda b, i: (b, i, 0)),
        out_shape=jax.ShapeDtypeStruct((bsz, seq, BRANCH_W), BF16),
        compiler_params=_cparams(2),
        name="short_conv",
    )(p3, p3, p3, p3, p3, conv_w)


def _attn_kernel(q_ref, k_ref, v_ref, o_ref, z_s, lb_s, hl_s, tail_s, w_s, c_s, rs_s, acc_s, *, tq, tk, scale):
    i = pl.program_id(2)
    n_sub = tq // tk
    rc = ATT_ROW_CHUNK
    suffix = (lax.broadcasted_iota(I32, (tk, tk), 0)
              > lax.broadcasted_iota(I32, (tk, tk), 1)).astype(BF16)
    suffix2 = jnp.concatenate([suffix, suffix], axis=0)
    r_io = lax.broadcasted_iota(I32, (rc, tk), 0)
    c_io = lax.broadcasted_iota(I32, (rc, tk), 1)
    acc_s[...] = jnp.zeros_like(acc_s)
    c_s[...] = jnp.zeros_like(c_s)
    n_chunks = tq // rc

    def key_rows(j):
        return pl.ds(pl.multiple_of(j * tk, tk), tk)

    def logits(s, diag_off):
        def step(r, carry):
            rows = pl.ds(pl.multiple_of(r * rc, rc), rc)
            z = z_s[s, rows, :] * scale
            log_beta = jnp.minimum(z, 0.0) - jnp.log(1.0 + jnp.exp(-jnp.abs(z)))
            log_1mb = log_beta - z
            if diag_off is not None:
                causal = (c_io + diag_off * tk) < (r_io + r * rc)
                log_1mb = jnp.where(causal, log_1mb, 0.0)
            hi, lo = _split2(log_1mb)
            hl_s[s, rows, 0:tk] = hi
            hl_s[s, rows, tk:2 * tk] = lo
            lb_s[s, rows, :] = log_beta
            rs_s[s, rows, :] = jnp.sum(log_1mb, axis=1, keepdims=True)
            return carry
        lax.fori_loop(0, n_chunks, step, 0, unroll=True)

    def weights(s, diag_off):
        def step(r, carry):
            rows = pl.ds(pl.multiple_of(r * rc, rc), rc)
            w = jnp.exp(lb_s[s, rows, :] + tail_s[s, rows, :] + c_s[rows, :])
            if diag_off is not None:
                causal = (c_io + diag_off * tk) < (r_io + r * rc)
                w = jnp.where(causal, w, 0.0)
            w_s[s, rows, :] = w.astype(BF16)
            return carry
        lax.fori_loop(0, n_chunks, step, 0, unroll=True)

    def block_pair(j_hi, diag):
        js = (j_hi, j_hi - 1)
        offs = (1, 0) if diag else (None, None)
        for s in range(2):
            z_s[s] = _dg(q_ref[...], k_ref[key_rows(js[s]), :], NT)
        for s in range(2):
            logits(s, offs[s])
        for s in range(2):
            tail_s[s] = _dg(hl_s[s], suffix2)
        for s in range(2):
            weights(s, offs[s])
            c_s[...] += rs_s[s]
        for s in range(2):
            acc_s[...] += _dg(w_s[s], v_ref[key_rows(js[s]), :])

    block_pair(n_sub * i + 1, True)

    def more(state):
        jj, c_max = state
        return jnp.logical_and(jj < i, c_max > ATT_UNDERFLOW_LOG)

    def below(state):
        jj, _ = state
        block_pair(n_sub * i - 1 - 2 * jj, False)
        return jj + 1, jnp.max(c_s[...])

    lax.while_loop(more, below, (jnp.int32(0), jnp.max(c_s[...])))
    o_ref[...] = acc_s[...].astype(o_ref.dtype)


def _stick_breaking(p_att, bsz, seq):
    p3 = p_att.reshape(bsz, seq, 3 * BRANCH_W)
    tq = ATT_TILE
    tk = ATT_KEY_TILE
    assert tq == 2 * tk, "the attention kernel walks the keys in pairs of blocks"
    kern = functools.partial(_attn_kernel, tq=tq, tk=tk, scale=ATT_HD ** -0.5)
    return pl.pallas_call(
        kern,
        grid=(bsz, ATT_HEADS, seq // tq),
        in_specs=[pl.BlockSpec((None, tq, ATT_HD), lambda b, h, i: (b, i, h)),
                  pl.BlockSpec((None, seq, ATT_HD), lambda b, h, i: (b, 0, ATT_HEADS + h)),
                  pl.BlockSpec((None, seq, ATT_HD), lambda b, h, i: (b, 0, 2 * ATT_HEADS + h))],
        out_specs=pl.BlockSpec((None, tq, ATT_HD), lambda b, h, i: (b, i, h)),
        out_shape=jax.ShapeDtypeStruct((bsz, seq, BRANCH_W), BF16),
        scratch_shapes=[pltpu.VMEM((2, tq, tk), F32), pltpu.VMEM((2, tq, tk), F32),
                        pltpu.VMEM((2, tq, 2 * tk), BF16), pltpu.VMEM((2, tq, tk), F32),
                        pltpu.VMEM((2, tq, tk), BF16), pltpu.VMEM((tq, 1), F32),
                        pltpu.VMEM((2, tq, 1), F32), pltpu.VMEM((tq, ATT_HD), F32)],
        compiler_params=_cparams(3),
        name="stick_breaking_attention",
    )(p3, p3, p3)


def _head_sum(x, ones_bd):
    hi, lo = _split2(x)
    outs = []
    for c in range(x.shape[1] // RWKV_GW):
        sl = slice(RWKV_GW * c, RWKV_GW * (c + 1))
        outs.append(_dg(hi[:, sl], ones_bd) + _dg(lo[:, sl], ones_bd))
    return outs[0] if len(outs) == 1 else jnp.concatenate(outs, axis=1)


def _head_ones():
    r = lax.broadcasted_iota(I32, (RWKV_GW, RWKV_GW), 0) // RWKV_HD
    c = lax.broadcasted_iota(I32, (RWKV_GW, RWKV_GW), 1) // RWKV_HD
    return (r == c).astype(BF16)


RWKV_LEVELS = RWKV_CHUNK.bit_length() - 1


def _rwkv_masks():
    r = jnp.arange(RWKV_GW)[:, None]
    c = jnp.arange(RWKV_GW)[None, :]
    bd = (r // RWKV_CHUNK) == (c // RWKV_HD)
    planes = [bd, bd & (r > c), bd & (r >= c)]
    for m in range(RWKV_LEVELS):
        s = 1 << m
        planes.append((r // (2 * s) == c // (2 * s)) & (r % (2 * s) >= s) & (c % (2 * s) < s))
    return jnp.stack(planes).astype(F32), bd.astype(BF16)


def _rwkv_scan_kernel(pr_ref, pk_ref, pv_ref, pl_ref, mr_ref, mk_ref, mv_ref, ml_ref, w0_ref, w2_ref, a0_ref,
                      a2_ref, g2_ref, kk_ref, ka_ref, rk_ref, lng_ref, lnb_ref, msk_ref, bdb_ref,
                      o_ref, s_ref, prev_ref):
    @pl.when(pl.program_id(1) == 0)
    def _():
        s_ref[...] = jnp.zeros_like(s_ref)
        prev_ref[...] = jnp.zeros_like(prev_ref)

    c_len = RWKV_CHUNK
    gw = RWKV_GW
    n_b = pr_ref.shape[0]
    chains = [(bb, gi) for bb in range(n_b) for gi in range(pr_ref.shape[2] // gw)]
    first_row = lax.broadcasted_iota(I32, (c_len, 1), 0) == 0

    def smap(f, *lists):
        return [f(*xs) for xs in zip(*lists)]

    def shifted_mix(ref, mix_ref, slot):
        outs = []
        for bb in range(n_b):
            x = ref[bb]
            shifted = jnp.where(first_row, prev_ref[bb, slot], pltpu.roll(x, 1, axis=0))
            prev_ref[bb, slot] = x[c_len - 1:c_len, :]
            outs.append(x + (shifted - x) * mix_ref[...])
        return outs

    ps_r, ps_k, ps_v = shifted_mix(pr_ref, mr_ref, 0), shifted_mix(pk_ref, mk_ref, 1), shifted_mix(pv_ref, mv_ref, 2)
    ps_l = shifted_mix(pl_ref, ml_ref, 3)
    lora_w = [jnp.tanh(x[:, 0:LORA_PAD]).astype(BF16) for x in ps_l]
    lora_a = [x[:, LORA_PAD:2 * LORA_PAD].astype(BF16) for x in ps_l]
    lora_g = [_sigmoid(x[:, 2 * LORA_PAD:]).astype(BF16) for x in ps_l]
    ones_hd = _head_ones()
    r, lw, k, v, a, b, g = [], [], [], [], [], [], []
    for bb, gi in chains:
        sl = slice(gi * gw, (gi + 1) * gw)
        w_log = -_softplus(-(w0_ref[:, sl] + _dg(lora_w[bb], w2_ref[:, sl]))) - 0.5
        rate = _sigmoid(a0_ref[:, sl] + _dg(lora_a[bb], a2_ref[:, sl]))
        kk = ps_k[bb][:, sl] * kk_ref[:, sl]
        kk = kk / jnp.maximum(jnp.sqrt(_head_sum(kk * kk, ones_hd)), 1e-12)
        r.append(ps_r[bb][:, sl])
        lw.append(-jnp.exp(w_log))
        k.append(ps_k[bb][:, sl] * (1.0 + (rate - 1.0) * ka_ref[:, sl]))
        v.append(ps_v[bb][:, sl])
        a.append(-kk)
        b.append(kk * rate)
        g.append(_dg(lora_g[bb], g2_ref[:, sl]))

    def par(ref):
        return [ref[:, gi * gw:(gi + 1) * gw] for _, gi in chains]

    def tile_heads(x):
        return jnp.concatenate([x.astype(BF16)] * RWKV_GROUP, axis=0)

    def unstack(z):
        out = z[0:c_len]
        for h in range(1, RWKV_GROUP):
            out = out + z[h * c_len:(h + 1) * c_len]
        return out

    tril_c =(lax.broadcasted_iota(I32, (c_len, c_len), 0)
              >= lax.broadcasted_iota(I32, (c_len, c_len), 1)).astype(BF16)

    def cum_decay(x):
        hi, mid, lo = _split3(x)
        return _dg(tril_c, hi) + _dg(tril_c, mid) + _dg(tril_c, lo)

    cum = smap(cum_decay, lw)
    cum_end = smap(lambda c: c[c_len - 1:c_len, :], cum)
    e_inv = smap(lambda c: jnp.exp(-c), cum)
    e_rem = smap(lambda ce, c: jnp.exp(ce - c), cum_end, cum)

    bd = msk_ref[0]
    strict = msk_ref[1]
    incl = msk_ref[2]
    head_b = bdb_ref[...]
    a_s = smap(lambda x, c, l: tile_heads(x * jnp.exp(c - l)) * head_b, a, cum, lw)
    r_s = smap(lambda x, c: tile_heads(x * jnp.exp(c)) * head_b, r, cum)
    b_x = smap(lambda x, e: tile_heads(x * e), b, e_inv)
    k_x = smap(lambda x, e: tile_heads(x * e), k, e_inv)
    v_x = smap(tile_heads, v)

    a_ab = smap(lambda p, q: _dg(p, q, NT) * strict, a_s, b_x)
    a_ak = smap(lambda p, q: (_dg(p, q, NT) * strict).astype(BF16), a_s, k_x)
    a_rb = smap(lambda p, q: (_dg(p, q, NT) * incl).astype(BF16), r_s, b_x)
    a_rk = smap(lambda p, q: (_dg(p, q, NT) * incl).astype(BF16), r_s, k_x)

    inv = smap(lambda m: (incl - strict) + m * msk_ref[3], a_ab)
    for lvl in range(1, RWKV_LEVELS):
        inv_b = smap(lambda t: t.astype(BF16), inv)
        low = smap(lambda m: (m * msk_ref[3 + lvl]).astype(BF16), a_ab)
        mid = smap(lambda t, lo_: _dg(t, lo_).astype(BF16), inv_b, low)
        inv = smap(lambda t, md, tb: t + _dg(md, tb), inv, mid, inv_b)
    inv_b = smap(lambda t: t.astype(BF16), inv)

    s0 = [s_ref[bb, gi] for bb, gi in chains]
    s0_b = smap(lambda s: s.astype(BF16), s0)
    rhs = smap(lambda p, s, m, vv: (_dg(p, s, NT) + _dg(m, vv)).astype(BF16), a_s, s0_b, a_ak, v_x)
    u = smap(_dg, inv_b, rhs)
    u_b = smap(lambda t: t.astype(BF16), u)
    y = smap(lambda p, s, m1, ub, m2, vv: unstack((_dg(p, s, NT) + _dg(m1, ub) + _dg(m2, vv)) * bd),
             r_s, s0_b, a_rb, u_b, a_rk, v_x)
    uv_t = smap(lambda uu, vv: jnp.concatenate([unstack(uu * bd), vv], axis=0).T.astype(BF16), u, v)
    bk_h = smap(lambda bb_, kk_, e: jnp.concatenate([bb_ * e, kk_ * e], axis=0).astype(BF16), b, k, e_rem)
    s_new = smap(lambda s, ce, p, q: s * jnp.exp(ce) + _dg(p, q) * bd, s0, cum_end, uv_t, bk_h)
    for (bb, gi), s in zip(chains, s_new):
        s_ref[bb, gi] = s

    inv_n = 1.0 / RWKV_HD
    d = smap(lambda yy: yy - _head_sum(yy, ones_hd) * inv_n, y)
    var = smap(lambda dd: _head_sum(dd * dd, ones_hd) * inv_n, d)
    yn = smap(lambda dd, vr, gg, bb_: dd * lax.rsqrt(vr + GN_EPS) * gg + bb_, d, var, par(lng_ref), par(lnb_ref))
    bonus = smap(lambda rr, kk_, rk, vv: _head_sum(rr * kk_ * rk, ones_hd) * vv, r, k, par(rk_ref), v)
    for (bb, gi), yy, bo, gg in zip(chains, yn, bonus, g):
        o_ref[bb, :, gi * gw:(gi + 1) * gw] = ((yy + bo) * gg).astype(o_ref.dtype)


def _rwkv_mixer(p_rwkv, mix, w0, w2, a0, a2, g2, k_k, k_a, r_k, lnx_g, lnx_b, bsz, seq):
    p3 = p_rwkv.reshape(bsz, seq, RWKV_COLS)
    n_groups = RWKV_W // RWKV_GW
    gps = RWKV_GROUPS_PER_STEP
    bw = gps * RWKV_GW
    n_cb = RWKV_W // bw
    lora_cb = 3 * n_cb
    assert RWKV_COLS == (lora_cb + 1) * bw
    tok = lambda cb0: pl.BlockSpec((bsz, RWKV_CHUNK, bw), lambda gg, i: (0, i, cb0 + gg))
    tok_lora = pl.BlockSpec((bsz, RWKV_CHUNK, bw), lambda gg, i: (0, i, lora_cb))
    mixv = lambda cb0: pl.BlockSpec((1, bw), lambda gg, i: (0, cb0 + gg))
    mix_lora = pl.BlockSpec((1, bw), lambda gg, i: (0, lora_cb))
    vec = pl.BlockSpec((1, bw), lambda gg, i: (0, gg))
    mat = lambda rows: pl.BlockSpec((rows, bw), lambda gg, i: (0, gg))
    masks, head_mask = _rwkv_masks()
    return pl.pallas_call(
        _rwkv_scan_kernel,
        grid=(n_groups // gps, seq // RWKV_CHUNK),
        in_specs=[tok(0), tok(n_cb), tok(2 * n_cb), tok_lora,
                  mixv(0), mixv(n_cb), mixv(2 * n_cb), mix_lora,
                  vec, mat(LORA_PAD), vec, mat(LORA_PAD), mat(GATE_LORA), vec, vec, vec, vec, vec,
                  pl.BlockSpec((3 + RWKV_LEVELS, RWKV_GW, RWKV_GW), lambda gg, i: (0, 0, 0)),
                  pl.BlockSpec((RWKV_GW, RWKV_GW), lambda gg, i: (0, 0))],
        out_specs=pl.BlockSpec((bsz, RWKV_CHUNK, bw), lambda gg, i: (0, i, gg)),
        out_shape=jax.ShapeDtypeStruct((bsz, seq, RWKV_W), BF16),
        scratch_shapes=[pltpu.VMEM((bsz, gps, RWKV_GW, RWKV_GW), F32),
                        pltpu.VMEM((bsz, 4, 1, bw), F32)],
        compiler_params=_cparams(2),
        name="rwkv_mixer",
    )(p3, p3, p3, p3, mix, mix, mix, mix, w0, w2, a0, a2, g2, k_k, k_a, r_k, lnx_g, lnx_b, masks, head_mask)


def _merge_kernel(yc_ref, ya_ref, yr_ref, wb_ref, g0_ref, g1_ref, g2_ref, o_ref):
    acc = None
    for n, (y_ref, g_ref) in enumerate(((yc_ref, g0_ref), (ya_ref, g1_ref), (yr_ref, g2_ref))):
        term = _sigmoid(g_ref[...]) * _dg(y_ref[...], wb_ref[n])
        acc = term if acc is None else acc + term
    o_ref[...] = acc.astype(o_ref.dtype)


def _merge(y_conv, y_att, y_rwkv, w_branch, p_gate, tm=512, tn=1024):
    m = y_conv.shape[0]
    nj = D_MODEL // tn
    ysp = pl.BlockSpec((tm, BRANCH_W), lambda j, i: (i, 0))
    gsp = lambda n: pl.BlockSpec((tm, tn), lambda j, i: (i, n * nj + j))
    return pl.pallas_call(
        _merge_kernel,
        grid=(nj, m // tm),
        in_specs=[ysp, ysp, ysp, pl.BlockSpec((3, BRANCH_W, tn), lambda j, i: (0, 0, j)),
                  gsp(0), gsp(1), gsp(2)],
        out_specs=pl.BlockSpec((tm, tn), lambda j, i: (i, j)),
        out_shape=jax.ShapeDtypeStruct((m, D_MODEL), BF16),
        compiler_params=_cparams(2),
        name="branch_merge",
    )(y_conv, y_att, y_rwkv, w_branch, p_gate, p_gate, p_gate)


def _layer_norm(y, g, b):
    mu = jnp.mean(y, axis=-1, keepdims=True)
    d = y - mu
    var = jnp.mean(d * d, axis=-1, keepdims=True)
    return d * lax.rsqrt(var + LN_EPS) * g + b


def _pack_rows(x):
    hi = lax.bitcast_convert_type(x[:, :PACK_W].astype(BF16).astype(F32), U32)
    lo = lax.bitcast_convert_type(x[:, PACK_W:].astype(BF16).astype(F32), U32)
    return hi | (lo >> 16)


def _unpack_rows(w):
    hi = lax.bitcast_convert_type(w & jnp.uint32(0xFFFF0000), F32).astype(BF16)
    lo = lax.bitcast_convert_type(w << 16, F32).astype(BF16)
    return hi, lo


def _oproj_ln_kernel(m_ref, w_ref, x_ref, g_ref, b_ref, o_ref, opk_ref):
    y = DEEPNORM_ALPHA * x_ref[...] + _dg(m_ref[...], w_ref[...])
    out = _layer_norm(y, g_ref[...], b_ref[...])
    o_ref[...] = out
    opk_ref[...] = _pack_rows(out)


def _oproj_ln(merged, w_o, x, g, b, tm=256):
    m = x.shape[0]
    row = lambda n: pl.BlockSpec((tm, n), lambda i: (i, 0))
    vec = pl.BlockSpec((1, D_MODEL), lambda i: (0, 0))
    return pl.pallas_call(
        _oproj_ln_kernel,
        grid=(m // tm,),
        in_specs=[row(D_MODEL), pl.BlockSpec((D_MODEL, D_MODEL), lambda i: (0, 0)), row(D_MODEL), vec, vec],
        out_specs=[row(D_MODEL), row(PACK_W)],
        out_shape=[jax.ShapeDtypeStruct((m, D_MODEL), F32), jax.ShapeDtypeStruct((m, PACK_W), U32)],
        compiler_params=_cparams(1),
        name="out_proj_ln",
    )(merged, w_o, x, g, b)


def _first_max(x, idx, n):
    m = jnp.max(x, axis=0, keepdims=True)
    first = jnp.min(jnp.where(x == m, idx, float(n)), axis=0, keepdims=True)
    return m, first


def _router_kernel(x_ref, wr_ref, bias_ref, ti_ref, tw_ref, rk_ref, cnt_ref, run_ref, *, tm):
    @pl.when(pl.program_id(0) == 0)
    def _():
        run_ref[...] = jnp.zeros_like(run_ref)

    xh, xl = _split2(x_ref[...])
    wh, wl = _split2(wr_ref[...])
    logits = _dg(wh, xh, NT) + _dg(wh, xl, NT) + _dg(wl, xh, NT)
    scores = _sigmoid(logits)
    biased = scores + bias_ref[...]
    neg = -jnp.inf

    sub8 = lax.broadcasted_iota(I32, (GROUP_SIZE, tm), 0).astype(F32)
    grp_rows = []
    for gi in range(N_GROUPS):
        blk = biased[gi * GROUP_SIZE:(gi + 1) * GROUP_SIZE, :]
        m1, f1 = _first_max(blk, sub8, GROUP_SIZE)
        m2 = jnp.max(jnp.where(sub8 == f1, neg, blk), axis=0, keepdims=True)
        grp_rows.append(m1 + m2)
    cur = jnp.concatenate(grp_rows, axis=0)
    grp_io = lax.broadcasted_iota(I32, (N_GROUPS, tm), 0).astype(F32)
    keep = jnp.zeros((N_GROUPS, tm), F32)
    for _ in range(TOPK_GROUPS):
        _, f = _first_max(cur, grp_io, N_GROUPS)
        hit = grp_io == f
        keep = jnp.where(hit, 1.0, keep)
        cur = jnp.where(hit, neg, cur)
    exp_keep = jnp.concatenate(
        [jnp.broadcast_to(keep[gi:gi + 1, :], (GROUP_SIZE, tm)) for gi in range(N_GROUPS)], axis=0)
    cand = jnp.where(exp_keep > 0.5, biased, neg)

    exp_io = lax.broadcasted_iota(I32, (N_EXPERTS, tm), 0).astype(F32)
    sel = jnp.zeros((N_EXPERTS, tm), F32)
    idxs, vals = [], []
    for _ in range(TOP_K):
        _, f = _first_max(cand, exp_io, N_EXPERTS)
        hit = exp_io == f
        idxs.append(f)
        vals.append(jnp.sum(jnp.where(hit, scores, 0.0), axis=0, keepdims=True))
        sel = jnp.where(hit, 1.0, sel)
        cand = jnp.where(hit, neg, cand)
    top_s = jnp.concatenate(vals, axis=0)
    ti_ref[...] = jnp.concatenate(idxs, axis=0).astype(I32)
    tw_ref[...] = top_s / jnp.sum(top_s, axis=0, keepdims=True) * ROUTED_SCALE

    before = (lax.broadcasted_iota(I32, (tm, tm), 0) < lax.broadcasted_iota(I32, (tm, tm), 1)).astype(BF16)
    rank_all = _dg(sel.astype(BF16), before) + run_ref[...][:, 0:1]
    ranks = [jnp.sum(jnp.where(exp_io == f, rank_all, 0.0), axis=0, keepdims=True) for f in idxs]
    rk_ref[...] = jnp.concatenate(ranks, axis=0).astype(I32)
    run_ref[...] = run_ref[...] + jnp.sum(sel, axis=1, keepdims=True)
    cnt_ref[...] = run_ref[...]


def _router(x, wr_t, bias, tm=512):
    m = x.shape[0]
    slot = pl.BlockSpec((TOP_K, tm), lambda i: (0, i))
    return pl.pallas_call(
        functools.partial(_router_kernel, tm=tm),
        grid=(m // tm,),
        in_specs=[pl.BlockSpec((tm, D_MODEL), lambda i: (i, 0)),
                  pl.BlockSpec((N_EXPERTS, D_MODEL), lambda i: (0, 0)),
                  pl.BlockSpec((N_EXPERTS, 1), lambda i: (0, 0))],
        out_specs=[slot, slot, slot, pl.BlockSpec((N_EXPERTS, LANES), lambda i: (0, 0))],
        out_shape=[jax.ShapeDtypeStruct((TOP_K, m), I32), jax.ShapeDtypeStruct((TOP_K, m), F32),
                   jax.ShapeDtypeStruct((TOP_K, m), I32), jax.ShapeDtypeStruct((N_EXPERTS, LANES), F32)],
        scratch_shapes=[pltpu.VMEM((N_EXPERTS, LANES), F32)],
        compiler_params=_cparams(1),
        name="moe_router",
    )(x, wr_t, bias)


def _positions_kernel(offs_ref, ti_ref, rk_ref, o_ref):
    ti = ti_ref[...]
    acc = rk_ref[...]
    for e in range(N_EXPERTS):
        acc = acc + jnp.where(ti == e, offs_ref[e], 0)
    o_ref[...] = acc


def _positions(offs, top_i, rank, tm):
    m = top_i.shape[1]
    slot = pl.BlockSpec((TOP_K, tm), lambda i, offs_ref: (0, i))
    return pl.pallas_call(
        _positions_kernel,
        grid_spec=pltpu.PrefetchScalarGridSpec(
            num_scalar_prefetch=1, grid=(m // tm,), in_specs=[slot, slot],
            out_specs=pl.BlockSpec((None, TOP_K, tm), lambda i, offs_ref: (i, 0, 0))),
        out_shape=jax.ShapeDtypeStruct((m // tm, TOP_K, tm), I32),
        compiler_params=_cparams(1),
        name="moe_positions",
    )(offs, top_i, rank)


def _dispatch_kernel(pos_hbm, x_ref, xs_hbm, pos_smem, sem_idx, sem, *, tm):
    i = pl.program_id(0)
    cp = pltpu.make_async_copy(pos_hbm.at[i], pos_smem, sem_idx)
    cp.start()
    cp.wait()

    def row_copy(t, kk):
        return pltpu.make_async_copy(x_ref.at[pl.ds(t, 1), :], xs_hbm.at[pl.ds(pos_smem[kk, t], 1), :], sem)

    def issue(g, carry):
        base = pl.multiple_of(g * ROW_GROUP, ROW_GROUP)
        for u in range(ROW_GROUP):
            for kk in range(TOP_K):
                row_copy(base + u, kk).start(priority=kk % 2)
        return carry

    lax.fori_loop(0, tm // ROW_GROUP, issue, 0)
    for kk in range(TOP_K):
        pltpu.make_async_copy(x_ref, xs_hbm.at[pl.ds(0, tm), :], sem).wait()


def _dispatch(pos_tiles, x_pk, tm):
    n_tiles = pos_tiles.shape[0]
    m = x_pk.shape[0]
    return pl.pallas_call(
        functools.partial(_dispatch_kernel, tm=tm),
        grid=(n_tiles,),
        in_specs=[pl.BlockSpec(memory_space=pl.ANY),
                  pl.BlockSpec((tm, PACK_W), lambda i: (i, 0))],
        out_specs=pl.BlockSpec(memory_space=pl.ANY),
        out_shape=jax.ShapeDtypeStruct((m * TOP_K, PACK_W), U32),
        scratch_shapes=[pltpu.SMEM((TOP_K, tm), I32), pltpu.SemaphoreType.DMA, pltpu.SemaphoreType.DMA],
        compiler_params=_cparams(1),
        name="moe_dispatch",
    )(pos_tiles, x_pk)


def _gmm_kernel(vt_ref, ve_ref, vlo_ref, vhi_ref, vfirst_ref, vnew_ref, vslot_ref, vnext_ref,
                x_ref, wg_hbm, wu_hbm, wd_hbm, o_ref, wg_f, wu_f, wd_f, wg_b, wu_b, wd_b, sem, *, layer):
    vi = pl.program_id(0)
    lo = vlo_ref[vi]
    hi = vhi_ref[vi]
    first = vfirst_ref[vi]
    slot = vslot_ref[vi]

    def weight_copies(e, s):
        return (pltpu.make_async_copy(wg_hbm.at[layer, e], wg_f.at[s], sem.at[s, 0]),
                pltpu.make_async_copy(wu_hbm.at[layer, e], wu_f.at[s], sem.at[s, 1]),
                pltpu.make_async_copy(wd_hbm.at[layer, e], wd_f.at[s], sem.at[s, 2]))

    @pl.when(vi == 0)
    def _():
        for c in weight_copies(ve_ref[0], slot):
            c.start()

    @pl.when(vnew_ref[vi] == 1)
    def _():
        for c in weight_copies(ve_ref[vi], slot):
            c.wait()
        wg_b[...] = wg_f[slot].astype(BF16)
        wu_b[...] = wu_f[slot].astype(BF16)
        wd_b[...] = wd_f[slot].astype(BF16)
        nxt = vnext_ref[vi]

        @pl.when(nxt >= 0)
        def _():
            for c in weight_copies(nxt, 1 - slot):
                c.start()

    @pl.when(hi > lo)
    def _():
        n_half = 2
        hr = x_ref.shape[0] // n_half
        halves = [slice(s * hr, (s + 1) * hr) for s in range(n_half)]
        xs = [_unpack_rows(x_ref[sl, :]) for sl in halves]
        gate = [_dg(xa, wg_b[0:PACK_W, :]) + _dg(xb, wg_b[PACK_W:, :]) for xa, xb in xs]
        up = [_dg(xa, wu_b[0:PACK_W, :]) + _dg(xb, wu_b[PACK_W:, :]) for xa, xb in xs]
        row = lax.broadcasted_iota(I32, (hr, 1), 0)
        h = [jnp.where((row + s * hr >= lo) & (row + s * hr < hi), g * _sigmoid(g) * u, 0.0).astype(BF16)
             for s, (g, u) in enumerate(zip(gate, up))]
        y = [_dg(hh, wd_b[...]) for hh in h]

        @pl.when(first == 1)
        def _():
            for sl, yy in zip(halves, y):
                o_ref[sl, :] = yy

        @pl.when(first != 1)
        def _():
            for sl, yy in zip(halves, y):
                o_ref[sl, :] += yy


def _grouped_mlp(visits, xs, w_gate, w_up, w_down, layer, n_visits):
    rows = xs.shape[0]
    tm = GMM_TILE
    by_tile = lambda vi, vt, *_: (vt[vi], 0)
    grid_spec = pltpu.PrefetchScalarGridSpec(
        num_scalar_prefetch=8,
        grid=(n_visits,),
        in_specs=[pl.BlockSpec((tm, PACK_W), by_tile),
                  pl.BlockSpec(memory_space=pl.ANY), pl.BlockSpec(memory_space=pl.ANY),
                  pl.BlockSpec(memory_space=pl.ANY)],
        out_specs=pl.BlockSpec((tm, D_MODEL), by_tile),
        scratch_shapes=[pltpu.VMEM((2, D_MODEL, EXPERT_FF), F32), pltpu.VMEM((2, D_MODEL, EXPERT_FF), F32),
                        pltpu.VMEM((2, EXPERT_FF, D_MODEL), F32),
                        pltpu.VMEM((D_MODEL, EXPERT_FF), BF16), pltpu.VMEM((D_MODEL, EXPERT_FF), BF16),
                        pltpu.VMEM((EXPERT_FF, D_MODEL), BF16), pltpu.SemaphoreType.DMA((2, 3))],
    )
    return pl.pallas_call(
        functools.partial(_gmm_kernel, layer=layer),
        grid_spec=grid_spec,
        out_shape=jax.ShapeDtypeStruct((rows, D_MODEL), F32),
        compiler_params=_cparams(1),
        name="moe_grouped_mlp",
    )(*visits, xs, w_gate, w_up, w_down)


def _visit_list(counts, n_rows, tm, n_visits):
    ends = jnp.cumsum(counts)
    offs = ends - counts
    first_tile = offs // tm
    last_tile = jnp.where(counts > 0, (ends - 1) // tm, first_tile - 1)
    n_t = last_tile - first_tile + 1
    v_end = jnp.cumsum(n_t)
    v_start = v_end - n_t
    total = v_end[-1]
    vi = jnp.arange(n_visits, dtype=I32)
    e = jnp.minimum(jnp.sum(v_end[None, :] <= vi[:, None], axis=1), N_EXPERTS - 1).astype(I32)
    active = vi < total
    last_e = jnp.max(jnp.where(counts > 0, jnp.arange(N_EXPERTS, dtype=I32), 0))
    e = jnp.where(active, e, last_e).astype(I32)
    is_e = e[:, None] == jnp.arange(N_EXPERTS, dtype=I32)[None, :]
    pick = lambda table: jnp.sum(jnp.where(is_e, table[None, :], 0), axis=1)
    tile = pick(first_tile) + (vi - pick(v_start))
    tile = jnp.where(active, tile, n_rows // tm - 1).astype(I32)
    lo = jnp.where(active, jnp.maximum(pick(offs), tile * tm) - tile * tm, 0).astype(I32)
    hi = jnp.where(active, jnp.minimum(pick(ends), (tile + 1) * tm) - tile * tm, 0).astype(I32)
    prev_tile = jnp.concatenate([jnp.full((1,), -1, I32), tile[:-1]])
    first = (active & (tile != prev_tile)).astype(I32)
    prev_e = jnp.concatenate([jnp.full((1,), -1, I32), e[:-1]])
    new_e = (e != prev_e).astype(I32)
    slot = ((jnp.cumsum(new_e) - 1) % 2).astype(I32)
    new_at = jnp.where(new_e == 1, vi, n_visits)
    next_new = jnp.concatenate([lax.cummin(new_at[::-1])[::-1][1:], jnp.full((1,), n_visits, I32)])
    next_e = jnp.sum(jnp.where(next_new[:, None] == vi[None, :], e[None, :], 0), axis=1)
    next_e = jnp.where(next_new < n_visits, next_e, -1).astype(I32)
    return tile, e, lo, hi, first, new_e, slot, next_e


def _moe_finish_kernel(pos_hbm, ys_hbm, w_ref, xpk_ref, x_ref, wg_ref, wu_ref, wd_ref, g_ref, b_ref,
                       o_ref, obf_ref, pos_smem, buf, sem_idx, sem, *, tm):
    i = pl.program_id(0)
    cp_pos = pltpu.make_async_copy(pos_hbm.at[i], pos_smem, sem_idx)
    cp_pos.start()
    cp_pos.wait()

    def row_copy(t, kk):
        return pltpu.make_async_copy(ys_hbm.at[pl.ds(pos_smem[kk, t], 1), :], buf.at[kk, pl.ds(t, 1), :], sem)

    def issue(g, carry):
        base = pl.multiple_of(g * ROW_GROUP, ROW_GROUP)
        for u in range(ROW_GROUP):
            for kk in range(TOP_K):
                row_copy(base + u, kk).start(priority=kk % 2)
        return carry

    lax.fori_loop(0, tm // ROW_GROUP, issue, 0)

    xa, xb = _unpack_rows(xpk_ref[...])
    gate = _dg(xa, wg_ref[0:PACK_W, :]) + _dg(xb, wg_ref[PACK_W:, :])
    up = _dg(xa, wu_ref[0:PACK_W, :]) + _dg(xb, wu_ref[PACK_W:, :])
    h = gate * _sigmoid(gate) * up
    y = DEEPNORM_ALPHA * x_ref[...] + _dg(h.astype(BF16), wd_ref[...])

    pltpu.make_async_copy(buf, buf, sem).wait()
    w = w_ref[...]
    routed = w[:, 0:1] * buf[0]
    for kk in range(1, TOP_K):
        routed = routed + w[:, kk:kk + 1] * buf[kk]
    out = _layer_norm(y + routed, g_ref[...], b_ref[...])
    o_ref[...] = out
    obf_ref[...] = out.astype(BF16)


def _moe_finish(pos_tiles, ys, w_tok, x_pk, x, w_gate, w_up, w_down, g, b, tm):
    m = x.shape[0]
    row = lambda n: pl.BlockSpec((tm, n), lambda i: (i, 0))
    vec = pl.BlockSpec((1, D_MODEL), lambda i: (0, 0))
    return pl.pallas_call(
        functools.partial(_moe_finish_kernel, tm=tm),
        grid=(m // tm,),
        in_specs=[pl.BlockSpec(memory_space=pl.ANY), pl.BlockSpec(memory_space=pl.ANY),
                  row(TOP_K), row(PACK_W), row(D_MODEL),
                  pl.BlockSpec((D_MODEL, SHARED_FF), lambda i: (0, 0)),
                  pl.BlockSpec((D_MODEL, SHARED_FF), lambda i: (0, 0)),
                  pl.BlockSpec((SHARED_FF, D_MODEL), lambda i: (0, 0)), vec, vec],
        out_specs=[row(D_MODEL), row(D_MODEL)],
        out_shape=[jax.ShapeDtypeStruct((m, D_MODEL), F32), jax.ShapeDtypeStruct((m, D_MODEL), BF16)],
        scratch_shapes=[pltpu.SMEM((TOP_K, tm), I32), pltpu.VMEM((TOP_K, tm, D_MODEL), F32),
                        pltpu.SemaphoreType.DMA, pltpu.SemaphoreType.DMA],
        compiler_params=_cparams(1),
        name="moe_finish",
    )(pos_tiles, ys, w_tok, x_pk, x, w_gate, w_up, w_down, g, b)


def _pad_cols(w, n):
    return jnp.pad(w, ((0, 0), (0, n - w.shape[1])))


def _in_proj_weight(w_in_l):
    c0 = 6 * BRANCH_W
    rw = c0
    lora = rw + 3 * RWKV_W
    parts = [w_in_l[:, :c0],
             w_in_l[:, c0 + 3 * RWKV_W + DECAY_LORA + AAA_LORA + GATE_LORA:],
             w_in_l[:, rw:lora],
             _pad_cols(w_in_l[:, lora:lora + DECAY_LORA], LORA_PAD),
             _pad_cols(w_in_l[:, lora + DECAY_LORA:lora + DECAY_LORA + AAA_LORA], LORA_PAD),
             w_in_l[:, lora + DECAY_LORA + AAA_LORA:lora + DECAY_LORA + AAA_LORA + GATE_LORA]]
    return jnp.concatenate(parts, axis=1).astype(BF16)


def _rwkv_mix_vector(mix_l):
    lora = 3 * RWKV_W
    parts = [mix_l[:lora],
             jnp.pad(mix_l[lora:lora + DECAY_LORA], (0, LORA_PAD - DECAY_LORA)),
             jnp.pad(mix_l[lora + DECAY_LORA:lora + DECAY_LORA + AAA_LORA], (0, LORA_PAD - AAA_LORA)),
             mix_l[lora + DECAY_LORA + AAA_LORA:]]
    return jnp.concatenate(parts)[None, :]


def _pad_rows(w, n):
    return jnp.pad(w, ((0, n - w.shape[0]), (0, 0))).astype(BF16)


def kernel(x, w_in, conv_w, rwkv_mix, rwkv_w0, rwkv_w2, rwkv_a0, rwkv_a2, rwkv_g2, rwkv_kk, rwkv_ka, rwkv_rk,
           rwkv_lnx_g, rwkv_lnx_b, w_branch, w_o, ln1_g, ln1_b, w_router, router_bias, w_e_gate, w_e_up,
           w_e_down, w_s_gate, w_s_up, w_s_down, ln2_g, ln2_b):
    bsz, seq, d = x.shape
    n_tok = bsz * seq
    n_rows = n_tok * TOP_K
    n_visits = n_rows // GMM_TILE + N_EXPERTS
    io_tile = 256
    xf = x.reshape(n_tok, d)
    xb = xf.astype(BF16)
    row1 = lambda v: v[None, :]
    for l in range(DEPTH):
        w_in_r = _in_proj_weight(w_in[l])
        p_conv = _matmul_cols(xb, w_in_r, COL_CONV, 3 * BRANCH_W, 1024, 1024, F32)
        p_att = _matmul_cols(xb, w_in_r, COL_ATT, 3 * BRANCH_W, 1024, 1024, BF16)
        p_gate = _matmul_cols(xb, w_in_r, COL_GATE, 3 * D_MODEL, 1024, 1024, F32)
        p_rwkv = _matmul_cols(xb, w_in_r, COL_RWKV, RWKV_COLS, 512, 1024, F32)

        y_conv = _short_conv(p_conv, conv_w[l], bsz, seq)
        y_att = _stick_breaking(p_att, bsz, seq)
        y_rwkv = _rwkv_mixer(p_rwkv, _rwkv_mix_vector(rwkv_mix[l]), row1(rwkv_w0[l]),
                             _pad_rows(rwkv_w2[l], LORA_PAD), row1(rwkv_a0[l]), _pad_rows(rwkv_a2[l], LORA_PAD),
                             rwkv_g2[l].astype(BF16), row1(rwkv_kk[l]), row1(rwkv_ka[l]),
                             rwkv_rk[l].reshape(1, RWKV_W), row1(rwkv_lnx_g[l]), row1(rwkv_lnx_b[l]), bsz, seq)

        merged = _merge(y_conv.reshape(n_tok, BRANCH_W), y_att.reshape(n_tok, BRANCH_W),
                        y_rwkv.reshape(n_tok, BRANCH_W), w_branch[l].astype(BF16), p_gate)
        x1, x1_pk = _oproj_ln(merged, w_o[l].astype(BF16), xf, row1(ln1_g[l]), row1(ln1_b[l]))

        top_i, top_w, rank, cnt = _router(x1, w_router[l].T, router_bias[l][:, None])
        counts = cnt[:, 0].astype(I32)
        offs = jnp.cumsum(counts) - counts
        pos_tiles = _positions(offs, top_i, rank, io_tile)
        xs = _dispatch(pos_tiles, x1_pk, io_tile)
        visits = _visit_list(counts, n_rows, GMM_TILE, n_visits)
        ys = _grouped_mlp(visits, xs, w_e_gate, w_e_up, w_e_down, l, n_visits)
        xf, xb = _moe_finish(pos_tiles, ys, top_w.T, x1_pk, x1, w_s_gate[l].astype(BF16),
                             w_s_up[l].astype(BF16), w_s_down[l].astype(BF16), row1(ln2_g[l]), row1(ln2_b[l]),
                             io_tile)
    return xf.reshape(bsz, seq, d)
```
